```python
import math
import jax, jax.numpy as jnp
from jax import lax
import numpy as np

D_MODEL = 2048
BATCH = 16
SEQ = 256
DEPTH = 4
DEC_BATCH = 2
DEC_SEQ = 1024
PAST_LEN = 256

GRID_W = 64
CONV_WIDTH = 1024
CONV_HEADS = 16
SSM_WIDTH = 1024
SSM_GROUP = 16
SSM_GROUPS = SSM_WIDTH // SSM_GROUP
SSM_STATE = 64
MIX_WIDTH = CONV_WIDTH + SSM_WIDTH
IN_COLS = 3 * CONV_WIDTH + SSM_WIDTH
D_FF = 5632
EPS = 1e-6
DT_MIN = 1e-3
DT_MAX = 1e-1

kernel_name = 'hybrid_conv_s5_diffusion_step'


def _rms(x, g):
    xf = x.astype(jnp.float32)
    y = xf * lax.rsqrt(jnp.mean(xf * xf, axis=-1, keepdims=True) + EPS)
    return (y * g.astype(jnp.float32)).astype(x.dtype)


def _head_rms(x, g):
    hd = CONV_WIDTH // CONV_HEADS
    xh = x.reshape(x.shape[:-1] + (CONV_HEADS, hd))
    return _rms(xh, g.reshape(CONV_HEADS, hd)).reshape(x.shape)


def _dwconv3(x, w, axis):
    n = x.shape[axis]
    pad = [(0, 0)] * x.ndim
    pad[axis] = (1, 1)
    xp = jnp.pad(x, pad)
    return (lax.slice_in_dim(xp, 0, n, axis=axis) * w[0]
            + lax.slice_in_dim(xp, 1, n + 1, axis=axis) * w[1]
            + lax.slice_in_dim(xp, 2, n + 2, axis=axis) * w[2])


def _diag_scan(bu, a_bar, h0):
    bu = bu.at[:, 0].add(a_bar * h0)
    a = jnp.broadcast_to(a_bar, bu.shape)

    def combine(left, right):
        al, bl = left
        ar, br = right
        return ar * al, ar * bl + br

    _, s = lax.associative_scan(combine, (a, bu), axis=1)
    return s


def _s5(u, p, h0_re, h0_im):
    f32 = jnp.float32
    bsz, L, _ = u.shape
    uc = u.astype(f32).reshape(bsz, L, SSM_GROUPS, SSM_GROUP).astype(jnp.complex64)
    h0 = lax.complex(h0_re.astype(f32), h0_im.astype(f32))
    y = jnp.zeros((bsz, L, SSM_GROUPS, SSM_GROUP), f32)
    fin = []
    for d in range(2):
        a = lax.complex(p['a_re'][d].astype(f32), p['a_im'][d].astype(f32))
        dt = jnp.exp(p['log_dt'][d].astype(f32))[:, None]
        a_bar = jnp.exp(a * dt)
        b_bar = ((a_bar - 1.0) / a)[..., None] * lax.complex(
            p['b_re'][d].astype(f32), p['b_im'][d].astype(f32))
        c_mat = lax.complex(p['c_re'][d].astype(f32), p['c_im'][d].astype(f32))
        ud = uc if d == 0 else jnp.flip(uc, 1)
        bu = jnp.einsum('blgh,gnh->blgn', ud, b_bar)
        s = _diag_scan(bu, a_bar, h0[:, d])
        fin.append(s[:, -1])
        yd = jnp.einsum('blgn,ghn->blgh', s, c_mat).real
        y = y + (yd if d == 0 else jnp.flip(yd, 1))
    y = y.reshape(bsz, L, SSM_WIDTH) + p['d'].astype(f32) * u.astype(f32)
    z = jax.nn.gelu(y).astype(u.dtype)
    out = z * jax.nn.sigmoid(z @ p['w_glu'] + p['b_glu'])
    fin = jnp.stack(fin, axis=1)
    return out, fin.real, fin.imag


def _modulation(cvec, w, b):
    m = jax.nn.silu(cvec) @ w + b
    return jnp.split(m[:, None, :], 6, axis=-1)


def _layer(x, mod, p, rows, h0_re, h0_im):
    sh1, sc1, gt1, sh2, sc2, gt2 = mod
    bsz, L, _ = x.shape
    h = _rms(x, p['g_pre1']) * (1 + sc1) + sh1
    proj = h @ p['w_in']
    gb, gc, hv, u = jnp.split(proj, [CONV_WIDTH, 2 * CONV_WIDTH, 3 * CONV_WIDTH], axis=-1)
    v = gc * hv
    if rows is None:
        v = _dwconv3(v, p['conv_w'], 1)
    else:
        v = _dwconv3(v.reshape(bsz, rows, GRID_W, CONV_WIDTH), p['conv_w'], 2).reshape(bsz, L, CONV_WIDTH)
    conv_out = gb * v
    ssm_out, fin_re, fin_im = _s5(u, p, h0_re, h0_im)
    mixed = jnp.concatenate([_head_rms(conv_out, p['g_conv_out']),
                             _rms(ssm_out, p['g_ssm_out'])], axis=-1) @ p['w_out']
    x = x + gt1 * _rms(mixed, p['g_post1'])
    h = _rms(x, p['g_pre2']) * (1 + sc2) + sh2
    up = h @ p['w_up']
    if rows is None:
        up = _dwconv3(up, p['ffn_conv_w'], 1)
    else:
        up = _dwconv3(up.reshape(bsz, rows, GRID_W, 2 * D_FF), p['ffn_conv_w'], 1).reshape(bsz, L, 2 * D_FF)
    a, g = jnp.split(up, 2, axis=-1)
    ff = (jax.nn.silu(g) * a) @ p['w_down']
    x = x + gt2 * _rms(ff, p['g_post2'])
    return x, fin_re, fin_im


def setup_inputs(seed: int = 0) -> dict:
    key = jax.random.key(seed)
    ks = iter(jax.random.split(key, 40))
    f32 = jnp.float32
    nrm = lambda shape, s: jax.random.normal(next(ks), shape, f32) * s
    gain = lambda shape: 1.0 + 0.02 * jax.random.normal(next(ks), shape, f32)
    n_idx = jnp.arange(SSM_STATE, dtype=f32)
    a_re = -0.5 + nrm((DEPTH, 2, SSM_GROUPS, SSM_STATE), 0.01)
    a_im = math.pi * n_idx + nrm((DEPTH, 2, SSM_GROUPS, SSM_STATE), 0.01)
    log_dt = jax.random.uniform(next(ks), (DEPTH, 2, SSM_GROUPS), f32,
                                math.log(DT_MIN), math.log(DT_MAX))
    return {
        'x_prompt': nrm((BATCH, SEQ, D_MODEL), 1.0),
        'x_sample': nrm((DEC_BATCH, DEC_SEQ, D_MODEL), 1.0),
        'state_ssm_re': nrm((DEC_BATCH, DEPTH, 2, SSM_GROUPS, SSM_STATE), 0.5),
        'state_ssm_im': nrm((DEC_BATCH, DEPTH, 2, SSM_GROUPS, SSM_STATE), 0.5),
        'c': nrm((DEC_BATCH, D_MODEL), 1.0),
        'c_ctx': nrm((D_MODEL,), 1.0),
        'w_ada': nrm((DEPTH, D_MODEL, 6 * D_MODEL), 0.5 * D_MODEL ** -0.5),
        'b_ada': nrm((DEPTH, 6 * D_MODEL), 0.01),
        'g_pre1': gain((DEPTH, D_MODEL)),
        'w_in': nrm((DEPTH, D_MODEL, IN_COLS), D_MODEL ** -0.5),
        'conv_w': nrm((DEPTH, 3, CONV_WIDTH), 3 ** -0.5),
        'ssm_a_re': a_re,
        'ssm_a_im': a_im,
        'ssm_log_dt': log_dt,
        'ssm_b_re': nrm((DEPTH, 2, SSM_GROUPS, SSM_STATE, SSM_GROUP), (2 * SSM_GROUP) ** -0.5),
        'ssm_b_im': nrm((DEPTH, 2, SSM_GROUPS, SSM_STATE, SSM_GROUP), (2 * SSM_GROUP) ** -0.5),
        'ssm_c_re': nrm((DEPTH, 2, SSM_GROUPS, SSM_GROUP, SSM_STATE), (2 * SSM_STATE) ** -0.5),
        'ssm_c_im': nrm((DEPTH, 2, SSM_GROUPS, SSM_GROUP, SSM_STATE), (2 * SSM_STATE) ** -0.5),
        'ssm_d': nrm((DEPTH, SSM_WIDTH), 1.0),
        'w_glu': nrm((DEPTH, SSM_WIDTH, SSM_WIDTH), SSM_WIDTH ** -0.5),
        'b_glu': nrm((DEPTH, SSM_WIDTH), 0.01),
        'g_conv_out': gain((DEPTH, CONV_WIDTH)),
        'g_ssm_out': gain((DEPTH, SSM_WIDTH)),
        'w_out': nrm((DEPTH, MIX_WIDTH, D_MODEL), MIX_WIDTH ** -0.5),
        'g_post1': gain((DEPTH, D_MODEL)),
        'g_pre2': gain((DEPTH, D_MODEL)),
        'w_up': nrm((DEPTH, D_MODEL, 2 * D_FF), D_MODEL ** -0.5),
        'ffn_conv_w': nrm((DEPTH, 3, 2 * D_FF), 3 ** -0.5),
        'w_down': nrm((DEPTH, D_FF, D_MODEL), D_FF ** -0.5),
        'g_post2': gain((DEPTH, D_MODEL)),
    }


def reference(x_prompt, x_sample, state_ssm_re, state_ssm_im, c, c_ctx, w_ada, b_ada,
              g_pre1, w_in, conv_w, ssm_a_re, ssm_a_im, ssm_log_dt, ssm_b_re, ssm_b_im,
              ssm_c_re, ssm_c_im, ssm_d, w_glu, b_glu, g_conv_out, g_ssm_out, w_out,
              g_post1, g_pre2, w_up, ffn_conv_w, w_down, g_post2):
    dec_rows = x_sample.shape[1] // GRID_W
    zero_state = jnp.zeros((x_prompt.shape[0], 2, SSM_GROUPS, SSM_STATE), jnp.float32)
    yp = x_prompt
    ys = x_sample
    new_re = []
    new_im = []
    for l in range(DEPTH):
        p = {
            'g_pre1': g_pre1[l], 'w_in': w_in[l], 'conv_w': conv_w[l],
            'a_re': ssm_a_re[l], 'a_im': ssm_a_im[l], 'log_dt': ssm_log_dt[l],
            'b_re': ssm_b_re[l], 'b_im': ssm_b_im[l], 'c_re': ssm_c_re[l], 'c_im': ssm_c_im[l],
            'd': ssm_d[l], 'w_glu': w_glu[l], 'b_glu': b_glu[l],
            'g_conv_out': g_conv_out[l], 'g_ssm_out': g_ssm_out[l], 'w_out': w_out[l],
            'g_post1': g_post1[l], 'g_pre2': g_pre2[l], 'w_up': w_up[l],
            'ffn_conv_w': ffn_conv_w[l], 'w_down': w_down[l], 'g_post2': g_post2[l],
        }
        mod_ctx = _modulation(c_ctx[None, :], w_ada[l], b_ada[l])
        yp, fin_re, fin_im = _layer(yp, mod_ctx, p, None, zero_state, zero_state)
        new_re.append(fin_re)
        new_im.append(fin_im)
        mod_lat = _modulation(c, w_ada[l], b_ada[l])
        ys, _, _ = _layer(ys, mod_lat, p, dec_rows, state_ssm_re[:, l], state_ssm_im[:, l])
    new_state_re = jnp.stack(new_re, axis=1)
    new_state_im = jnp.stack(new_im, axis=1)
    return (yp, ys, new_state_re, new_state_im)
```

```python
import functools
import math

import jax
import jax.numpy as jnp
from jax import lax
from jax.experimental import pallas as pl
from jax.experimental.pallas import tpu as pltpu

D_MODEL = 2048
BATCH = 16
SEQ = 256
DEPTH = 4
DEC_BATCH = 2
DEC_SEQ = 1024
GRID_W = 64
CONV_WIDTH = 1024
HEAD_DIM = 64
SSM_WIDTH = 1024
SSM_GROUP = 16
SSM_GROUPS = 64
SSM_STATE = 64
D_FF = 5632
EPS = 1e-6

N_CTX = BATCH * SEQ
N_LAT = DEC_BATCH * DEC_SEQ
N_TOK = N_CTX + N_LAT
CHUNK = 16
NC_CTX = SEQ // CHUNK
NC_LAT = DEC_SEQ // CHUNK
ROWS_CTX = NC_CTX * BATCH
LAT_PAD = 8
ROWS_LAT = NC_LAT * LAT_PAD
ROWS_S5 = ROWS_CTX + ROWS_LAT
N_PAIR = SSM_GROUPS // 2
PAIR_IN = 2 * CHUNK * SSM_GROUP
PAIR_ST = 2 * SSM_STATE

TM1 = 1024
TC1 = 256
TM3 = 256
TM4 = 512
TF4 = 512
HALO = GRID_W

VMEM_LIMIT = 56 * 1024 * 1024

f32 = jnp.float32
bf16 = jnp.bfloat16


def _dot(a, b):
    return jnp.dot(a, b, preferred_element_type=f32)


def _split_bf16(x):
    hi = x.astype(bf16)
    lo = (x - hi.astype(f32)).astype(bf16)
    return hi, lo


def _dot3(a, b):
    ah, al = _split_bf16(a)
    bh, bl = _split_bf16(b)
    return _dot(ah, bh) + (_dot(ah, bl) + _dot(al, bh))


def _rms_rows(x, g):
    ms = jnp.mean(x * x, axis=-1, keepdims=True)
    return x * lax.rsqrt(ms + EPS) * g


def _cmul(ar, ai, br, bi):
    return ar * br - ai * bi, ar * bi + ai * br


def _params(*sem):
    return pltpu.CompilerParams(dimension_semantics=sem, vmem_limit_bytes=VMEM_LIMIT)


def _cast_kernel(x_ref, o_ref):
    o_ref[...] = x_ref[...].astype(o_ref.dtype)


def _cast_bf16(w, block_rows):
    rows, cols = w.shape
    return pl.pallas_call(
        _cast_kernel,
        grid=(rows // block_rows,),
        in_specs=[pl.BlockSpec((block_rows, cols), lambda i: (i, 0))],
        out_specs=pl.BlockSpec((block_rows, cols), lambda i: (i, 0)),
        out_shape=jax.ShapeDtypeStruct((rows, cols), bf16),
        compiler_params=_params("arbitrary"),
        name="cast_bf16",
    )(w)


def _mod_kernel(cv_ref, w_ref, b_ref, o_ref):
    cv = cv_ref[...]
    s = cv * jax.nn.sigmoid(cv)
    o_ref[...] = _dot(s.astype(bf16), w_ref[...].astype(bf16)) + b_ref[...]


def _modulation(cvec8, w_ada, b_ada):
    tn = 1024
    n_out = 6 * D_MODEL
    return pl.pallas_call(
        _mod_kernel,
        grid=(DEPTH, n_out // tn),
        in_specs=[
            pl.BlockSpec((8, D_MODEL), lambda l, n: (0, 0)),
            pl.BlockSpec((None, D_MODEL, tn), lambda l, n: (l, 0, n)),
            pl.BlockSpec((None, 1, tn), lambda l, n: (l, 0, n)),
        ],
        out_specs=pl.BlockSpec((None, 8, tn), lambda l, n: (l, 0, n)),
        out_shape=jax.ShapeDtypeStruct((DEPTH, 8, n_out), f32),
        compiler_params=_params("arbitrary", "arbitrary"),
        name="adaln_modulation",
    )(cvec8, w_ada, b_ada.reshape(DEPTH, 1, n_out))


def _disc_kernel(ar_ref, ai_ref, ldt_ref, abr_ref, abi_ref, bfr_ref, bfi_ref):
    ar = ar_ref[...]
    ai = ai_ref[...]
    dt = jnp.exp(ldt_ref[...])
    mag = jnp.exp(ar * dt)
    abr = mag * jnp.cos(ai * dt)
    abi = mag * jnp.sin(ai * dt)
    nr = abr - 1.0
    den = ar * ar + ai * ai
    abr_ref[...] = abr
    abi_ref[...] = abi
    bfr_ref[...] = (nr * ar + abi * ai) / den
    bfi_ref[...] = (abi * ar - nr * ai) / den


def _discretise(a_re, a_im, log_dt):
    shape = (DEPTH * 2, SSM_GROUPS * SSM_STATE)
    spec = pl.BlockSpec(shape, lambda: (0, 0))
    out = jax.ShapeDtypeStruct(shape, f32)
    return pl.pallas_call(
        _disc_kernel,
        in_specs=[spec, spec, spec],
        out_specs=[spec, spec, spec, spec],
        out_shape=[out, out, out, out],
        name="s5_discretise",
    )(a_re.reshape(shape), a_im.reshape(shape),
      jnp.broadcast_to(log_dt[..., None], (DEPTH, 2, SSM_GROUPS, SSM_STATE)).reshape(shape))


def _pow_table(br, bi, kk, shape):
    tr = ti = None
    pr, pi = br, bi
    for bit in range(4):
        sel = ((kk >> bit) & 1) == 1
        fr = jnp.broadcast_to(jnp.where(sel, pr, 1.0), shape)
        fi = jnp.broadcast_to(jnp.where(sel, pi, 0.0), shape)
        if tr is None:
            tr, ti = fr, fi
        else:
            tr, ti = _cmul(tr, ti, fr, fi)
        pr, pi = _cmul(pr, pi, pr, pi)
    return tr, ti


def _prep_kernel(pr_ref, pc_ref, btr_ref, bti_ref, ctr_ref, cti_ref, dt_ref,
                 w1_ref, w2_ref, at_ref):
    n_in1 = CHUNK * SSM_GROUP
    lane_st = lax.broadcasted_iota(jnp.int32, (1, PAIR_ST), 1)
    first_grp_lane = lane_st < SSM_STATE
    row_st = lax.broadcasted_iota(jnp.int32, (PAIR_ST, 1), 0)
    first_grp_row = row_st < SSM_STATE
    row256 = lax.broadcasted_iota(jnp.int32, (n_in1, 1), 0)
    lane256 = lax.broadcasted_iota(jnp.int32, (1, n_in1), 1)
    step_of_row = row256 >> 4
    step_of_lane = lane256 >> 4

    g_tab = [[None, None], [None, None]]
    for d in range(2):
        abr = pr_ref[4 * d + 0:4 * d + 1, :]
        abi = pr_ref[4 * d + 1:4 * d + 2, :]
        bfr = pr_ref[4 * d + 2:4 * d + 3, :]
        bfi = pr_ref[4 * d + 3:4 * d + 4, :]
        a2 = _cmul(abr, abi, abr, abi)
        a4 = _cmul(*a2, *a2)
        a8 = _cmul(*a4, *a4)
        a16 = _cmul(*a8, *a8)
        at_ref[2 * d:2 * d + 1, :] = a16[0]
        at_ref[2 * d + 1:2 * d + 2, :] = a16[1]
        bbr, bbi = _cmul(bfr, bfi, btr_ref[d], bti_ref[d])
        kk_rows = (CHUNK - 1 - step_of_row) if d == 0 else step_of_row
        pbr, pbi = _pow_table(abr, abi, kk_rows, (n_in1, PAIR_ST))
        bbr16 = jnp.concatenate([bbr] * CHUNK, axis=0)
        bbi16 = jnp.concatenate([bbi] * CHUNK, axis=0)
        wbr, wbi = _cmul(pbr, pbi, bbr16, bbi16)
        for part, wb in ((0, wbr), (1, wbi)):
            c0 = PAIR_IN + (2 * d + part) * PAIR_ST
            w1_ref[0:n_in1, c0:c0 + PAIR_ST] = jnp.where(first_grp_lane, wb, 0.0).astype(bf16)
            w1_ref[n_in1:2 * n_in1, c0:c0 + PAIR_ST] = jnp.where(first_grp_lane, 0.0, wb).astype(bf16)

        acr = pc_ref[:, 2 * d:2 * d + 1]
        aci = pc_ref[:, 2 * d + 1:2 * d + 2]
        kk_lanes = step_of_lane if d == 0 else (CHUNK - 1 - step_of_lane)
        t0r, t0i = _pow_table(acr, aci, kk_lanes, (PAIR_ST, n_in1))
        t1r, t1i = _cmul(t0r, t0i, acr, aci)
        ctr = ctr_ref[d]
        cti = cti_ref[d]
        ca0r, ca0i = _cmul(t0r, t0i, ctr, cti)
        ca1r, ca1i = _cmul(t1r, t1i, ctr, cti)
        for part, ca in ((0, ca1r), (1, -ca1i)):
            r0 = (2 * d + part) * PAIR_ST
            w2_ref[r0:r0 + PAIR_ST, 0:n_in1] = jnp.where(first_grp_row, ca, 0.0).astype(bf16)
            w2_ref[r0:r0 + PAIR_ST, n_in1:2 * n_in1] = jnp.where(first_grp_row, 0.0, ca).astype(bf16)
        for e in range(2):
            keep = first_grp_lane if e == 0 else jnp.logical_not(first_grp_lane)
            lbr = jnp.where(keep, bbr, 0.0)
            lbi = jnp.where(keep, bbi, 0.0)
            g_tab[d][e] = _dot3(lbr, ca0r) - _dot3(lbi, ca0i)

    chan_of_row = lax.broadcasted_iota(jnp.int32, (SSM_GROUP, 1), 0)
    for e in range(2):
        gf = g_tab[0][e]
        gb = g_tab[1][e]
        dvec = dt_ref[e:e + 1, :]
        for jp in range(CHUNK):
            lo = SSM_GROUP * jp
            hi = SSM_GROUP * (jp + 1)
            rf = gf if jp == 0 else pltpu.roll(gf, lo, axis=1)
            rb = gb if jp == CHUNK - 1 else pltpu.roll(gb, hi, axis=1)
            blk = jnp.where(lane256 >= lo, rf, 0.0) + jnp.where(lane256 < hi, rb, 0.0)
            on_diag = (step_of_lane == jp) & ((lane256 & (SSM_GROUP - 1)) == chan_of_row)
            blk = blk + jnp.where(on_diag, dvec, 0.0)
            r0 = e * n_in1 + lo
            w1_ref[r0:r0 + SSM_GROUP, e * n_in1:(e + 1) * n_in1] = blk.astype(bf16)
            w1_ref[r0:r0 + SSM_GROUP, (1 - e) * n_in1:(2 - e) * n_in1] = jnp.zeros(
                (SSM_GROUP, n_in1), bf16)


def _s5_matrices(prow, pcol, bt_re, bt_im, ct_re, ct_im, d_tiled):
    n_in1 = CHUNK * SSM_GROUP
    return pl.pallas_call(
        _prep_kernel,
        grid=(DEPTH, N_PAIR),
        in_specs=[
            pl.BlockSpec((None, None, 8, PAIR_ST), lambda l, q: (l, q, 0, 0)),
            pl.BlockSpec((None, None, PAIR_ST, 8), lambda l, q: (l, q, 0, 0)),
            pl.BlockSpec((None, 2, None, SSM_GROUP, PAIR_ST), lambda l, q: (l, 0, q, 0, 0)),
            pl.BlockSpec((None, 2, None, SSM_GROUP, PAIR_ST), lambda l, q: (l, 0, q, 0, 0)),
            pl.BlockSpec((None, 2, None, PAIR_ST, n_in1), lambda l, q: (l, 0, q, 0, 0)),
            pl.BlockSpec((None, 2, None, PAIR_ST, n_in1), lambda l, q: (l, 0, q, 0, 0)),
            pl.BlockSpec((None, None, 2, n_in1), lambda l, q: (l, q, 0, 0)),
        ],
        out_specs=[
            pl.BlockSpec((None, None, PAIR_IN, PAIR_IN + 4 * PAIR_ST), lambda l, q: (l, q, 0, 0)),
            pl.BlockSpec((None, None, 4 * PAIR_ST, PAIR_IN), lambda l, q: (l, q, 0, 0)),
            pl.BlockSpec((None, 4, PAIR_ST), lambda l, q: (l, 0, q)),
        ],
        out_shape=[
            jax.ShapeDtypeStruct((DEPTH, N_PAIR, PAIR_IN, PAIR_IN + 4 * PAIR_ST), bf16),
            jax.ShapeDtypeStruct((DEPTH, N_PAIR, 4 * PAIR_ST, PAIR_IN), bf16),
            jax.ShapeDtypeStruct((DEPTH, 4, SSM_GROUPS * SSM_STATE), f32),
        ],
        compiler_params=_params("arbitrary", "arbitrary"),
        name="s5_chunk_matrices",
    )(prow, pcol, bt_re, bt_im, ct_re, ct_im, d_tiled)


def _s5_kernel(u_ref, w1_ref, w2_ref, at_ref, h0_ref, y_ref, fin_ref, r_scr, sp_scr):
    r_scr[...] = _dot(u_ref[...], w1_ref[...])
    coef = []
    for d in range(2):
        coef.append((at_ref[2 * d:2 * d + 1, :], at_ref[2 * d + 1:2 * d + 2, :]))

    def step(d, rows, sr, si):
        c0 = 2 * d * PAIR_ST
        sp_scr[rows, c0:c0 + PAIR_ST] = sr
        sp_scr[rows, c0 + PAIR_ST:c0 + 2 * PAIR_ST] = si
        pr = r_scr[rows, PAIR_IN + c0:PAIR_IN + c0 + PAIR_ST]
        pi = r_scr[rows, PAIR_IN + c0 + PAIR_ST:PAIR_IN + c0 + 2 * PAIR_ST]
        ar, ai = coef[d]
        return ar * sr - ai * si + pr, ar * si + ai * sr + pi

    for d in range(2):
        sr = jnp.zeros((BATCH, PAIR_ST), f32)
        si = jnp.zeros((BATCH, PAIR_ST), f32)
        order = range(NC_CTX) if d == 0 else range(NC_CTX - 1, -1, -1)
        for c in order:
            sr, si = step(d, pl.ds(c * BATCH, BATCH), sr, si)
        fin_ref[2 * d] = sr
        fin_ref[2 * d + 1] = si

    for d in range(2):
        def body(k, carry, d=d):
            c = k if d == 0 else NC_LAT - 1 - k
            start = pl.multiple_of(ROWS_CTX + c * LAT_PAD, LAT_PAD)
            return step(d, pl.ds(start, LAT_PAD), *carry)
        lax.fori_loop(0, NC_LAT, body, (h0_ref[2 * d], h0_ref[2 * d + 1]))

    y_ref[...] = r_scr[:, 0:PAIR_IN] + _dot(sp_scr[...].astype(bf16), w2_ref[...])


def _s5_mixer(u_chunks, w1, w2, a_chunk, h0, layer):
    n_state = SSM_GROUPS * SSM_STATE
    return pl.pallas_call(
        _s5_kernel,
        grid=(N_PAIR,),
        in_specs=[
            pl.BlockSpec((ROWS_S5, PAIR_IN), lambda q: (0, q)),
            pl.BlockSpec((None, None, PAIR_IN, PAIR_IN + 4 * PAIR_ST), lambda q: (layer, q, 0, 0)),
            pl.BlockSpec((None, None, 4 * PAIR_ST, PAIR_IN), lambda q: (layer, q, 0, 0)),
            pl.BlockSpec((None, 4, PAIR_ST), lambda q: (layer, 0, q)),
            pl.BlockSpec((4, LAT_PAD, PAIR_ST), lambda q: (0, 0, q)),
        ],
        out_specs=[
            pl.BlockSpec((ROWS_S5, PAIR_IN), lambda q: (0, q)),
            pl.BlockSpec((4, BATCH, PAIR_ST), lambda q: (0, 0, q)),
        ],
        out_shape=[
            jax.ShapeDtypeStruct((ROWS_S5, N_PAIR * PAIR_IN), f32),
            jax.ShapeDtypeStruct((4, BATCH, n_state), f32),
        ],
        scratch_shapes=[
            pltpu.VMEM((ROWS_S5, PAIR_IN + 4 * PAIR_ST), f32),
            pltpu.VMEM((ROWS_S5, 4 * PAIR_ST), f32),
        ],
        compiler_params=_params("arbitrary"),
        name="s5_chunked_mixer",
    )(u_chunks, w1, w2, a_chunk, h0)


def _shift_rows(v, period, n_rows):
    row = lax.broadcasted_iota(jnp.int32, (n_rows, 1), 0)
    pos = row & (period - 1)
    prev = jnp.where(pos == 0, 0.0, pltpu.roll(v, 1, axis=0))
    nxt = jnp.where(pos == period - 1, 0.0, pltpu.roll(v, n_rows - 1, axis=0))
    return prev, nxt


def _dwconv3_rows(v, w_ref, period, n_rows):
    prev, nxt = _shift_rows(v, period, n_rows)
    return prev * w_ref[0:1, :] + v * w_ref[1:2, :] + nxt * w_ref[2:3, :]


def _k1_kernel(x_ref, mod_ref, gpre_ref, wgb_ref, wgc_ref, whv_ref, wu_ref, cw_ref, gco_ref,
               cn_ref, u_ref, h_scr):
    i = pl.program_id(0)
    n = pl.program_id(1)

    @pl.when(n == 0)
    def _():
        r = jnp.maximum(i - (N_CTX // TM1 - 1), 0)
        sh1 = mod_ref[pl.ds(r, 1), 0:D_MODEL]
        sc1 = mod_ref[pl.ds(r, 1), D_MODEL:2 * D_MODEL]
        y = _rms_rows(x_ref[...], gpre_ref[...])
        h_scr[...] = (y * (1.0 + sc1) + sh1).astype(bf16)

    h = h_scr[...]
    u_ref[...] = _dot(h, wu_ref[...]).astype(bf16)
    v = _dot(h, wgc_ref[...]) * _dot(h, whv_ref[...])
    period = jnp.where(i < N_CTX // TM1, SEQ, GRID_W)
    v = _dwconv3_rows(v, cw_ref, period, TM1)
    co = _dot(h, wgb_ref[...]) * v
    rr = lax.broadcasted_iota(jnp.int32, (TC1, TC1), 0) // HEAD_DIM
    cc = lax.broadcasted_iota(jnp.int32, (TC1, TC1), 1) // HEAD_DIM
    avg = jnp.where(rr == cc, 1.0 / HEAD_DIM, 0.0).astype(bf16)
    hi, lo = _split_bf16(co * co)
    ms = _dot(hi, avg) + _dot(lo, avg)
    cn_ref[...] = (co * lax.rsqrt(ms + EPS) * gco_ref[...]).astype(bf16)


def _in_proj(x, mod, g_pre1, w_in_bf, conv_w, g_conv_out, layer):
    nb = CONV_WIDTH // TC1
    wspec = lambda off: pl.BlockSpec((None, D_MODEL, TC1), lambda i, n: (layer, 0, off * nb + n))
    return pl.pallas_call(
        _k1_kernel,
        grid=(N_TOK // TM1, nb),
        in_specs=[
            pl.BlockSpec((TM1, D_MODEL), lambda i, n: (i, 0)),
            pl.BlockSpec((None, 8, 6 * D_MODEL), lambda i, n: (layer, 0, 0)),
            pl.BlockSpec((None, 1, D_MODEL), lambda i, n: (layer, 0, 0)),
            wspec(0), wspec(1), wspec(2), wspec(3),
            pl.BlockSpec((None, 3, TC1), lambda i, n: (layer, 0, n)),
            pl.BlockSpec((None, 1, TC1), lambda i, n: (layer, 0, n)),
        ],
        out_specs=[
            pl.BlockSpec((TM1, TC1), lambda i, n: (i, n)),
            pl.BlockSpec((TM1, TC1), lambda i, n: (i, n)),
        ],
        out_shape=[
            jax.ShapeDtypeStruct((N_TOK, CONV_WIDTH), bf16),
            jax.ShapeDtypeStruct((N_TOK, SSM_WIDTH), bf16),
        ],
        scratch_shapes=[pltpu.VMEM((TM1, D_MODEL), bf16)],
        compiler_params=_params("arbitrary", "arbitrary"),
        name="in_proj_conv_mixer",
    )(x, mod, g_pre1, w_in_bf, w_in_bf, w_in_bf, w_in_bf, conv_w, g_conv_out)


def _k3_kernel(x_ref, cn_ref, y_ref, mod_ref, wglu_ref, bglu_ref, gsso_ref, wout_ref,
               gpost1_ref, gpre2_ref, x1_ref, h2_ref):
    i = pl.program_id(0)
    n_ctx_tiles = N_CTX // TM3
    r = jnp.where(i < n_ctx_tiles, 0, 1 + (i - n_ctx_tiles) // (DEC_SEQ // TM3))
    gt1 = mod_ref[pl.ds(r, 1), 2 * D_MODEL:3 * D_MODEL]
    sh2 = mod_ref[pl.ds(r, 1), 3 * D_MODEL:4 * D_MODEL]
    sc2 = mod_ref[pl.ds(r, 1), 4 * D_MODEL:5 * D_MODEL]

    z = jax.nn.gelu(y_ref[...], approximate=True)
    gate = jax.nn.sigmoid(_dot(z.astype(bf16), wglu_ref[...]) + bglu_ref[...])
    sn = _rms_rows(z * gate, gsso_ref[...])
    mixed = (_dot(cn_ref[...], wout_ref[0:CONV_WIDTH, :])
             + _dot(sn.astype(bf16), wout_ref[CONV_WIDTH:CONV_WIDTH + SSM_WIDTH, :]))
    x1 = x_ref[...] + gt1 * _rms_rows(mixed, gpost1_ref[...])
    x1_ref[...] = x1
    h2_ref[...] = (_rms_rows(x1, gpre2_ref[...]) * (1.0 + sc2) + sh2).astype(bf16)


def _out_proj(x, cn, y, mod, w_glu_bf, b_glu, g_ssm_out, w_out_bf, g_post1, g_pre2, layer):
    vec = lambda width: pl.BlockSpec((None, 1, width), lambda i: (layer, 0, 0))
    return pl.pallas_call(
        _k3_kernel,
        grid=(N_TOK // TM3,),
        in_specs=[
            pl.BlockSpec((TM3, D_MODEL), lambda i: (i, 0)),
            pl.BlockSpec((TM3, CONV_WIDTH), lambda i: (i, 0)),
            pl.BlockSpec((TM3, SSM_WIDTH), lambda i: (i, 0)),
            pl.BlockSpec((None, 8, 6 * D_MODEL), lambda i: (layer, 0, 0)),
            pl.BlockSpec((None, SSM_WIDTH, SSM_WIDTH), lambda i: (layer, 0, 0)),
            vec(SSM_WIDTH), vec(SSM_WIDTH),
            pl.BlockSpec((None, CONV_WIDTH + SSM_WIDTH, D_MODEL), lambda i: (layer, 0, 0)),
            vec(D_MODEL), vec(D_MODEL),
        ],
        out_specs=[
            pl.BlockSpec((TM3, D_MODEL), lambda i: (i, 0)),
            pl.BlockSpec((TM3, D_MODEL), lambda i: (i, 0)),
        ],
        out_shape=[
            jax.ShapeDtypeStruct((N_TOK, D_MODEL), f32),
            jax.ShapeDtypeStruct((N_TOK, D_MODEL), bf16),
        ],
        compiler_params=_params("arbitrary"),
        name="glu_out_proj",
    )(x, cn, y, mod, w_glu_bf, b_glu, g_ssm_out, w_out_bf, g_post1, g_pre2)


def _k4_kernel(hm_ref, hp_ref, hn_ref, x1_ref, mod_ref, wa_ref, wg_ref, cwa_ref, cwg_ref,
               wd_ref, gpost2_ref, o_ref, hext_scr):
    i = pl.program_id(0)
    j = pl.program_id(1)
    n_ctx_tiles = N_CTX // TM4
    tiles_per_lat = DEC_SEQ // TM4
    is_lat = i >= n_ctx_tiles

    @pl.when(j == 0)
    def _():
        o_ref[...] = jnp.zeros((TM4, D_MODEL), f32)

    def finish(a, g):
        act = (g * jax.nn.sigmoid(g) * a).astype(bf16)
        o_ref[...] += _dot(act, wd_ref[...])

    @pl.when(jnp.logical_not(is_lat))
    def _():
        h = hm_ref[...]
        a = _dwconv3_rows(_dot(h, wa_ref[...]), cwa_ref, SEQ, TM4)
        g = _dwconv3_rows(_dot(h, wg_ref[...]), cwg_ref, SEQ, TM4)
        finish(a, g)

    @pl.when(is_lat)
    def _():
        @pl.when(j == 0)
        def _():
            t = (i - n_ctx_tiles) % tiles_per_lat
            hext_scr[0:HALO, :] = hp_ref[...]
            hext_scr[HALO:HALO + TM4, :] = hm_ref[...]
            hext_scr[HALO + TM4:2 * HALO + TM4, :] = hn_ref[...]

            @pl.when(t == 0)
            def _():
                hext_scr[0:HALO, :] = jnp.zeros((HALO, D_MODEL), bf16)

            @pl.when(t == tiles_per_lat - 1)
            def _():
                hext_scr[HALO + TM4:2 * HALO + TM4, :] = jnp.zeros((HALO, D_MODEL), bf16)

        h = hext_scr[...]

        def conv(up, w_ref):
            return (up[0:TM4] * w_ref[0:1, :] + up[HALO:HALO + TM4] * w_ref[1:2, :]
                    + up[2 * HALO:2 * HALO + TM4] * w_ref[2:3, :])

        finish(conv(_dot(h, wa_ref[...]), cwa_ref), conv(_dot(h, wg_ref[...]), cwg_ref))

    @pl.when(j == pl.num_programs(1) - 1)
    def _():
        r = jnp.where(is_lat, 1 + (i - n_ctx_tiles) // tiles_per_lat, 0)
        gt2 = mod_ref[pl.ds(r, 1), 5 * D_MODEL:6 * D_MODEL]
        o_ref[...] = x1_ref[...] + gt2 * _rms_rows(o_ref[...], gpost2_ref[...])


def _conv_ffn(h2, x1, mod, w_up_bf, ffn_conv_w, w_down_bf, g_post2, layer):
    nj = D_FF // TF4
    halo_per_tile = TM4 // HALO
    n_halo_blocks = N_TOK // HALO
    return pl.pallas_call(
        _k4_kernel,
        grid=(N_TOK // TM4, nj),
        in_specs=[
            pl.BlockSpec((TM4, D_MODEL), lambda i, j: (i, 0)),
            pl.BlockSpec((HALO, D_MODEL), lambda i, j: (jnp.maximum(i * halo_per_tile - 1, 0), 0)),
            pl.BlockSpec((HALO, D_MODEL),
                         lambda i, j: (jnp.minimum((i + 1) * halo_per_tile, n_halo_blocks - 1), 0)),
            pl.BlockSpec((TM4, D_MODEL), lambda i, j: (i, 0)),
            pl.BlockSpec((None, 8, 6 * D_MODEL), lambda i, j: (layer, 0, 0)),
            pl.BlockSpec((None, D_MODEL, TF4), lambda i, j: (layer, 0, j)),
            pl.BlockSpec((None, D_MODEL, TF4), lambda i, j: (layer, 0, nj + j)),
            pl.BlockSpec((None, 3, TF4), lambda i, j: (layer, 0, j)),
            pl.BlockSpec((None, 3, TF4), lambda i, j: (layer, 0, nj + j)),
            pl.BlockSpec((None, TF4, D_MODEL), lambda i, j: (layer, j, 0)),
            pl.BlockSpec((None, 1, D_MODEL), lambda i, j: (layer, 0, 0)),
        ],
        out_specs=pl.BlockSpec((TM4, D_MODEL), lambda i, j: (i, 0)),
        out_shape=jax.ShapeDtypeStruct((N_TOK, D_MODEL), f32),
        scratch_shapes=[pltpu.VMEM((TM4 + 2 * HALO, D_MODEL), bf16)],
        compiler_params=_params("arbitrary", "arbitrary"),
        name="conv_ffn",
    )(h2, h2, h2, x1, mod, w_up_bf, w_up_bf, ffn_conv_w, ffn_conv_w, w_down_bf, g_post2)


def _to_chunk_rows(u):
    def one(part, nb, nc, nb_pad):
        part = part.reshape(nb, nc, CHUNK, SSM_GROUPS, SSM_GROUP).transpose(1, 0, 3, 2, 4)
        part = jnp.pad(part, ((0, 0), (0, nb_pad - nb), (0, 0), (0, 0), (0, 0)))
        return part.reshape(nc * nb_pad, SSM_GROUPS * CHUNK * SSM_GROUP)
    return jnp.concatenate([one(u[:N_CTX], BATCH, NC_CTX, BATCH),
                            one(u[N_CTX:], DEC_BATCH, NC_LAT, LAT_PAD)], axis=0)


def _from_chunk_rows(y):
    def one(part, nb, nc, nb_pad):
        part = part.reshape(nc, nb_pad, SSM_GROUPS, CHUNK, SSM_GROUP)[:, :nb]
        return part.transpose(1, 0, 3, 2, 4).reshape(nb * nc * CHUNK, SSM_WIDTH)
    return jnp.concatenate([one(y[:ROWS_CTX], BATCH, NC_CTX, BATCH),
                            one(y[ROWS_CTX:], DEC_BATCH, NC_LAT, LAT_PAD)], axis=0)


def _pair_lanes(x):
    return x.reshape(x.shape[:-2] + (N_PAIR, PAIR_ST))


def kernel(x_prompt, x_sample, state_ssm_re, state_ssm_im, c, c_ctx, w_ada, b_ada, g_pre1, w_in, conv_w, ssm_a_re, ssm_a_im, ssm_log_dt, ssm_b_re, ssm_b_im, ssm_c_re, ssm_c_im, ssm_d, w_glu, b_glu, g_conv_out, g_ssm_out, w_out, g_post1, g_pre2, w_up, ffn_conv_w, w_down, g_post2):
    w_in_bf = _cast_bf16(w_in.reshape(DEPTH * D_MODEL, -1), 256).reshape(w_in.shape)
    w_glu_bf = _cast_bf16(w_glu.reshape(DEPTH * SSM_WIDTH, -1), 1024).reshape(w_glu.shape)
    w_out_bf = _cast_bf16(w_out.reshape(DEPTH * 2 * CONV_WIDTH, -1), 512).reshape(w_out.shape)
    w_up_bf = _cast_bf16(w_up.reshape(DEPTH * D_MODEL, -1), 64).reshape(w_up.shape)
    w_down_bf = _cast_bf16(w_down.reshape(DEPTH * D_FF, -1), 512).reshape(w_down.shape)

    cvec8 = jnp.concatenate([c_ctx[None, :], c, jnp.zeros((8 - 1 - DEC_BATCH, D_MODEL), f32)], axis=0)
    mod = _modulation(cvec8, w_ada, b_ada)

    abr, abi, bfr, bfi = [t.reshape(DEPTH, 2, N_PAIR, PAIR_ST)
                          for t in _discretise(ssm_a_re, ssm_a_im, ssm_log_dt)]
    prow = jnp.stack([abr, abi, bfr, bfi], axis=2)
    prow = prow.transpose(0, 3, 1, 2, 4).reshape(DEPTH, N_PAIR, 8, PAIR_ST)
    zcol = jnp.zeros_like(abr[:, 0])
    pcol = jnp.stack([abr[:, 0], abi[:, 0], abr[:, 1], abi[:, 1], zcol, zcol, zcol, zcol], axis=-1)

    def b_rows(b):
        b = b.transpose(0, 1, 2, 4, 3).reshape(DEPTH, 2, N_PAIR, 2, SSM_GROUP, SSM_STATE)
        return b.transpose(0, 1, 2, 4, 3, 5).reshape(DEPTH, 2, N_PAIR, SSM_GROUP, PAIR_ST)

    def c_cols(cm):
        cm = cm.transpose(0, 1, 2, 4, 3)[..., None, :]
        cm = jnp.broadcast_to(cm, (DEPTH, 2, SSM_GROUPS, SSM_STATE, CHUNK, SSM_GROUP))
        return cm.reshape(DEPTH, 2, N_PAIR, PAIR_ST, CHUNK * SSM_GROUP)

    d_tiled = jnp.broadcast_to(ssm_d.reshape(DEPTH, SSM_GROUPS, 1, SSM_GROUP),
                               (DEPTH, SSM_GROUPS, CHUNK, SSM_GROUP))
    d_tiled = d_tiled.reshape(DEPTH, N_PAIR, 2, CHUNK * SSM_GROUP)
    w1, w2, a_chunk = _s5_matrices(prow, pcol, b_rows(ssm_b_re), b_rows(ssm_b_im),
                                   c_cols(ssm_c_re), c_cols(ssm_c_im), d_tiled)

    g3 = lambda g: g.reshape(DEPTH, 1, -1)
    g_pre1_, g_conv_out_, g_ssm_out_, g_post1_, g_pre2_, g_post2_, b_glu_ = map(
        g3, (g_pre1, g_conv_out, g_ssm_out, g_post1, g_pre2, g_post2, b_glu))

    x = jnp.concatenate([x_prompt.reshape(N_CTX, D_MODEL), x_sample.reshape(N_LAT, D_MODEL)], axis=0)
    fins = []
    for l in range(DEPTH):
        cn, u = _in_proj(x, mod, g_pre1_, w_in_bf, conv_w, g_conv_out_, l)
        sre = state_ssm_re[:, l].reshape(DEC_BATCH, 2, -1)
        sim = state_ssm_im[:, l].reshape(DEC_BATCH, 2, -1)
        h0 = jnp.stack([sre[:, 0], sim[:, 0], sre[:, 1], sim[:, 1]], axis=0)
        h0 = jnp.pad(h0, ((0, 0), (0, LAT_PAD - DEC_BATCH), (0, 0)))
        y_chunks, fin = _s5_mixer(_to_chunk_rows(u), w1, w2, a_chunk, h0, l)
        fins.append(fin)
        x1, h2 = _out_proj(x, cn, _from_chunk_rows(y_chunks), mod, w_glu_bf, b_glu_, g_ssm_out_,
                           w_out_bf, g_post1_, g_pre2_, l)
        x = _conv_ffn(h2, x1, mod, w_up_bf, ffn_conv_w, w_down_bf, g_post2_, l)

    fin = jnp.stack(fins, axis=0).reshape(DEPTH, 2, 2, BATCH, SSM_GROUPS, SSM_STATE)
    new_re = fin[:, :, 0].transpose(2, 0, 1, 3, 4)
    new_im = fin[:, :, 1].transpose(2, 0, 1, 3, 4)
    y_prompt = x[:N_CTX].reshape(BATCH, SEQ, D_MODEL)
    y_sample = x[N_CTX:].reshape(DEC_BATCH, DEC_SEQ, D_MODEL)
    return (y_prompt, y_sample, new_re, new_im)
```

```python
import jax
import jax.numpy as jnp
from jax import lax
from jax.experimental import pallas as pl
from jax.experimental.pallas import tpu as pltpu

D_MODEL = 2048
BATCH = 16
SEQ = 256
DEPTH = 4
DEC_BATCH = 2
DEC_SEQ = 1024
GRID_W = 64
CONV_WIDTH = 1024
HEAD_DIM = 64
SSM_WIDTH = 1024
SSM_GROUP = 16
SSM_GROUPS = 64
SSM_STATE = 64
D_FF = 5632
EPS = 1e-6

LANES = 128
N_CTX = BATCH * SEQ
N_LAT = DEC_BATCH * DEC_SEQ
N_TOK = N_CTX + N_LAT
CHUNK = 16
NC_CTX = SEQ // CHUNK
NC_LAT = DEC_SEQ // CHUNK
ROWS_CTX = BATCH * NC_CTX
ROWS_LAT = DEC_BATCH * NC_LAT
ROWS_S5 = ROWS_CTX + ROWS_LAT
N_PAIR = SSM_GROUPS // 2
PAIR_CH = 2 * SSM_GROUP
PAIR_IN = CHUNK * PAIR_CH
PAIR_ST = 2 * SSM_STATE
PAIRS_PER_STEP = LANES // PAIR_CH
STEPS_PER_LANE_BLOCK = LANES // PAIR_CH

TM1 = 1024
TC1 = 256
TM3 = 256
TM4 = 512
TF4 = 512
HALO = GRID_W

VMEM_LIMIT = 56 * 1024 * 1024

f32 = jnp.float32
bf16 = jnp.bfloat16


def _dot(a, b):
    return jnp.dot(a, b, preferred_element_type=f32)


def _split_bf16(x):
    hi = x.astype(bf16)
    lo = (x - hi.astype(f32)).astype(bf16)
    return hi, lo


def _dot3(a, b):
    ah, al = _split_bf16(a)
    bh, bl = _split_bf16(b)
    return _dot(ah, bh) + (_dot(ah, bl) + _dot(al, bh))


def _dot_select(a, sel):
    a1 = a.astype(bf16)
    r1 = a - a1.astype(f32)
    a2 = r1.astype(bf16)
    a3 = (r1 - a2.astype(f32)).astype(bf16)
    return _dot(a1, sel) + (_dot(a2, sel) + _dot(a3, sel))


def _rms_rows(x, g):
    ms = jnp.mean(x * x, axis=-1, keepdims=True)
    return x * lax.rsqrt(ms + EPS) * g


def _cmul(ar, ai, br, bi):
    return ar * br - ai * bi, ar * bi + ai * br


def _params(*sem):
    return pltpu.CompilerParams(dimension_semantics=sem, vmem_limit_bytes=VMEM_LIMIT)


def _cast_kernel(x_ref, o_ref):
    o_ref[...] = x_ref[...].astype(o_ref.dtype)


def _cast_bf16(w, block_rows):
    rows, cols = w.shape
    return pl.pallas_call(
        _cast_kernel,
        grid=(rows // block_rows,),
        in_specs=[pl.BlockSpec((block_rows, cols), lambda i: (i, 0))],
        out_specs=pl.BlockSpec((block_rows, cols), lambda i: (i, 0)),
        out_shape=jax.ShapeDtypeStruct((rows, cols), bf16),
        compiler_params=_params("arbitrary"),
        name="cast_bf16",
    )(w)


def _mod_kernel(cv_ref, w_ref, b_ref, o_ref):
    cv = cv_ref[...]
    s = cv * jax.nn.sigmoid(cv)
    o_ref[...] = _dot(s.astype(bf16), w_ref[...].astype(bf16)) + b_ref[...]


def _modulation(cvec8, w_ada, b_ada):
    tn = 1024
    n_out = 6 * D_MODEL
    return pl.pallas_call(
        _mod_kernel,
        grid=(DEPTH, n_out // tn),
        in_specs=[
            pl.BlockSpec((8, D_MODEL), lambda l, n: (0, 0)),
            pl.BlockSpec((None, D_MODEL, tn), lambda l, n: (l, 0, n)),
            pl.BlockSpec((None, 1, tn), lambda l, n: (l, 0, n)),
        ],
        out_specs=pl.BlockSpec((None, 8, tn), lambda l, n: (l, 0, n)),
        out_shape=jax.ShapeDtypeStruct((DEPTH, 8, n_out), f32),
        compiler_params=_params("arbitrary", "arbitrary"),
        name="adaln_modulation",
    )(cvec8, w_ada, b_ada.reshape(DEPTH, 1, n_out))


def _disc_kernel(ar_ref, ai_ref, ldt_ref, abr_ref, abi_ref, bfr_ref, bfi_ref):
    ar = ar_ref[...]
    ai = ai_ref[...]
    dt = jnp.exp(ldt_ref[...])
    mag = jnp.exp(ar * dt)
    abr = mag * jnp.cos(ai * dt)
    abi = mag * jnp.sin(ai * dt)
    nr = abr - 1.0
    den = ar * ar + ai * ai
    abr_ref[...] = abr
    abi_ref[...] = abi
    bfr_ref[...] = (nr * ar + abi * ai) / den
    bfi_ref[...] = (abi * ar - nr * ai) / den


def _discretise(a_re, a_im, log_dt):
    shape = (DEPTH * 2, SSM_GROUPS * SSM_STATE)
    spec = pl.BlockSpec(shape, lambda: (0, 0))
    out = jax.ShapeDtypeStruct(shape, f32)
    return pl.pallas_call(
        _disc_kernel,
        in_specs=[spec, spec, spec],
        out_specs=[spec, spec, spec, spec],
        out_shape=[out, out, out, out],
        name="s5_discretise",
    )(a_re.reshape(shape), a_im.reshape(shape),
      jnp.broadcast_to(log_dt[..., None], (DEPTH, 2, SSM_GROUPS, SSM_STATE)).reshape(shape))


def _pow_table(br, bi, kk, shape):
    tr = ti = None
    pr, pi = br, bi
    for bit in range(4):
        sel = ((kk >> bit) & 1) == 1
        fr = jnp.broadcast_to(jnp.where(sel, pr, 1.0), shape)
        fi = jnp.broadcast_to(jnp.where(sel, pi, 0.0), shape)
        if tr is None:
            tr, ti = fr, fi
        else:
            tr, ti = _cmul(tr, ti, fr, fi)
        pr, pi = _cmul(pr, pi, pr, pi)
    return tr, ti


def _prep_kernel(pr_ref, pc_ref, btr_ref, bti_ref, ctr_ref, cti_ref, dv_ref,
                 w1_ref, w2_ref, at_ref):
    lane_in = lax.broadcasted_iota(jnp.int32, (1, PAIR_IN), 1)
    step_of_lane = lane_in >> 5
    grp_of_lane = (lane_in >> 4) & 1
    chan_of_lane = lane_in & (SSM_GROUP - 1)
    row_in = lax.broadcasted_iota(jnp.int32, (PAIR_IN, 1), 0)
    step_of_row = row_in >> 5
    lane_st = lax.broadcasted_iota(jnp.int32, (1, PAIR_ST), 1)
    grp_of_st_lane = lane_st >> 6
    row_st = lax.broadcasted_iota(jnp.int32, (PAIR_ST, 1), 0)
    grp_of_st_row = row_st >> 6
    chan_row = lax.broadcasted_iota(jnp.int32, (SSM_GROUP, 1), 0)

    spread = jnp.where(chan_of_lane == chan_row, 1.0, 0.0).astype(bf16)
    same_grp = grp_of_st_row == grp_of_lane

    taps = []
    for d in range(2):
        abr = pr_ref[4 * d + 0:4 * d + 1, :]
        abi = pr_ref[4 * d + 1:4 * d + 2, :]
        bfr = pr_ref[4 * d + 2:4 * d + 3, :]
        bfi = pr_ref[4 * d + 3:4 * d + 4, :]
        a2 = _cmul(abr, abi, abr, abi)
        a4 = _cmul(*a2, *a2)
        a8 = _cmul(*a4, *a4)
        a16 = _cmul(*a8, *a8)
        at_ref[2 * d:2 * d + 1, :] = a16[0]
        at_ref[2 * d + 1:2 * d + 2, :] = a16[1]
        bbr, bbi = _cmul(bfr, bfi, btr_ref[d], bti_ref[d])
        kk_rows = (CHUNK - 1 - step_of_row) if d == 0 else step_of_row
        pbr, pbi = _pow_table(abr, abi, kk_rows, (PAIR_IN, PAIR_ST))
        slab_r = jnp.concatenate([jnp.where(grp_of_st_lane == e, bbr, 0.0) for e in range(2)], axis=0)
        slab_i = jnp.concatenate([jnp.where(grp_of_st_lane == e, bbi, 0.0) for e in range(2)], axis=0)
        tile_r = jnp.concatenate([slab_r] * CHUNK, axis=0)
        tile_i = jnp.concatenate([slab_i] * CHUNK, axis=0)
        wbr, wbi = _cmul(pbr, pbi, tile_r, tile_i)
        for part, wb in ((0, wbr), (1, wbi)):
            c0 = PAIR_IN + (2 * d + part) * PAIR_ST
            w1_ref[:, c0:c0 + PAIR_ST] = wb.astype(bf16)

        acr = pc_ref[:, 2 * d:2 * d + 1]
        aci = pc_ref[:, 2 * d + 1:2 * d + 2]
        kk_lanes = step_of_lane if d == 0 else (CHUNK - 1 - step_of_lane)
        t0r, t0i = _pow_table(acr, aci, kk_lanes, (PAIR_ST, PAIR_IN))
        t1r, t1i = _cmul(t0r, t0i, acr, aci)
        ctr = jnp.where(same_grp, _dot_select(ctr_ref[d], spread), 0.0)
        cti = jnp.where(same_grp, _dot_select(cti_ref[d], spread), 0.0)
        ca0r, ca0i = _cmul(t0r, t0i, ctr, cti)
        ca1r, ca1i = _cmul(t1r, t1i, ctr, cti)
        w2_ref[(2 * d) * PAIR_ST:(2 * d + 1) * PAIR_ST, :] = ca1r.astype(bf16)
        w2_ref[(2 * d + 1) * PAIR_ST:(2 * d + 2) * PAIR_ST, :] = (-ca1i).astype(bf16)
        taps.append(_dot3(bbr, ca0r) - _dot3(bbi, ca0i))

    gf, gb = taps
    dvec = dv_ref[...]
    for jp in range(CHUNK):
        lo = PAIR_CH * jp
        hi = PAIR_CH * (jp + 1)
        rf = gf if jp == 0 else pltpu.roll(gf, lo, axis=1)
        rb = gb if jp == CHUNK - 1 else pltpu.roll(gb, hi, axis=1)
        blk = jnp.where(lane_in >= lo, rf, 0.0) + jnp.where(lane_in < hi, rb, 0.0)
        on_diag = (step_of_lane == jp) & (chan_of_lane == chan_row)
        blk = blk + jnp.where(on_diag, dvec, 0.0)
        for e in range(2):
            r0 = lo + e * SSM_GROUP
            w1_ref[r0:r0 + SSM_GROUP, 0:PAIR_IN] = jnp.where(grp_of_lane == e, blk, 0.0).astype(bf16)


def _s5_matrices(prow, pcol, bt_re, bt_im, ct_re, ct_im, d_vec):
    return pl.pallas_call(
        _prep_kernel,
        grid=(DEPTH, N_PAIR),
        in_specs=[
            pl.BlockSpec((None, None, 8, PAIR_ST), lambda l, q: (l, q, 0, 0)),
            pl.BlockSpec((None, None, PAIR_ST, 8), lambda l, q: (l, q, 0, 0)),
            pl.BlockSpec((None, 2, None, SSM_GROUP, PAIR_ST), lambda l, q: (l, 0, q, 0, 0)),
            pl.BlockSpec((None, 2, None, SSM_GROUP, PAIR_ST), lambda l, q: (l, 0, q, 0, 0)),
            pl.BlockSpec((None, 2, None, PAIR_ST, SSM_GROUP), lambda l, q: (l, 0, q, 0, 0)),
            pl.BlockSpec((None, 2, None, PAIR_ST, SSM_GROUP), lambda l, q: (l, 0, q, 0, 0)),
            pl.BlockSpec((None, None, 1, PAIR_IN), lambda l, q: (l, q, 0, 0)),
        ],
        out_specs=[
            pl.BlockSpec((None, None, PAIR_IN, PAIR_IN + 4 * PAIR_ST), lambda l, q: (l, q, 0, 0)),
            pl.BlockSpec((None, None, 4 * PAIR_ST, PAIR_IN), lambda l, q: (l, q, 0, 0)),
            pl.BlockSpec((None, 4, PAIR_ST), lambda l, q: (l, 0, q)),
        ],
        out_shape=[
            jax.ShapeDtypeStruct((DEPTH, N_PAIR, PAIR_IN, PAIR_IN + 4 * PAIR_ST), bf16),
            jax.ShapeDtypeStruct((DEPTH, N_PAIR, 4 * PAIR_ST, PAIR_IN), bf16),
            jax.ShapeDtypeStruct((DEPTH, 4, SSM_GROUPS * SSM_STATE), f32),
        ],
        compiler_params=_params("arbitrary", "arbitrary"),
        name="s5_chunk_matrices",
    )(prow, pcol, bt_re, bt_im, ct_re, ct_im, d_vec)


def _chunk_scan(pr, pi, pows, n_rows, seq, reverse):
    pos = lax.broadcasted_iota(jnp.int32, (n_rows, 1), 0) & (seq - 1)
    shift, k = 1, 0
    while shift < seq:
        ar, ai = pows[k]
        if reverse:
            valid = pos < seq - shift
            sr = pltpu.roll(pr, n_rows - shift, axis=0)
            si = pltpu.roll(pi, n_rows - shift, axis=0)
        else:
            valid = pos >= shift
            sr = pltpu.roll(pr, shift, axis=0)
            si = pltpu.roll(pi, shift, axis=0)
        tr, ti = _cmul(ar, ai, sr, si)
        pr = pr + jnp.where(valid, tr, 0.0)
        pi = pi + jnp.where(valid, ti, 0.0)
        shift, k = 2 * shift, k + 1
    return pr, pi


def _s5_kernel(*refs):
    a_refs = refs[0:CHUNK]
    w1_ref, w2_ref, at_ref, h0_ref = refs[CHUNK:CHUNK + 4]
    y_ref, fin_ref = refs[CHUNK + 4:CHUNK + 6]
    r_scr, sp_scr, y_scr, sr_scr, si_scr = refs[CHUNK + 6:]

    lane_blk = lax.broadcasted_iota(jnp.int32, (1, LANES), 1) // PAIR_CH
    row_lat = lax.broadcasted_iota(jnp.int32, (ROWS_LAT, 1), 0)
    pos_ctx = lax.broadcasted_iota(jnp.int32, (ROWS_CTX, 1), 0) & (NC_CTX - 1)
    pos_lat = row_lat & (NC_LAT - 1)

    for p in range(PAIRS_PER_STEP):
        pieces = []
        for k in range(CHUNK // STEPS_PER_LANE_BLOCK):
            piece = None
            for jj in range(STEPS_PER_LANE_BLOCK):
                src = a_refs[STEPS_PER_LANE_BLOCK * k + jj][...]
                sh = (PAIR_CH * (jj - p)) % LANES
                rolled = src if sh == 0 else pltpu.roll(src, sh, axis=1)
                piece = rolled if piece is None else jnp.where(lane_blk == jj, rolled, piece)
            pieces.append(piece.astype(bf16))
        u_pair = jnp.concatenate(pieces, axis=1)
        r_scr[...] = _dot(u_pair, w1_ref[p])

        lanes_p = slice(p * PAIR_ST, (p + 1) * PAIR_ST)
        for d in range(2):
            pows = [(at_ref[2 * d:2 * d + 1, lanes_p], at_ref[2 * d + 1:2 * d + 2, lanes_p])]
            for _ in range(5):
                pows.append(_cmul(*pows[-1], *pows[-1]))
            c_re = PAIR_IN + 2 * d * PAIR_ST
            c_im = c_re + PAIR_ST
            reverse = d == 1

            sr, si = _chunk_scan(r_scr[0:ROWS_CTX, c_re:c_re + PAIR_ST],
                                 r_scr[0:ROWS_CTX, c_im:c_im + PAIR_ST],
                                 pows, ROWS_CTX, NC_CTX, reverse)
            sr_scr[...] = sr
            si_scr[...] = si
            last = 0 if reverse else NC_CTX - 1
            fin_ref[2 * d, :, lanes_p] = sr_scr[pl.ds(last, BATCH, stride=NC_CTX), :]
            fin_ref[2 * d + 1, :, lanes_p] = si_scr[pl.ds(last, BATCH, stride=NC_CTX), :]
            edge = pos_ctx == (NC_CTX - 1 if reverse else 0)
            back = ROWS_CTX - 1 if reverse else 1
            sp_scr[0:ROWS_CTX, 2 * d * PAIR_ST:(2 * d + 1) * PAIR_ST] = jnp.where(
                edge, 0.0, pltpu.roll(sr, back, axis=0))
            sp_scr[0:ROWS_CTX, (2 * d + 1) * PAIR_ST:(2 * d + 2) * PAIR_ST] = jnp.where(
                edge, 0.0, pltpu.roll(si, back, axis=0))

            h0r = jnp.where(row_lat < NC_LAT, h0_ref[2 * d, 0:1, lanes_p], h0_ref[2 * d, 1:2, lanes_p])
            h0i = jnp.where(row_lat < NC_LAT, h0_ref[2 * d + 1, 0:1, lanes_p],
                            h0_ref[2 * d + 1, 1:2, lanes_p])
            edge = pos_lat == (NC_LAT - 1 if reverse else 0)
            ahr, ahi = _cmul(*pows[0], h0r, h0i)
            pr = r_scr[ROWS_CTX:ROWS_S5, c_re:c_re + PAIR_ST] + jnp.where(edge, ahr, 0.0)
            pi = r_scr[ROWS_CTX:ROWS_S5, c_im:c_im + PAIR_ST] + jnp.where(edge, ahi, 0.0)
            sr, si = _chunk_scan(pr, pi, pows, ROWS_LAT, NC_LAT, reverse)
            back = ROWS_LAT - 1 if reverse else 1
            sp_scr[ROWS_CTX:ROWS_S5, 2 * d * PAIR_ST:(2 * d + 1) * PAIR_ST] = jnp.where(
                edge, h0r, pltpu.roll(sr, back, axis=0))
            sp_scr[ROWS_CTX:ROWS_S5, (2 * d + 1) * PAIR_ST:(2 * d + 2) * PAIR_ST] = jnp.where(
                edge, h0i, pltpu.roll(si, back, axis=0))

        y_scr[:, p * PAIR_IN:(p + 1) * PAIR_IN] = (
            r_scr[:, 0:PAIR_IN] + _dot(sp_scr[...].astype(bf16), w2_ref[p]))

    for j in range(CHUNK):
        k, jj = divmod(j, STEPS_PER_LANE_BLOCK)
        out = None
        for p in range(PAIRS_PER_STEP):
            src = y_scr[:, p * PAIR_IN + k * LANES:p * PAIR_IN + (k + 1) * LANES]
            sh = (PAIR_CH * (p - jj)) % LANES
            rolled = src if sh == 0 else pltpu.roll(src, sh, axis=1)
            out = rolled if out is None else jnp.where(lane_blk == p, rolled, out)
        y_ref[:, j, :] = out


def _s5_mixer(u_rows, w1, w2, a_chunk, h0, layer):
    n_state = SSM_GROUPS * SSM_STATE
    n_blk = SSM_WIDTH // LANES
    step_spec = lambda j: pl.BlockSpec((ROWS_S5, LANES), lambda g, j=j: (0, j * n_blk + g))
    w_pairs = lambda rows, cols: pl.BlockSpec((None, PAIRS_PER_STEP, rows, cols),
                                              lambda g: (layer, g, 0, 0))
    st_lanes = PAIRS_PER_STEP * PAIR_ST
    return pl.pallas_call(
        _s5_kernel,
        grid=(n_blk,),
        in_specs=[step_spec(j) for j in range(CHUNK)] + [
            w_pairs(PAIR_IN, PAIR_IN + 4 * PAIR_ST),
            w_pairs(4 * PAIR_ST, PAIR_IN),
            pl.BlockSpec((None, 4, st_lanes), lambda g: (layer, 0, g)),
            pl.BlockSpec((4, DEC_BATCH, st_lanes), lambda g: (0, 0, g)),
        ],
        out_specs=[
            pl.BlockSpec((ROWS_S5, CHUNK, LANES), lambda g: (0, 0, g)),
            pl.BlockSpec((4, BATCH, st_lanes), lambda g: (0, 0, g)),
        ],
        out_shape=[
            jax.ShapeDtypeStruct((ROWS_S5, CHUNK, SSM_WIDTH), f32),
            jax.ShapeDtypeStruct((4, BATCH, n_state), f32),
        ],
        scratch_shapes=[
            pltpu.VMEM((ROWS_S5, PAIR_IN + 4 * PAIR_ST), f32),
            pltpu.VMEM((ROWS_S5, 4 * PAIR_ST), f32),
            pltpu.VMEM((ROWS_S5, PAIRS_PER_STEP * PAIR_IN), f32),
            pltpu.VMEM((ROWS_CTX, PAIR_ST), f32),
            pltpu.VMEM((ROWS_CTX, PAIR_ST), f32),
        ],
        compiler_params=_params("arbitrary"),
        name="s5_chunked_mixer",
    )(*([u_rows] * CHUNK), w1, w2, a_chunk, h0)


def _shift_rows(v, period, n_rows):
    row = lax.broadcasted_iota(jnp.int32, (n_rows, 1), 0)
    pos = row & (period - 1)
    prev = jnp.where(pos == 0, 0.0, pltpu.roll(v, 1, axis=0))
    nxt = jnp.where(pos == period - 1, 0.0, pltpu.roll(v, n_rows - 1, axis=0))
    return prev, nxt


def _dwconv3_rows(v, w_ref, period, n_rows):
    prev, nxt = _shift_rows(v, period, n_rows)
    return prev * w_ref[0:1, :] + v * w_ref[1:2, :] + nxt * w_ref[2:3, :]


def _k1_kernel(x_ref, mod_ref, gpre_ref, wgb_ref, wgc_ref, whv_ref, wu_ref, cw_ref, gco_ref,
               cn_ref, u_ref, h_scr):
    i = pl.program_id(0)
    n = pl.program_id(1)

    @pl.when(n == 0)
    def _():
        r = jnp.maximum(i - (N_CTX // TM1 - 1), 0)
        sh1 = mod_ref[pl.ds(r, 1), 0:D_MODEL]
        sc1 = mod_ref[pl.ds(r, 1), D_MODEL:2 * D_MODEL]
        y = _rms_rows(x_ref[...], gpre_ref[...])
        h_scr[...] = (y * (1.0 + sc1) + sh1).astype(bf16)

    h = h_scr[...]
    u_ref[...] = _dot(h, wu_ref[...])
    v = _dot(h, wgc_ref[...]) * _dot(h, whv_ref[...])
    period = jnp.where(i < N_CTX // TM1, SEQ, GRID_W)
    v = _dwconv3_rows(v, cw_ref, period, TM1)
    co = _dot(h, wgb_ref[...]) * v
    rr = lax.broadcasted_iota(jnp.int32, (TC1, TC1), 0) // HEAD_DIM
    cc = lax.broadcasted_iota(jnp.int32, (TC1, TC1), 1) // HEAD_DIM
    avg = jnp.where(rr == cc, 1.0 / HEAD_DIM, 0.0).astype(bf16)
    hi, lo = _split_bf16(co * co)
    ms = _dot(hi, avg) + _dot(lo, avg)
    cn_ref[...] = (co * lax.rsqrt(ms + EPS) * gco_ref[...]).astype(bf16)


def _in_proj(x, mod, g_pre1, w_in_bf, conv_w, g_conv_out, layer):
    nb = CONV_WIDTH // TC1
    wspec = lambda off: pl.BlockSpec((None, D_MODEL, TC1), lambda i, n: (layer, 0, off * nb + n))
    return pl.pallas_call(
        _k1_kernel,
        grid=(N_TOK // TM1, nb),
        in_specs=[
            pl.BlockSpec((TM1, D_MODEL), lambda i, n: (i, 0)),
            pl.BlockSpec((None, 8, 6 * D_MODEL), lambda i, n: (layer, 0, 0)),
            pl.BlockSpec((None, 1, D_MODEL), lambda i, n: (layer, 0, 0)),
            wspec(0), wspec(1), wspec(2), wspec(3),
            pl.BlockSpec((None, 3, TC1), lambda i, n: (layer, 0, n)),
            pl.BlockSpec((None, 1, TC1), lambda i, n: (layer, 0, n)),
        ],
        out_specs=[
            pl.BlockSpec((TM1, TC1), lambda i, n: (i, n)),
            pl.BlockSpec((TM1, TC1), lambda i, n: (i, n)),
        ],
        out_shape=[
            jax.ShapeDtypeStruct((N_TOK, CONV_WIDTH), bf16),
            jax.ShapeDtypeStruct((N_TOK, SSM_WIDTH), f32),
        ],
        scratch_shapes=[pltpu.VMEM((TM1, D_MODEL), bf16)],
        compiler_params=_params("arbitrary", "arbitrary"),
        name="in_proj_conv_mixer",
    )(x, mod, g_pre1, w_in_bf, w_in_bf, w_in_bf, w_in_bf, conv_w, g_conv_out)


def _k3_kernel(x_ref, cn_ref, y_ref, mod_ref, wglu_ref, bglu_ref, gsso_ref, wout_ref,
               gpost1_ref, gpre2_ref, x1_ref, h2_ref):
    i = pl.program_id(0)
    n_ctx_tiles = N_CTX // TM3
    r = jnp.where(i < n_ctx_tiles, 0, 1 + (i - n_ctx_tiles) // (DEC_SEQ // TM3))
    gt1 = mod_ref[pl.ds(r, 1), 2 * D_MODEL:3 * D_MODEL]
    sh2 = mod_ref[pl.ds(r, 1), 3 * D_MODEL:4 * D_MODEL]
    sc2 = mod_ref[pl.ds(r, 1), 4 * D_MODEL:5 * D_MODEL]

    z = jax.nn.gelu(y_ref[...], approximate=True)
    gate = jax.nn.sigmoid(_dot(z.astype(bf16), wglu_ref[...]) + bglu_ref[...])
    sn = _rms_rows(z * gate, gsso_ref[...])
    mixed = (_dot(cn_ref[...], wout_ref[0:CONV_WIDTH, :])
             + _dot(sn.astype(bf16), wout_ref[CONV_WIDTH:CONV_WIDTH + SSM_WIDTH, :]))
    x1 = x_ref[...] + gt1 * _rms_rows(mixed, gpost1_ref[...])
    x1_ref[...] = x1
    h2_ref[...] = (_rms_rows(x1, gpre2_ref[...]) * (1.0 + sc2) + sh2).astype(bf16)


def _out_proj(x, cn, y, mod, w_glu_bf, b_glu, g_ssm_out, w_out_bf, g_post1, g_pre2, layer):
    vec = lambda width: pl.BlockSpec((None, 1, width), lambda i: (layer, 0, 0))
    return pl.pallas_call(
        _k3_kernel,
        grid=(N_TOK // TM3,),
        in_specs=[
            pl.BlockSpec((TM3, D_MODEL), lambda i: (i, 0)),
            pl.BlockSpec((TM3, CONV_WIDTH), lambda i: (i, 0)),
            pl.BlockSpec((TM3, SSM_WIDTH), lambda i: (i, 0)),
            pl.BlockSpec((None, 8, 6 * D_MODEL), lambda i: (layer, 0, 0)),
            pl.BlockSpec((None, SSM_WIDTH, SSM_WIDTH), lambda i: (layer, 0, 0)),
            vec(SSM_WIDTH), vec(SSM_WIDTH),
            pl.BlockSpec((None, CONV_WIDTH + SSM_WIDTH, D_MODEL), lambda i: (layer, 0, 0)),
            vec(D_MODEL), vec(D_MODEL),
        ],
        out_specs=[
            pl.BlockSpec((TM3, D_MODEL), lambda i: (i, 0)),
            pl.BlockSpec((TM3, D_MODEL), lambda i: (i, 0)),
        ],
        out_shape=[
            jax.ShapeDtypeStruct((N_TOK, D_MODEL), f32),
            jax.ShapeDtypeStruct((N_TOK, D_MODEL), bf16),
        ],
        compiler_params=_params("arbitrary"),
        name="glu_out_proj",
    )(x, cn, y, mod, w_glu_bf, b_glu, g_ssm_out, w_out_bf, g_post1, g_pre2)


def _k4_kernel(hm_ref, hp_ref, hn_ref, x1_ref, mod_ref, wa_ref, wg_ref, cwa_ref, cwg_ref,
               wd_ref, gpost2_ref, o_ref, hext_scr):
    i = pl.program_id(0)
    j = pl.program_id(1)
    n_ctx_tiles = N_CTX // TM4
    tiles_per_lat = DEC_SEQ // TM4
    is_lat = i >= n_ctx_tiles

    @pl.when(j == 0)
    def _():
        o_ref[...] = jnp.zeros((TM4, D_MODEL), f32)

    def finish(a, g):
        act = (g * jax.nn.sigmoid(g) * a).astype(bf16)
        o_ref[...] += _dot(act, wd_ref[...])

    @pl.when(jnp.logical_not(is_lat))
    def _():
        h = hm_ref[...]
        a = _dwconv3_rows(_dot(h, wa_ref[...]), cwa_ref, SEQ, TM4)
        g = _dwconv3_rows(_dot(h, wg_ref[...]), cwg_ref, SEQ, TM4)
        finish(a, g)

    @pl.when(is_lat)
    def _():
        @pl.when(j == 0)
        def _():
            t = (i - n_ctx_tiles) % tiles_per_lat
            hext_scr[0:HALO, :] = hp_ref[...]
            hext_scr[HALO:HALO + TM4, :] = hm_ref[...]
            hext_scr[HALO + TM4:2 * HALO + TM4, :] = hn_ref[...]

            @pl.when(t == 0)
            def _():
                hext_scr[0:HALO, :] = jnp.zeros((HALO, D_MODEL), bf16)

            @pl.when(t == tiles_per_lat - 1)
            def _():
                hext_scr[HALO + TM4:2 * HALO + TM4, :] = jnp.zeros((HALO, D_MODEL), bf16)

        h = hext_scr[...]

        def conv(up, w_ref):
            return (up[0:TM4] * w_ref[0:1, :] + up[HALO:HALO + TM4] * w_ref[1:2, :]
                    + up[2 * HALO:2 * HALO + TM4] * w_ref[2:3, :])

        finish(conv(_dot(h, wa_ref[...]), cwa_ref), conv(_dot(h, wg_ref[...]), cwg_ref))

    @pl.when(j == pl.num_programs(1) - 1)
    def _():
        r = jnp.where(is_lat, 1 + (i - n_ctx_tiles) // tiles_per_lat, 0)
        gt2 = mod_ref[pl.ds(r, 1), 5 * D_MODEL:6 * D_MODEL]
        o_ref[...] = x1_ref[...] + gt2 * _rms_rows(o_ref[...], gpost2_ref[...])


def _conv_ffn(h2, x1, mod, w_up_bf, ffn_conv_w, w_down_bf, g_post2, layer):
    nj = D_FF // TF4
    halo_per_tile = TM4 // HALO
    n_halo_blocks = N_TOK // HALO
    return pl.pallas_call(
        _k4_kernel,
        grid=(N_TOK // TM4, nj),
        in_specs=[
            pl.BlockSpec((TM4, D_MODEL), lambda i, j: (i, 0)),
            pl.BlockSpec((HALO, D_MODEL), lambda i, j: (jnp.maximum(i * halo_per_tile - 1, 0), 0)),
            pl.BlockSpec((HALO, D_MODEL),
                         lambda i, j: (jnp.minimum((i + 1) * halo_per_tile, n_halo_blocks - 1), 0)),
            pl.BlockSpec((TM4, D_MODEL), lambda i, j: (i, 0)),
            pl.BlockSpec((None, 8, 6 * D_MODEL), lambda i, j: (layer, 0, 0)),
            pl.BlockSpec((None, D_MODEL, TF4), lambda i, j: (layer, 0, j)),
            pl.BlockSpec((None, D_MODEL, TF4), lambda i, j: (layer, 0, nj + j)),
            pl.BlockSpec((None, 3, TF4), lambda i, j: (layer, 0, j)),
            pl.BlockSpec((None, 3, TF4), lambda i, j: (layer, 0, nj + j)),
            pl.BlockSpec((None, TF4, D_MODEL), lambda i, j: (layer, j, 0)),
            pl.BlockSpec((None, 1, D_MODEL), lambda i, j: (layer, 0, 0)),
        ],
        out_specs=pl.BlockSpec((TM4, D_MODEL), lambda i, j: (i, 0)),
        out_shape=jax.ShapeDtypeStruct((N_TOK, D_MODEL), f32),
        scratch_shapes=[pltpu.VMEM((TM4 + 2 * HALO, D_MODEL), bf16)],
        compiler_params=_params("arbitrary", "arbitrary"),
        name="conv_ffn",
    )(h2, h2, h2, x1, mod, w_up_bf, w_up_bf, ffn_conv_w, ffn_conv_w, w_down_bf, g_post2)


def kernel(x_prompt, x_sample, state_ssm_re, state_ssm_im, c, c_ctx, w_ada, b_ada, g_pre1, w_in, conv_w, ssm_a_re, ssm_a_im, ssm_log_dt, ssm_b_re, ssm_b_im, ssm_c_re, ssm_c_im, ssm_d, w_glu, b_glu, g_conv_out, g_ssm_out, w_out, g_post1, g_pre2, w_up, ffn_conv_w, w_down, g_post2):
    w_in_bf = _cast_bf16(w_in.reshape(DEPTH * D_MODEL, -1), 256).reshape(w_in.shape)
    w_glu_bf = _cast_bf16(w_glu.reshape(DEPTH * SSM_WIDTH, -1), 1024).reshape(w_glu.shape)
    w_out_bf = _cast_bf16(w_out.reshape(DEPTH * 2 * CONV_WIDTH, -1), 512).reshape(w_out.shape)
    w_up_bf = _cast_bf16(w_up.reshape(DEPTH * D_MODEL, -1), 64).reshape(w_up.shape)
    w_down_bf = _cast_bf16(w_down.reshape(DEPTH * D_FF, -1), 512).reshape(w_down.shape)

    cvec8 = jnp.concatenate([c_ctx[None, :], c, jnp.zeros((8 - 1 - DEC_BATCH, D_MODEL), f32)], axis=0)
    mod = _modulation(cvec8, w_ada, b_ada)

    abr, abi, bfr, bfi = [t.reshape(DEPTH, 2, N_PAIR, PAIR_ST)
                          for t in _discretise(ssm_a_re, ssm_a_im, ssm_log_dt)]
    prow = jnp.stack([abr, abi, bfr, bfi], axis=2)
    prow = prow.transpose(0, 3, 1, 2, 4).reshape(DEPTH, N_PAIR, 8, PAIR_ST)
    zcol = jnp.zeros_like(abr[:, 0])
    pcol = jnp.stack([abr[:, 0], abi[:, 0], abr[:, 1], abi[:, 1], zcol, zcol, zcol, zcol], axis=-1)

    def b_rows(b):
        b = b.transpose(0, 1, 2, 4, 3).reshape(DEPTH, 2, N_PAIR, 2, SSM_GROUP, SSM_STATE)
        return b.transpose(0, 1, 2, 4, 3, 5).reshape(DEPTH, 2, N_PAIR, SSM_GROUP, PAIR_ST)

    def c_cols(cm):
        return cm.transpose(0, 1, 2, 4, 3).reshape(DEPTH, 2, N_PAIR, PAIR_ST, SSM_GROUP)

    d_vec = jnp.broadcast_to(ssm_d.reshape(DEPTH, N_PAIR, 1, PAIR_CH), (DEPTH, N_PAIR, CHUNK, PAIR_CH))
    d_vec = d_vec.reshape(DEPTH, N_PAIR, 1, PAIR_IN)
    w1, w2, a_chunk = _s5_matrices(prow, pcol, b_rows(ssm_b_re), b_rows(ssm_b_im),
                                   c_cols(ssm_c_re), c_cols(ssm_c_im), d_vec)

    g3 = lambda g: g.reshape(DEPTH, 1, -1)
    g_pre1_, g_conv_out_, g_ssm_out_, g_post1_, g_pre2_, g_post2_, b_glu_ = map(
        g3, (g_pre1, g_conv_out, g_ssm_out, g_post1, g_pre2, g_post2, b_glu))

    x = jnp.concatenate([x_prompt.reshape(N_CTX, D_MODEL), x_sample.reshape(N_LAT, D_MODEL)], axis=0)
    fins = []
    for l in range(DEPTH):
        cn, u = _in_proj(x, mod, g_pre1_, w_in_bf, conv_w, g_conv_out_, l)
        sre = state_ssm_re[:, l].reshape(DEC_BATCH, 2, -1)
        sim = state_ssm_im[:, l].reshape(DEC_BATCH, 2, -1)
        h0 = jnp.stack([sre[:, 0], sim[:, 0], sre[:, 1], sim[:, 1]], axis=0)
        y, fin = _s5_mixer(u.reshape(ROWS_S5, CHUNK * SSM_WIDTH), w1, w2, a_chunk, h0, l)
        fins.append(fin)
        x1, h2 = _out_proj(x, cn, y.reshape(N_TOK, SSM_WIDTH), mod, w_glu_bf, b_glu_, g_ssm_out_,
                           w_out_bf, g_post1_, g_pre2_, l)
        x = _conv_ffn(h2, x1, mod, w_up_bf, ffn_conv_w, w_down_bf, g_post2_, l)

    fin = jnp.stack(fins, axis=0).reshape(DEPTH, 2, 2, BATCH, SSM_GROUPS, SSM_STATE)
    new_re = fin[:, :, 0].transpose(2, 0, 1, 3, 4)
    new_im = fin[:, :, 1].transpose(2, 0, 1, 3, 4)
    y_prompt = x[:N_CTX].reshape(BATCH, SEQ, D_MODEL)
    y_sample = x[N_CTX:].reshape(DEC_BATCH, DEC_SEQ, D_MODEL)
    return (y_prompt, y_sample, new_re, new_im)
```

```python
import functools

import jax
import jax.numpy as jnp
from jax import lax
from jax.experimental import pallas as pl
from jax.experimental.pallas import tpu as pltpu

D_MODEL = 2048
BATCH = 16
SEQ = 256
DEPTH = 4
DEC_BATCH = 2
DEC_SEQ = 1024
GRID_W = 64
CONV_WIDTH = 1024
HEAD_DIM = 64
SSM_WIDTH = 1024
SSM_GROUP = 16
SSM_GROUPS = 64
SSM_STATE = 64
D_FF = 5632
EPS = 1e-6

LANES = 128
N_CTX = BATCH * SEQ
N_LAT = DEC_BATCH * DEC_SEQ
N_TOK = N_CTX + N_LAT
CHUNK = 16
NC_CTX = SEQ // CHUNK
NC_LAT = DEC_SEQ // CHUNK
ROWS_CTX = BATCH * NC_CTX
ROWS_LAT = DEC_BATCH * NC_LAT
ROWS_S5 = ROWS_CTX + ROWS_LAT
N_PAIR = SSM_GROUPS // 2
PAIR_CH = 2 * SSM_GROUP
PAIR_IN = CHUNK * PAIR_CH
PAIR_ST = 2 * SSM_STATE
PAIRS_PER_STEP = LANES // PAIR_CH
STEPS_PER_LANE_BLOCK = LANES // PAIR_CH

TM1 = 1024
TC1 = 256
TM3 = 512
SUB3 = 256
TM4 = 512
TF4 = 512
LAT_STRIP = 32

VMEM_LIMIT = 56 * 1024 * 1024

f32 = jnp.float32
bf16 = jnp.bfloat16


def _dot(a, b):
    return jnp.dot(a, b, preferred_element_type=f32)


def _split_bf16(x):
    hi = x.astype(bf16)
    lo = (x - hi.astype(f32)).astype(bf16)
    return hi, lo


def _dot3(a, b):
    ah, al = _split_bf16(a)
    bh, bl = _split_bf16(b)
    return _dot(ah, bh) + (_dot(ah, bl) + _dot(al, bh))


def _dot_select(a, sel):
    a1 = a.astype(bf16)
    r1 = a - a1.astype(f32)
    a2 = r1.astype(bf16)
    a3 = (r1 - a2.astype(f32)).astype(bf16)
    return _dot(a1, sel) + (_dot(a2, sel) + _dot(a3, sel))


def _select_dot(sel, a):
    a1 = a.astype(bf16)
    r1 = a - a1.astype(f32)
    a2 = r1.astype(bf16)
    a3 = (r1 - a2.astype(f32)).astype(bf16)
    return _dot(sel, a1) + (_dot(sel, a2) + _dot(sel, a3))


def _rms_rows(x, g):
    ms = jnp.mean(x * x, axis=-1, keepdims=True)
    return x * lax.rsqrt(ms + EPS) * g


def _cmul(ar, ai, br, bi):
    return ar * br - ai * bi, ar * bi + ai * br


def _params(*sem):
    return pltpu.CompilerParams(dimension_semantics=sem, vmem_limit_bytes=VMEM_LIMIT)


def _cast_kernel(x_ref, o_ref):
    o_ref[...] = x_ref[...].astype(o_ref.dtype)


def _cast_layer_bf16(w, layer, block_rows):
    _, rows, cols = w.shape
    return pl.pallas_call(
        _cast_kernel,
        grid=(rows // block_rows,),
        in_specs=[pl.BlockSpec((None, block_rows, cols), lambda i: (layer, i, 0))],
        out_specs=pl.BlockSpec((block_rows, cols), lambda i: (i, 0)),
        out_shape=jax.ShapeDtypeStruct((rows, cols), bf16),
        compiler_params=_params("arbitrary"),
        name="cast_bf16",
    )(w)


def _mod_kernel(cv_ref, w_ref, b_ref, o_ref):
    cv = cv_ref[...]
    s = cv * jax.nn.sigmoid(cv)
    o_ref[...] = _dot(s.astype(bf16), w_ref[...].astype(bf16)) + b_ref[...]


def _modulation(cvec8, w_ada, b_ada):
    tn = 1024
    n_out = 6 * D_MODEL
    return pl.pallas_call(
        _mod_kernel,
        grid=(DEPTH, n_out // tn),
        in_specs=[
            pl.BlockSpec((8, D_MODEL), lambda l, n: (0, 0)),
            pl.BlockSpec((None, D_MODEL, tn), lambda l, n: (l, 0, n)),
            pl.BlockSpec((None, 1, tn), lambda l, n: (l, 0, n)),
        ],
        out_specs=pl.BlockSpec((None, 8, tn), lambda l, n: (l, 0, n)),
        out_shape=jax.ShapeDtypeStruct((DEPTH, 8, n_out), f32),
        compiler_params=_params("arbitrary", "arbitrary"),
        name="adaln_modulation",
    )(cvec8, w_ada, b_ada.reshape(DEPTH, 1, n_out))


def _disc_kernel(ar_ref, ai_ref, ldt_ref, abr_ref, abi_ref, bfr_ref, bfi_ref):
    ar = ar_ref[...]
    ai = ai_ref[...]
    dt = jnp.exp(ldt_ref[...])
    mag = jnp.exp(ar * dt)
    abr = mag * jnp.cos(ai * dt)
    abi = mag * jnp.sin(ai * dt)
    nr = abr - 1.0
    den = ar * ar + ai * ai
    abr_ref[...] = abr
    abi_ref[...] = abi
    bfr_ref[...] = (nr * ar + abi * ai) / den
    bfi_ref[...] = (abi * ar - nr * ai) / den


def _discretise(a_re, a_im, log_dt):
    shape = (DEPTH * 2, SSM_GROUPS * SSM_STATE)
    spec = pl.BlockSpec(shape, lambda: (0, 0))
    out = jax.ShapeDtypeStruct(shape, f32)
    return pl.pallas_call(
        _disc_kernel,
        in_specs=[spec, spec, spec],
        out_specs=[spec, spec, spec, spec],
        out_shape=[out, out, out, out],
        name="s5_discretise",
    )(a_re.reshape(shape), a_im.reshape(shape),
      jnp.broadcast_to(log_dt[..., None], (DEPTH, 2, SSM_GROUPS, SSM_STATE)).reshape(shape))


def _pow_table(br, bi, kk, shape, bits=4):
    tr = ti = None
    pr, pi = br, bi
    for bit in range(bits):
        sel = ((kk >> bit) & 1) == 1
        fr = jnp.broadcast_to(jnp.where(sel, pr, 1.0), shape)
        fi = jnp.broadcast_to(jnp.where(sel, pi, 0.0), shape)
        if tr is None:
            tr, ti = fr, fi
        else:
            tr, ti = _cmul(tr, ti, fr, fi)
        pr, pi = _cmul(pr, pi, pr, pi)
    return tr, ti


def _prep_kernel(pr_ref, pc_ref, btr_ref, bti_ref, ctr_ref, cti_ref, dv_ref,
                 w1_ref, w2_ref, at_ref):
    lane_in = lax.broadcasted_iota(jnp.int32, (1, PAIR_IN), 1)
    step_of_lane = lane_in >> 5
    grp_of_lane = (lane_in >> 4) & 1
    chan_of_lane = lane_in & (SSM_GROUP - 1)
    row_in = lax.broadcasted_iota(jnp.int32, (PAIR_IN, 1), 0)
    step_of_row = row_in >> 5
    lane_st = lax.broadcasted_iota(jnp.int32, (1, PAIR_ST), 1)
    grp_of_st_lane = lane_st >> 6
    row_st = lax.broadcasted_iota(jnp.int32, (PAIR_ST, 1), 0)
    grp_of_st_row = row_st >> 6
    chan_row = lax.broadcasted_iota(jnp.int32, (SSM_GROUP, 1), 0)

    lane_blk = lax.broadcasted_iota(jnp.int32, (1, LANES), 1)
    step_in_blk = lane_blk >> 5
    spread_ch = jnp.where((lane_blk & (SSM_GROUP - 1)) == chan_row, 1.0, 0.0).astype(bf16)
    same_grp = grp_of_st_row == ((lane_blk >> 4) & 1)
    n_blk = PAIR_IN // LANES

    taps = []
    for d in range(2):
        abr = pr_ref[4 * d + 0:4 * d + 1, :]
        abi = pr_ref[4 * d + 1:4 * d + 2, :]
        bfr = pr_ref[4 * d + 2:4 * d + 3, :]
        bfi = pr_ref[4 * d + 3:4 * d + 4, :]
        a2 = _cmul(abr, abi, abr, abi)
        a4 = _cmul(*a2, *a2)
        a8 = _cmul(*a4, *a4)
        a16 = _cmul(*a8, *a8)
        at_ref[2 * d:2 * d + 1, :] = a16[0]
        at_ref[2 * d + 1:2 * d + 2, :] = a16[1]
        bbr, bbi = _cmul(bfr, bfi, btr_ref[d], bti_ref[d])
        kk_rows = (CHUNK - 1 - chan_row) if d == 0 else chan_row
        pcr, pci = _pow_table(abr, abi, kk_rows, (CHUNK, PAIR_ST))
        pbr = jnp.concatenate([jnp.broadcast_to(pcr[j:j + 1, :], (PAIR_CH, PAIR_ST))
                               for j in range(CHUNK)], axis=0)
        pbi = jnp.concatenate([jnp.broadcast_to(pci[j:j + 1, :], (PAIR_CH, PAIR_ST))
                               for j in range(CHUNK)], axis=0)
        slab_r = jnp.concatenate([jnp.where(grp_of_st_lane == e, bbr, 0.0) for e in range(2)], axis=0)
        slab_i = jnp.concatenate([jnp.where(grp_of_st_lane == e, bbi, 0.0) for e in range(2)], axis=0)
        tile_r = jnp.concatenate([slab_r] * CHUNK, axis=0)
        tile_i = jnp.concatenate([slab_i] * CHUNK, axis=0)
        wbr, wbi = _cmul(pbr, pbi, tile_r, tile_i)
        for part, wb in ((0, wbr), (1, wbi)):
            c0 = PAIR_IN + (2 * d + part) * PAIR_ST
            w1_ref[:, c0:c0 + PAIR_ST] = wb.astype(bf16)

        acr = pc_ref[:, 2 * d:2 * d + 1]
        aci = pc_ref[:, 2 * d + 1:2 * d + 2]
        ctr = jnp.where(same_grp, _dot_select(ctr_ref[d], spread_ch), 0.0)
        cti = jnp.where(same_grp, _dot_select(cti_ref[d], spread_ch), 0.0)
        kk_blk = (step_in_blk + 1) if d == 0 else (STEPS_PER_LANE_BLOCK - step_in_blk)
        m0r, m0i = _cmul(*_pow_table(acr, aci, kk_blk, (PAIR_ST, LANES), bits=3), ctr, cti)
        c2 = _cmul(acr, aci, acr, aci)
        c4 = _cmul(*c2, *c2)
        c8 = _cmul(*c4, *c4)
        c12 = _cmul(*c8, *c4)
        blocks = [(m0r, m0i)] + [_cmul(*cm, m0r, m0i) for cm in (c4, c8, c12)]
        if d == 1:
            blocks = blocks[::-1]
        ca1r = jnp.concatenate([b[0] for b in blocks], axis=1)
        ca1i = jnp.concatenate([b[1] for b in blocks], axis=1)
        ctr4 = jnp.concatenate([ctr] * n_blk, axis=1)
        cti4 = jnp.concatenate([cti] * n_blk, axis=1)
        if d == 0:
            ca0r = jnp.where(lane_in < PAIR_CH, ctr4, pltpu.roll(ca1r, PAIR_CH, axis=1))
            ca0i = jnp.where(lane_in < PAIR_CH, cti4, pltpu.roll(ca1i, PAIR_CH, axis=1))
        else:
            ca0r = jnp.where(lane_in >= PAIR_IN - PAIR_CH, ctr4, pltpu.roll(ca1r, PAIR_IN - PAIR_CH, axis=1))
            ca0i = jnp.where(lane_in >= PAIR_IN - PAIR_CH, cti4, pltpu.roll(ca1i, PAIR_IN - PAIR_CH, axis=1))
        w2_ref[(2 * d) * PAIR_ST:(2 * d + 1) * PAIR_ST, :] = ca1r.astype(bf16)
        w2_ref[(2 * d + 1) * PAIR_ST:(2 * d + 2) * PAIR_ST, :] = (-ca1i).astype(bf16)
        taps.append(_dot3(bbr, ca0r) - _dot3(bbi, ca0i))

    gf, gb = taps
    dvec = dv_ref[...]
    for jp in range(CHUNK):
        lo = PAIR_CH * jp
        hi = PAIR_CH * (jp + 1)
        rf = gf if jp == 0 else pltpu.roll(gf, lo, axis=1)
        rb = gb if jp == CHUNK - 1 else pltpu.roll(gb, hi, axis=1)
        blk = jnp.where(lane_in >= lo, rf, 0.0) + jnp.where(lane_in < hi, rb, 0.0)
        on_diag = (step_of_lane == jp) & (chan_of_lane == chan_row)
        blk = blk + jnp.where(on_diag, dvec, 0.0)
        for e in range(2):
            r0 = lo + e * SSM_GROUP
            w1_ref[r0:r0 + SSM_GROUP, 0:PAIR_IN] = jnp.where(grp_of_lane == e, blk, 0.0).astype(bf16)


def _s5_matrices(prow, pcol, bt_re, bt_im, ct_re, ct_im, d_vec):
    return pl.pallas_call(
        _prep_kernel,
        grid=(DEPTH, N_PAIR),
        in_specs=[
            pl.BlockSpec((None, None, 8, PAIR_ST), lambda l, q: (l, q, 0, 0)),
            pl.BlockSpec((None, None, PAIR_ST, 8), lambda l, q: (l, q, 0, 0)),
            pl.BlockSpec((None, 2, None, SSM_GROUP, PAIR_ST), lambda l, q: (l, 0, q, 0, 0)),
            pl.BlockSpec((None, 2, None, SSM_GROUP, PAIR_ST), lambda l, q: (l, 0, q, 0, 0)),
            pl.BlockSpec((None, 2, None, PAIR_ST, SSM_GROUP), lambda l, q: (l, 0, q, 0, 0)),
            pl.BlockSpec((None, 2, None, PAIR_ST, SSM_GROUP), lambda l, q: (l, 0, q, 0, 0)),
            pl.BlockSpec((None, None, 1, PAIR_IN), lambda l, q: (l, q, 0, 0)),
        ],
        out_specs=[
            pl.BlockSpec((None, None, PAIR_IN, PAIR_IN + 4 * PAIR_ST), lambda l, q: (l, q, 0, 0)),
            pl.BlockSpec((None, None, 4 * PAIR_ST, PAIR_IN), lambda l, q: (l, q, 0, 0)),
            pl.BlockSpec((None, 4, PAIR_ST), lambda l, q: (l, 0, q)),
        ],
        out_shape=[
            jax.ShapeDtypeStruct((DEPTH, N_PAIR, PAIR_IN, PAIR_IN + 4 * PAIR_ST), bf16),
            jax.ShapeDtypeStruct((DEPTH, N_PAIR, 4 * PAIR_ST, PAIR_IN), bf16),
            jax.ShapeDtypeStruct((DEPTH, 4, SSM_GROUPS * SSM_STATE), f32),
        ],
        compiler_params=_params("arbitrary", "arbitrary"),
        name="s5_chunk_matrices",
    )(prow, pcol, bt_re, bt_im, ct_re, ct_im, d_vec)


def _chunk_scan(pr, pi, pows, n_rows, seq, reverse):
    pos = lax.broadcasted_iota(jnp.int32, (n_rows, 1), 0) & (seq - 1)
    shift, k = 1, 0
    while shift < seq:
        ar, ai = pows[k]
        if reverse:
            valid = pos < seq - shift
            sr = pltpu.roll(pr, n_rows - shift, axis=0)
            si = pltpu.roll(pi, n_rows - shift, axis=0)
        else:
            valid = pos >= shift
            sr = pltpu.roll(pr, shift, axis=0)
            si = pltpu.roll(pi, shift, axis=0)
        tr, ti = _cmul(ar, ai, sr, si)
        pr = pr + jnp.where(valid, tr, 0.0)
        pi = pi + jnp.where(valid, ti, 0.0)
        shift, k = 2 * shift, k + 1
    return pr, pi


def _s5_kernel(*refs):
    a_refs = refs[0:CHUNK]
    w1_ref, w2_ref, at_ref, h0_ref = refs[CHUNK:CHUNK + 4]
    y_ref, fin_ref = refs[CHUNK + 4:CHUNK + 6]
    r_scr, sp_scr, y_scr, sr_scr, si_scr = refs[CHUNK + 6:]

    lane_blk = lax.broadcasted_iota(jnp.int32, (1, LANES), 1) // PAIR_CH
    row_lat = lax.broadcasted_iota(jnp.int32, (ROWS_LAT, 1), 0)
    pos_ctx = lax.broadcasted_iota(jnp.int32, (ROWS_CTX, 1), 0) & (NC_CTX - 1)
    pos_lat = row_lat & (NC_LAT - 1)

    for p in range(PAIRS_PER_STEP):
        pieces = []
        for k in range(CHUNK // STEPS_PER_LANE_BLOCK):
            piece = None
            for jj in range(STEPS_PER_LANE_BLOCK):
                src = a_refs[STEPS_PER_LANE_BLOCK * k + jj][...]
                sh = (PAIR_CH * (jj - p)) % LANES
                rolled = src if sh == 0 else pltpu.roll(src, sh, axis=1)
                piece = rolled if piece is None else jnp.where(lane_blk == jj, rolled, piece)
            pieces.append(piece.astype(bf16))
        u_pair = jnp.concatenate(pieces, axis=1)
        r_scr[...] = _dot(u_pair, w1_ref[p])

        lanes_p = slice(p * PAIR_ST, (p + 1) * PAIR_ST)
        for d in range(2):
            pows = [(at_ref[2 * d:2 * d + 1, lanes_p], at_ref[2 * d + 1:2 * d + 2, lanes_p])]
            for _ in range(5):
                pows.append(_cmul(*pows[-1], *pows[-1]))
            c_re = PAIR_IN + 2 * d * PAIR_ST
            c_im = c_re + PAIR_ST
            reverse = d == 1

            sr, si = _chunk_scan(r_scr[0:ROWS_CTX, c_re:c_re + PAIR_ST],
                                 r_scr[0:ROWS_CTX, c_im:c_im + PAIR_ST],
                                 pows, ROWS_CTX, NC_CTX, reverse)
            sr_scr[...] = sr
            si_scr[...] = si
            last = 0 if reverse else NC_CTX - 1
            fin_ref[2 * d, :, lanes_p] = sr_scr[pl.ds(last, BATCH, stride=NC_CTX), :]
            fin_ref[2 * d + 1, :, lanes_p] = si_scr[pl.ds(last, BATCH, stride=NC_CTX), :]
            edge = pos_ctx == (NC_CTX - 1 if reverse else 0)
            back = ROWS_CTX - 1 if reverse else 1
            sp_scr[0:ROWS_CTX, 2 * d * PAIR_ST:(2 * d + 1) * PAIR_ST] = jnp.where(
                edge, 0.0, pltpu.roll(sr, back, axis=0))
            sp_scr[0:ROWS_CTX, (2 * d + 1) * PAIR_ST:(2 * d + 2) * PAIR_ST] = jnp.where(
                edge, 0.0, pltpu.roll(si, back, axis=0))

            h0r = jnp.where(row_lat < NC_LAT, h0_ref[2 * d, 0:1, lanes_p], h0_ref[2 * d, 1:2, lanes_p])
            h0i = jnp.where(row_lat < NC_LAT, h0_ref[2 * d + 1, 0:1, lanes_p],
                            h0_ref[2 * d + 1, 1:2, lanes_p])
            edge = pos_lat == (NC_LAT - 1 if reverse else 0)
            ahr, ahi = _cmul(*pows[0], h0r, h0i)
            pr = r_scr[ROWS_CTX:ROWS_S5, c_re:c_re + PAIR_ST] + jnp.where(edge, ahr, 0.0)
            pi = r_scr[ROWS_CTX:ROWS_S5, c_im:c_im + PAIR_ST] + jnp.where(edge, ahi, 0.0)
            sr, si = _chunk_scan(pr, pi, pows, ROWS_LAT, NC_LAT, reverse)
            back = ROWS_LAT - 1 if reverse else 1
            sp_scr[ROWS_CTX:ROWS_S5, 2 * d * PAIR_ST:(2 * d + 1) * PAIR_ST] = jnp.where(
                edge, h0r, pltpu.roll(sr, back, axis=0))
            sp_scr[ROWS_CTX:ROWS_S5, (2 * d + 1) * PAIR_ST:(2 * d + 2) * PAIR_ST] = jnp.where(
                edge, h0i, pltpu.roll(si, back, axis=0))

        y_scr[:, p * PAIR_IN:(p + 1) * PAIR_IN] = (
            r_scr[:, 0:PAIR_IN] + _dot(sp_scr[...].astype(bf16), w2_ref[p]))

    for j in range(CHUNK):
        k, jj = divmod(j, STEPS_PER_LANE_BLOCK)
        out = None
        for p in range(PAIRS_PER_STEP):
            src = y_scr[:, p * PAIR_IN + k * LANES:p * PAIR_IN + (k + 1) * LANES]
            sh = (PAIR_CH * (p - jj)) % LANES
            rolled = src if sh == 0 else pltpu.roll(src, sh, axis=1)
            out = rolled if out is None else jnp.where(lane_blk == p, rolled, out)
        y_ref[:, j, :] = out


def _s5_mixer(u_rows, w1, w2, a_chunk, h0, layer):
    n_state = SSM_GROUPS * SSM_STATE
    n_blk = SSM_WIDTH // LANES
    step_spec = lambda j: pl.BlockSpec((None, ROWS_S5, LANES), lambda g, j=j: (j, 0, g))
    w_pairs = lambda rows, cols: pl.BlockSpec((None, PAIRS_PER_STEP, rows, cols),
                                              lambda g: (layer, g, 0, 0))
    st_lanes = PAIRS_PER_STEP * PAIR_ST
    return pl.pallas_call(
        _s5_kernel,
        grid=(n_blk,),
        in_specs=[step_spec(j) for j in range(CHUNK)] + [
            w_pairs(PAIR_IN, PAIR_IN + 4 * PAIR_ST),
            w_pairs(4 * PAIR_ST, PAIR_IN),
            pl.BlockSpec((None, 4, st_lanes), lambda g: (layer, 0, g)),
            pl.BlockSpec((4, DEC_BATCH, st_lanes), lambda g: (0, 0, g)),
        ],
        out_specs=[
            pl.BlockSpec((ROWS_S5, CHUNK, LANES), lambda g: (0, 0, g)),
            pl.BlockSpec((4, BATCH, st_lanes), lambda g: (0, 0, g)),
        ],
        out_shape=[
            jax.ShapeDtypeStruct((ROWS_S5, CHUNK, SSM_WIDTH), f32),
            jax.ShapeDtypeStruct((4, BATCH, n_state), f32),
        ],
        scratch_shapes=[
            pltpu.VMEM((ROWS_S5, PAIR_IN + 4 * PAIR_ST), f32),
            pltpu.VMEM((ROWS_S5, 4 * PAIR_ST), f32),
            pltpu.VMEM((ROWS_S5, PAIRS_PER_STEP * PAIR_IN), f32),
            pltpu.VMEM((ROWS_CTX, PAIR_ST), f32),
            pltpu.VMEM((ROWS_CTX, PAIR_ST), f32),
        ],
        compiler_params=_params("arbitrary"),
        name="s5_chunked_mixer",
    )(*([u_rows] * CHUNK), w1, w2, a_chunk, h0)


def _shift_rows(v, period, n_rows, shift):
    row = lax.broadcasted_iota(jnp.int32, (n_rows, 1), 0)
    pos = row & (period - 1)
    prev = jnp.where(pos < shift, 0.0, pltpu.roll(v, shift, axis=0))
    nxt = jnp.where(pos >= period - shift, 0.0, pltpu.roll(v, n_rows - shift, axis=0))
    return prev, nxt


def _dwconv3_rows(v, w_ref, period, n_rows, shift=1):
    prev, nxt = _shift_rows(v, period, n_rows, shift)
    return prev * w_ref[0:1, :] + v * w_ref[1:2, :] + nxt * w_ref[2:3, :]


def _k1_kernel(x_ref, mod_ref, gpre_ref, wgb_ref, wgc_ref, whv_ref, wu_ref, cw_ref, gco_ref,
               cn_ref, u_ref, h_scr, u_scr):
    i = pl.program_id(0)
    n = pl.program_id(1)

    @pl.when(n == 0)
    def _():
        r = jnp.maximum(i - (N_CTX // TM1 - 1), 0)
        sh1 = mod_ref[pl.ds(r, 1), 0:D_MODEL]
        sc1 = mod_ref[pl.ds(r, 1), D_MODEL:2 * D_MODEL]
        y = _rms_rows(x_ref[...], gpre_ref[...])
        h_scr[...] = (y * (1.0 + sc1) + sh1).astype(bf16)

    h = h_scr[...]
    u = _dot(h, wu_ref[...])
    for t in range(TC1 // LANES):
        u_scr[t] = u[:, t * LANES:(t + 1) * LANES]
    for j in range(CHUNK):
        for t in range(TC1 // LANES):
            u_ref[j, :, t * LANES:(t + 1) * LANES] = u_scr[t, pl.ds(j, TM1 // CHUNK, stride=CHUNK), :]
    v =_dot(h, wgc_ref[...]) * _dot(h, whv_ref[...])
    period = jnp.where(i < N_CTX // TM1, SEQ, GRID_W)
    v = _dwconv3_rows(v, cw_ref, period, TM1)
    co = _dot(h, wgb_ref[...]) * v
    rr = lax.broadcasted_iota(jnp.int32, (TC1, TC1), 0) // HEAD_DIM
    cc = lax.broadcasted_iota(jnp.int32, (TC1, TC1), 1) // HEAD_DIM
    avg = jnp.where(rr == cc, 1.0 / HEAD_DIM, 0.0).astype(bf16)
    hi, lo = _split_bf16(co * co)
    ms = _dot(hi, avg) + _dot(lo, avg)
    cn_ref[...] = (co * lax.rsqrt(ms + EPS) * gco_ref[...]).astype(bf16)


def _in_proj(x, mod, g_pre1, w_in_bf, conv_w, g_conv_out, layer):
    nb = CONV_WIDTH // TC1
    wspec = lambda off: pl.BlockSpec((D_MODEL, TC1), lambda i, n: (0, off * nb + n))
    return pl.pallas_call(
        _k1_kernel,
        grid=(N_TOK // TM1, nb),
        in_specs=[
            pl.BlockSpec((TM1, D_MODEL), lambda i, n: (i, 0)),
            pl.BlockSpec((None, 8, 6 * D_MODEL), lambda i, n: (layer, 0, 0)),
            pl.BlockSpec((None, 1, D_MODEL), lambda i, n: (layer, 0, 0)),
            wspec(0), wspec(1), wspec(2), wspec(3),
            pl.BlockSpec((None, 3, TC1), lambda i, n: (layer, 0, n)),
            pl.BlockSpec((None, 1, TC1), lambda i, n: (layer, 0, n)),
        ],
        out_specs=[
            pl.BlockSpec((TM1, TC1), lambda i, n: (i, n)),
            pl.BlockSpec((CHUNK, TM1 // CHUNK, TC1), lambda i, n: (0, i, n)),
        ],
        out_shape=[
            jax.ShapeDtypeStruct((N_TOK, CONV_WIDTH), bf16),
            jax.ShapeDtypeStruct((CHUNK, ROWS_S5, SSM_WIDTH), f32),
        ],
        scratch_shapes=[pltpu.VMEM((TM1, D_MODEL), bf16),
                        pltpu.VMEM((TC1 // LANES, TM1, LANES), f32)],
        compiler_params=_params("arbitrary", "arbitrary"),
        name="in_proj_conv_mixer",
    )(x, mod, g_pre1, w_in_bf, w_in_bf, w_in_bf, w_in_bf, conv_w, g_conv_out)


def _k3_kernel(x_ref, cn_ref, y_ref, mod_ref, wglu_ref, bglu_ref, gsso_ref, wout_ref,
               gpost1_ref, gpre2_ref, x1_ref, h2_ref):
    i = pl.program_id(0)
    n_ctx_tiles = N_CTX // TM3
    r = jnp.where(i < n_ctx_tiles, 0, 1 + (i - n_ctx_tiles) // (DEC_SEQ // TM3))
    gt1 = mod_ref[pl.ds(r, 1), 2 * D_MODEL:3 * D_MODEL]
    sh2 = mod_ref[pl.ds(r, 1), 3 * D_MODEL:4 * D_MODEL]
    sc2 = mod_ref[pl.ds(r, 1), 4 * D_MODEL:5 * D_MODEL]

    for s in range(TM3 // SUB3):
        rows = slice(s * SUB3, (s + 1) * SUB3)
        z = jax.nn.gelu(y_ref[rows, :], approximate=True)
        gate = jax.nn.sigmoid(_dot(z.astype(bf16), wglu_ref[...]) + bglu_ref[...])
        sn = _rms_rows(z * gate, gsso_ref[...])
        mixed = (_dot(cn_ref[rows, :], wout_ref[0:CONV_WIDTH, :])
                 + _dot(sn.astype(bf16), wout_ref[CONV_WIDTH:CONV_WIDTH + SSM_WIDTH, :]))
        x1 = x_ref[rows, :] + gt1 * _rms_rows(mixed, gpost1_ref[...])
        x1_ref[rows, :] = x1
        h2_ref[rows, :] = (_rms_rows(x1, gpre2_ref[...]) * (1.0 + sc2) + sh2).astype(bf16)


def _out_proj(x, cn, y, mod, w_glu_bf, b_glu, g_ssm_out, w_out_bf, g_post1, g_pre2, layer):
    vec = lambda width: pl.BlockSpec((None, 1, width), lambda i: (layer, 0, 0))
    return pl.pallas_call(
        _k3_kernel,
        grid=(N_TOK // TM3,),
        in_specs=[
            pl.BlockSpec((TM3, D_MODEL), lambda i: (i, 0)),
            pl.BlockSpec((TM3, CONV_WIDTH), lambda i: (i, 0)),
            pl.BlockSpec((TM3, SSM_WIDTH), lambda i: (i, 0)),
            pl.BlockSpec((None, 8, 6 * D_MODEL), lambda i: (layer, 0, 0)),
            pl.BlockSpec((SSM_WIDTH, SSM_WIDTH), lambda i: (0, 0)),
            vec(SSM_WIDTH), vec(SSM_WIDTH),
            pl.BlockSpec((CONV_WIDTH + SSM_WIDTH, D_MODEL), lambda i: (0, 0)),
            vec(D_MODEL), vec(D_MODEL),
        ],
        out_specs=[
            pl.BlockSpec((TM3, D_MODEL), lambda i: (i, 0)),
            pl.BlockSpec((TM3, D_MODEL), lambda i: (i, 0)),
        ],
        out_shape=[
            jax.ShapeDtypeStruct((N_TOK, D_MODEL), f32),
            jax.ShapeDtypeStruct((N_TOK, D_MODEL), bf16),
        ],
        compiler_params=_params("arbitrary"),
        name="glu_out_proj",
    )(x, cn, y, mod, w_glu_bf, b_glu, g_ssm_out, w_out_bf, g_post1, g_pre2)


def _k4_kernel(h_ref, x1_ref, mod_ref, wa_ref, wg_ref, cwa_ref, cwg_ref, wd_ref, gpost2_ref,
               *rest, latent):
    rest = rest[1:] if latent else rest
    n_cast = (len(rest) - 1) // 2
    o_ref = rest[n_cast]
    for src, dst in zip(rest[:n_cast], rest[n_cast + 1:]):
        dst[...] = src[...].astype(bf16)

    i = pl.program_id(0)
    j = pl.program_id(1)
    blk = o_ref.shape

    @pl.when(j == 0)
    def _():
        o_ref[...] = jnp.zeros(blk, f32)

    period, shift = (TM4, LAT_STRIP) if latent else (SEQ, 1)
    h = h_ref[...].reshape(TM4, D_MODEL)
    a = _dwconv3_rows(_dot(h, wa_ref[...]), cwa_ref, period, TM4, shift)
    g = _dwconv3_rows(_dot(h, wg_ref[...]), cwg_ref, period, TM4, shift)
    act = (g * jax.nn.sigmoid(g) * a).astype(bf16)
    o_ref[...] += _dot(act, wd_ref[...]).reshape(blk)

    @pl.when(j == pl.num_programs(1) - 1)
    def _():
        r = (1 + i // (GRID_W // LAT_STRIP)) if latent else 0
        gt2 = mod_ref[pl.ds(r, 1), 5 * D_MODEL:6 * D_MODEL]
        ff = o_ref[...].reshape(TM4, D_MODEL)
        x2 = x1_ref[...].reshape(TM4, D_MODEL) + gt2 * _rms_rows(ff, gpost2_ref[...])
        o_ref[...] = x2.reshape(blk)


def _conv_ffn(h2, x1, mod, w_up_bf, ffn_conv_w, w_down_bf, g_post2, layer, next_f32=()):
    nj = D_FF // TF4
    n_ctx_tiles = N_CTX // TM4
    cast_in, cast_out, cast_shapes = [], [], []
    for w in next_f32:
        _, rows, cols = w.shape
        if cols % nj == 0:
            blk = (rows // n_ctx_tiles, cols // nj)
            idx = lambda i, j: (i, j)
        elif rows % nj == 0:
            blk = (rows // nj, cols // n_ctx_tiles)
            idx = lambda i, j: (j, i)
        else:
            blk = (rows // n_ctx_tiles, cols // n_ctx_tiles)
            idx = lambda i, j: (i, jnp.minimum(j, n_ctx_tiles - 1))
        cast_in.append(pl.BlockSpec((None,) + blk, lambda i, j, idx=idx: (layer + 1,) + idx(i, j)))
        cast_out.append(pl.BlockSpec(blk, idx))
        cast_shapes.append(jax.ShapeDtypeStruct((rows, cols), bf16))

    weight_specs = [
        pl.BlockSpec((None, 8, 6 * D_MODEL), lambda i, j: (layer, 0, 0)),
        pl.BlockSpec((D_MODEL, TF4), lambda i, j: (0, j)),
        pl.BlockSpec((D_MODEL, TF4), lambda i, j: (0, nj + j)),
        pl.BlockSpec((None, 3, TF4), lambda i, j: (layer, 0, j)),
        pl.BlockSpec((None, 3, TF4), lambda i, j: (layer, 0, nj + j)),
        pl.BlockSpec((TF4, D_MODEL), lambda i, j: (j, 0)),
        pl.BlockSpec((None, 1, D_MODEL), lambda i, j: (layer, 0, 0)),
    ]
    weights = (mod, w_up_bf, w_up_bf, ffn_conv_w, ffn_conv_w, w_down_bf, g_post2)

    row_spec = pl.BlockSpec((TM4, D_MODEL), lambda i, j: (i, 0))
    res = pl.pallas_call(
        functools.partial(_k4_kernel, latent=False),
        grid=(n_ctx_tiles, nj),
        in_specs=[row_spec, row_spec] + weight_specs + cast_in,
        out_specs=[row_spec] + cast_out,
        out_shape=[jax.ShapeDtypeStruct((N_TOK, D_MODEL), f32)] + cast_shapes,
        compiler_params=_params("arbitrary", "arbitrary"),
        name="conv_ffn_ctx",
    )(h2, x1, *weights, *next_f32)

    view = lambda t: t.reshape(N_TOK // GRID_W, GRID_W, D_MODEL)
    rows_per_img = DEC_SEQ // GRID_W
    strips = GRID_W // LAT_STRIP
    strip_spec = pl.BlockSpec((rows_per_img, LAT_STRIP, D_MODEL),
                              lambda i, j: (N_CTX // DEC_SEQ + i // strips, i % strips, 0))
    x2 = pl.pallas_call(
        functools.partial(_k4_kernel, latent=True),
        grid=(DEC_BATCH * strips, nj),
        in_specs=[strip_spec, strip_spec] + weight_specs + [pl.BlockSpec(memory_space=pl.ANY)],
        out_specs=strip_spec,
        out_shape=jax.ShapeDtypeStruct((N_TOK // GRID_W, GRID_W, D_MODEL), f32),
        input_output_aliases={2 + len(weights): 0},
        compiler_params=_params("arbitrary", "arbitrary"),
        name="conv_ffn_lat",
    )(view(h2), view(x1), *weights, view(res[0]))
    return x2.reshape(N_TOK, D_MODEL), tuple(res[1:])


def kernel(x_prompt, x_sample, state_ssm_re, state_ssm_im, c, c_ctx, w_ada, b_ada, g_pre1, w_in, conv_w, ssm_a_re, ssm_a_im, ssm_log_dt, ssm_b_re, ssm_b_im, ssm_c_re, ssm_c_im, ssm_d, w_glu, b_glu, g_conv_out, g_ssm_out, w_out, g_post1, g_pre2, w_up, ffn_conv_w, w_down, g_post2):
    f32_weights = (w_in, w_glu, w_out, w_up, w_down)
    bf_weights = tuple(_cast_layer_bf16(w, 0, rows)
                       for w, rows in zip(f32_weights, (256, 1024, 512, 64, 512)))

    cvec8 = jnp.concatenate([c_ctx[None, :], c, jnp.zeros((8 - 1 - DEC_BATCH, D_MODEL), f32)], axis=0)
    mod = _modulation(cvec8, w_ada, b_ada)

    abr, abi, bfr, bfi = [t.reshape(DEPTH, 2, N_PAIR, PAIR_ST)
                          for t in _discretise(ssm_a_re, ssm_a_im, ssm_log_dt)]
    prow = jnp.stack([abr, abi, bfr, bfi], axis=2)
    prow = prow.transpose(0, 3, 1, 2, 4).reshape(DEPTH, N_PAIR, 8, PAIR_ST)
    zcol = jnp.zeros_like(abr[:, 0])
    pcol = jnp.stack([abr[:, 0], abi[:, 0], abr[:, 1], abi[:, 1], zcol, zcol, zcol, zcol], axis=-1)

    def b_rows(b):
        b = b.transpose(0, 1, 2, 4, 3).reshape(DEPTH, 2, N_PAIR, 2, SSM_GROUP, SSM_STATE)
        return b.transpose(0, 1, 2, 4, 3, 5).reshape(DEPTH, 2, N_PAIR, SSM_GROUP, PAIR_ST)

    def c_cols(cm):
        return cm.transpose(0, 1, 2, 4, 3).reshape(DEPTH, 2, N_PAIR, PAIR_ST, SSM_GROUP)

    d_vec = jnp.broadcast_to(ssm_d.reshape(DEPTH, N_PAIR, 1, PAIR_CH), (DEPTH, N_PAIR, CHUNK, PAIR_CH))
    d_vec = d_vec.reshape(DEPTH, N_PAIR, 1, PAIR_IN)
    w1, w2, a_chunk = _s5_matrices(prow, pcol, b_rows(ssm_b_re), b_rows(ssm_b_im),
                                   c_cols(ssm_c_re), c_cols(ssm_c_im), d_vec)

    g3 = lambda g: g.reshape(DEPTH, 1, -1)
    g_pre1_, g_conv_out_, g_ssm_out_, g_post1_, g_pre2_, g_post2_, b_glu_ = map(
        g3, (g_pre1, g_conv_out, g_ssm_out, g_post1, g_pre2, g_post2, b_glu))

    x = jnp.concatenate([x_prompt.reshape(N_CTX, D_MODEL), x_sample.reshape(N_LAT, D_MODEL)], axis=0)
    fins = []
    for l in range(DEPTH):
        w_in_bf, w_glu_bf, w_out_bf, w_up_bf, w_down_bf = bf_weights
        cn, u = _in_proj(x, mod, g_pre1_, w_in_bf, conv_w, g_conv_out_, l)
        sre = state_ssm_re[:, l].reshape(DEC_BATCH, 2, -1)
        sim = state_ssm_im[:, l].reshape(DEC_BATCH, 2, -1)
        h0 = jnp.stack([sre[:, 0], sim[:, 0], sre[:, 1], sim[:, 1]], axis=0)
        y, fin = _s5_mixer(u, w1, w2, a_chunk, h0, l)
        fins.append(fin)
        x1, h2 = _out_proj(x, cn, y.reshape(N_TOK, SSM_WIDTH), mod, w_glu_bf, b_glu_, g_ssm_out_,
                           w_out_bf, g_post1_, g_pre2_, l)
        x, bf_weights = _conv_ffn(h2, x1, mod, w_up_bf, ffn_conv_w, w_down_bf, g_post2_, l,
                                  next_f32=f32_weights if l + 1 < DEPTH else ())

    fin = jnp.stack(fins, axis=0).reshape(DEPTH, 2, 2, BATCH, SSM_GROUPS, SSM_STATE)
    new_re = fin[:, :, 0].transpose(2, 0, 1, 3, 4)
    new_im = fin[:, :, 1].transpose(2, 0, 1, 3, 4)
    y_prompt = x[:N_CTX].reshape(BATCH, SEQ, D_MODEL)
    y_sample = x[N_CTX:].reshape(DEC_BATCH, DEC_SEQ, D_MODEL)
    return (y_prompt, y_sample, new_re, new_im)
```

```python
import functools

import jax
import jax.numpy as jnp
from jax import lax
from jax.experimental import pallas as pl
from jax.experimental.pallas import tpu as pltpu

D_MODEL = 2048
BATCH = 16
SEQ = 256
DEPTH = 4
DEC_BATCH = 2
DEC_SEQ = 1024
GRID_W = 64
CONV_WIDTH = 1024
HEAD_DIM = 64
SSM_WIDTH = 1024
SSM_GROUP = 16
SSM_GROUPS = 64
SSM_STATE = 64
D_FF = 5632
EPS = 1e-6

LANES = 128
N_CTX = BATCH * SEQ
N_LAT = DEC_BATCH * DEC_SEQ
N_TOK = N_CTX + N_LAT
CHUNK = 16
NC_CTX = SEQ // CHUNK
NC_LAT = DEC_SEQ // CHUNK
ROWS_CTX = BATCH * NC_CTX
ROWS_LAT = DEC_BATCH * NC_LAT
ROWS_S5 = ROWS_CTX + ROWS_LAT
N_PAIR = SSM_GROUPS // 2
PAIR_CH = 2 * SSM_GROUP
PAIR_IN = CHUNK * PAIR_CH
PAIR_ST = 2 * SSM_STATE
PAIRS_PER_STEP = LANES // PAIR_CH
STEPS_PER_LANE_BLOCK = LANES // PAIR_CH

TM1 = 1024
TM1_SPLIT = 512
TC1 = 256
TM3 = 512
SUB3 = 256
TM4 = 512
TF4 = 512
SUB4 = 256
LAT_STRIP = 32

VMEM_LIMIT = 56 * 1024 * 1024

f32 = jnp.float32
bf16 = jnp.bfloat16


def _dot(a, b):
    return jnp.dot(a, b, preferred_element_type=f32)


def _split_bf16(x):
    hi = x.astype(bf16)
    lo = (x - hi.astype(f32)).astype(bf16)
    return hi, lo


def _dot3(a, b):
    ah, al = _split_bf16(a)
    bh, bl = _split_bf16(b)
    return _dot(ah, bh) + (_dot(ah, bl) + _dot(al, bh))


def _dot_select(a, sel):
    a1 = a.astype(bf16)
    r1 = a - a1.astype(f32)
    a2 = r1.astype(bf16)
    a3 = (r1 - a2.astype(f32)).astype(bf16)
    return _dot(a1, sel) + (_dot(a2, sel) + _dot(a3, sel))


def _rms_rows(x, g):
    ms = jnp.mean(x * x, axis=-1, keepdims=True)
    return x * lax.rsqrt(ms + EPS) * g


def _cmul(ar, ai, br, bi):
    return ar * br - ai * bi, ar * bi + ai * br


def _params(*sem):
    return pltpu.CompilerParams(dimension_semantics=sem, vmem_limit_bytes=VMEM_LIMIT)


def _mod_kernel(cv_ref, w_ref, b_ref, o_ref):
    cv = cv_ref[...]
    s = cv * jax.nn.sigmoid(cv)
    o_ref[...] = _dot(s.astype(bf16), w_ref[...].astype(bf16)) + b_ref[...]


def _modulation(cvec8, w_ada, b_ada):
    tn = 1024
    n_out = 6 * D_MODEL
    return pl.pallas_call(
        _mod_kernel,
        grid=(DEPTH, n_out // tn),
        in_specs=[
            pl.BlockSpec((8, D_MODEL), lambda l, n: (0, 0)),
            pl.BlockSpec((None, D_MODEL, tn), lambda l, n: (l, 0, n)),
            pl.BlockSpec((None, 1, tn), lambda l, n: (l, 0, n)),
        ],
        out_specs=pl.BlockSpec((None, 8, tn), lambda l, n: (l, 0, n)),
        out_shape=jax.ShapeDtypeStruct((DEPTH, 8, n_out), f32),
        compiler_params=_params("arbitrary", "arbitrary"),
        name="adaln_modulation",
    )(cvec8, w_ada, b_ada.reshape(DEPTH, 1, n_out))


def _disc_kernel(ar_ref, ai_ref, ldt_ref, abr_ref, abi_ref, bfr_ref, bfi_ref):
    ar = ar_ref[...]
    ai = ai_ref[...]
    dt = jnp.exp(ldt_ref[...])
    mag = jnp.exp(ar * dt)
    abr = mag * jnp.cos(ai * dt)
    abi = mag * jnp.sin(ai * dt)
    nr = abr - 1.0
    den = ar * ar + ai * ai
    abr_ref[...] = abr
    abi_ref[...] = abi
    bfr_ref[...] = (nr * ar + abi * ai) / den
    bfi_ref[...] = (abi * ar - nr * ai) / den


def _discretise(a_re, a_im, log_dt):
    shape = (DEPTH * 2, SSM_GROUPS * SSM_STATE)
    spec = pl.BlockSpec(shape, lambda: (0, 0))
    out = jax.ShapeDtypeStruct(shape, f32)
    return pl.pallas_call(
        _disc_kernel,
        in_specs=[spec, spec, spec],
        out_specs=[spec, spec, spec, spec],
        out_shape=[out, out, out, out],
        name="s5_discretise",
    )(a_re.reshape(shape), a_im.reshape(shape),
      jnp.broadcast_to(log_dt[..., None], (DEPTH, 2, SSM_GROUPS, SSM_STATE)).reshape(shape))


def _pow_table(br, bi, kk, shape, bits=4):
    tr = ti = None
    pr, pi = br, bi
    for bit in range(bits):
        sel = ((kk >> bit) & 1) == 1
        fr = jnp.broadcast_to(jnp.where(sel, pr, 1.0), shape)
        fi = jnp.broadcast_to(jnp.where(sel, pi, 0.0), shape)
        if tr is None:
            tr, ti = fr, fi
        else:
            tr, ti = _cmul(tr, ti, fr, fi)
        pr, pi = _cmul(pr, pi, pr, pi)
    return tr, ti


def _prep_kernel(pr_ref, btr_ref, bti_ref, ctr_ref, cti_ref, dv_ref, *rest):
    n_cast = (len(rest) - 3) // 2
    w1_ref, w2_ref, at_ref = rest[n_cast:n_cast + 3]
    for src, dst in zip(rest[:n_cast], rest[n_cast + 3:]):
        dst[...] = src[...].astype(bf16)

    lane_in = lax.broadcasted_iota(jnp.int32, (1, PAIR_IN), 1)
    step_of_lane = lane_in >> 5
    grp_of_lane = (lane_in >> 4) & 1
    chan_of_lane = lane_in & (SSM_GROUP - 1)
    lane_st = lax.broadcasted_iota(jnp.int32, (1, PAIR_ST), 1)
    grp_of_st_lane = lane_st >> 6
    row_st = lax.broadcasted_iota(jnp.int32, (PAIR_ST, 1), 0)
    grp_of_st_row = row_st >> 6
    on_diag_st = row_st == lane_st
    chan_row =lax.broadcasted_iota(jnp.int32, (SSM_GROUP, 1), 0)

    lane_blk = lax.broadcasted_iota(jnp.int32, (1, LANES), 1)
    step_in_blk = lane_blk >> 5
    spread_ch = jnp.where((lane_blk & (SSM_GROUP - 1)) == chan_row, 1.0, 0.0).astype(bf16)
    same_grp = grp_of_st_row == ((lane_blk >> 4) & 1)
    n_blk = PAIR_IN // LANES

    taps = []
    for d in range(2):
        abr = pr_ref[4 * d + 0:4 * d + 1, :]
        abi = pr_ref[4 * d + 1:4 * d + 2, :]
        bfr = pr_ref[4 * d + 2:4 * d + 3, :]
        bfi = pr_ref[4 * d + 3:4 * d + 4, :]
        a2 = _cmul(abr, abi, abr, abi)
        a4 = _cmul(*a2, *a2)
        a8 = _cmul(*a4, *a4)
        a16 = _cmul(*a8, *a8)
        at_ref[2 * d:2 * d + 1, :] = a16[0]
        at_ref[2 * d + 1:2 * d + 2, :] = a16[1]
        bbr, bbi = _cmul(bfr, bfi, btr_ref[d], bti_ref[d])
        kk_rows = (CHUNK - 1 - chan_row) if d == 0 else chan_row
        pcr, pci = _pow_table(abr, abi, kk_rows, (CHUNK, PAIR_ST))
        pbr = jnp.concatenate([jnp.broadcast_to(pcr[j:j + 1, :], (PAIR_CH, PAIR_ST))
                               for j in range(CHUNK)], axis=0)
        pbi = jnp.concatenate([jnp.broadcast_to(pci[j:j + 1, :], (PAIR_CH, PAIR_ST))
                               for j in range(CHUNK)], axis=0)
        slab_r = jnp.concatenate([jnp.where(grp_of_st_lane == e, bbr, 0.0) for e in range(2)], axis=0)
        slab_i = jnp.concatenate([jnp.where(grp_of_st_lane == e, bbi, 0.0) for e in range(2)], axis=0)
        tile_r = jnp.concatenate([slab_r] * CHUNK, axis=0)
        tile_i = jnp.concatenate([slab_i] * CHUNK, axis=0)
        wbr, wbi = _cmul(pbr, pbi, tile_r, tile_i)
        for part, wb in ((0, wbr), (1, wbi)):
            c0 = PAIR_IN + (2 * d + part) * PAIR_ST
            w1_ref[:, c0:c0 + PAIR_ST] = wb.astype(bf16)

        acr = jnp.sum(jnp.where(on_diag_st, abr, 0.0), axis=1, keepdims=True)
        aci = jnp.sum(jnp.where(on_diag_st, abi, 0.0), axis=1, keepdims=True)
        ctr = jnp.where(same_grp, _dot_select(ctr_ref[d], spread_ch), 0.0)
        cti = jnp.where(same_grp, _dot_select(cti_ref[d], spread_ch), 0.0)
        kk_blk = (step_in_blk + 1) if d == 0 else (STEPS_PER_LANE_BLOCK - step_in_blk)
        m0r, m0i = _cmul(*_pow_table(acr, aci, kk_blk, (PAIR_ST, LANES), bits=3), ctr, cti)
        c2 = _cmul(acr, aci, acr, aci)
        c4 = _cmul(*c2, *c2)
        c8 = _cmul(*c4, *c4)
        c12 = _cmul(*c8, *c4)
        blocks = [(m0r, m0i)] + [_cmul(*cm, m0r, m0i) for cm in (c4, c8, c12)]
        if d == 1:
            blocks = blocks[::-1]
        ca1r = jnp.concatenate([b[0] for b in blocks], axis=1)
        ca1i = jnp.concatenate([b[1] for b in blocks], axis=1)
        ctr4 = jnp.concatenate([ctr] * n_blk, axis=1)
        cti4 = jnp.concatenate([cti] * n_blk, axis=1)
        if d == 0:
            ca0r = jnp.where(lane_in < PAIR_CH, ctr4, pltpu.roll(ca1r, PAIR_CH, axis=1))
            ca0i = jnp.where(lane_in < PAIR_CH, cti4, pltpu.roll(ca1i, PAIR_CH, axis=1))
        else:
            ca0r = jnp.where(lane_in >= PAIR_IN - PAIR_CH, ctr4, pltpu.roll(ca1r, PAIR_IN - PAIR_CH, axis=1))
            ca0i = jnp.where(lane_in >= PAIR_IN - PAIR_CH, cti4, pltpu.roll(ca1i, PAIR_IN - PAIR_CH, axis=1))
        w2_ref[(2 * d) * PAIR_ST:(2 * d + 1) * PAIR_ST, :] = ca1r.astype(bf16)
        w2_ref[(2 * d + 1) * PAIR_ST:(2 * d + 2) * PAIR_ST, :] = (-ca1i).astype(bf16)
        taps.append(_dot3(bbr, ca0r) - _dot3(bbi, ca0i))

    gf, gb = taps
    dvec = dv_ref[...]
    for jp in range(CHUNK):
        lo = PAIR_CH * jp
        hi = PAIR_CH * (jp + 1)
        rf = gf if jp == 0 else pltpu.roll(gf, lo, axis=1)
        rb = gb if jp == CHUNK - 1 else pltpu.roll(gb, hi, axis=1)
        blk = jnp.where(lane_in >= lo, rf, 0.0) + jnp.where(lane_in < hi, rb, 0.0)
        on_diag = (step_of_lane == jp) & (chan_of_lane == chan_row)
        blk = blk + jnp.where(on_diag, dvec, 0.0)
        for e in range(2):
            r0 = lo + e * SSM_GROUP
            w1_ref[r0:r0 + SSM_GROUP, 0:PAIR_IN] = jnp.where(grp_of_lane == e, blk, 0.0).astype(bf16)


def _s5_matrices(prow, bt_re, bt_im, ct_re, ct_im, d_vec, first_f32=()):
    n_steps = DEPTH * N_PAIR
    cast_in, cast_out, cast_shapes = [], [], []
    for w in first_f32:
        _, rows, cols = w.shape
        n_blk = next(n for n in (n_steps, n_steps // 2, n_steps // 4)
                     if rows % n == 0 and (rows // n) % 16 == 0)
        idx = lambda l, q, n_blk=n_blk: (jnp.minimum(l * N_PAIR + q, n_blk - 1), 0)
        cast_in.append(pl.BlockSpec((None, rows // n_blk, cols), lambda l, q, idx=idx: (0,) + idx(l, q)))
        cast_out.append(pl.BlockSpec((rows // n_blk, cols), idx))
        cast_shapes.append(jax.ShapeDtypeStruct((rows, cols), bf16))
    res = pl.pallas_call(
        _prep_kernel,
        grid=(DEPTH, N_PAIR),
        in_specs=[
            pl.BlockSpec((None, None, 8, PAIR_ST), lambda l, q: (l, q, 0, 0)),
            pl.BlockSpec((None, 2, None, SSM_GROUP, PAIR_ST), lambda l, q: (l, 0, q, 0, 0)),
            pl.BlockSpec((None, 2, None, SSM_GROUP, PAIR_ST), lambda l, q: (l, 0, q, 0, 0)),
            pl.BlockSpec((None, 2, None, PAIR_ST, SSM_GROUP), lambda l, q: (l, 0, q, 0, 0)),
            pl.BlockSpec((None, 2, None, PAIR_ST, SSM_GROUP), lambda l, q: (l, 0, q, 0, 0)),
            pl.BlockSpec((None, None, 1, PAIR_IN), lambda l, q: (l, q, 0, 0)),
        ] + cast_in,
        out_specs=[
            pl.BlockSpec((None, None, PAIR_IN, PAIR_IN + 4 * PAIR_ST), lambda l, q: (l, q, 0, 0)),
            pl.BlockSpec((None, None, 4 * PAIR_ST, PAIR_IN), lambda l, q: (l, q, 0, 0)),
            pl.BlockSpec((None, 4, PAIR_ST), lambda l, q: (l, 0, q)),
        ] + cast_out,
        out_shape=[
            jax.ShapeDtypeStruct((DEPTH, N_PAIR, PAIR_IN, PAIR_IN + 4 * PAIR_ST), bf16),
            jax.ShapeDtypeStruct((DEPTH, N_PAIR, 4 * PAIR_ST, PAIR_IN), bf16),
            jax.ShapeDtypeStruct((DEPTH, 4, SSM_GROUPS * SSM_STATE), f32),
        ] + cast_shapes,
        compiler_params=_params("arbitrary", "arbitrary"),
        name="s5_chunk_matrices",
    )(prow, bt_re, bt_im, ct_re, ct_im, d_vec, *first_f32)
    return res[0], res[1], res[2], tuple(res[3:])


def _chunk_scan(pr, pi, pows, n_rows, seq, reverse):
    pos = lax.broadcasted_iota(jnp.int32, (n_rows, 1), 0) & (seq - 1)
    shift, k = 1, 0
    while shift < seq:
        ar, ai = pows[k]
        if reverse:
            valid = pos < seq - shift
            sr = pltpu.roll(pr, n_rows - shift, axis=0)
            si = pltpu.roll(pi, n_rows - shift, axis=0)
        else:
            valid = pos >= shift
            sr = pltpu.roll(pr, shift, axis=0)
            si = pltpu.roll(pi, shift, axis=0)
        tr, ti = _cmul(ar, ai, sr, si)
        pr = pr + jnp.where(valid, tr, 0.0)
        pi = pi + jnp.where(valid, ti, 0.0)
        shift, k = 2 * shift, k + 1
    return pr, pi


def _s5_kernel(*refs):
    a_refs = refs[0:CHUNK]
    w1_ref, w2_ref, at_ref, h0_ref = refs[CHUNK:CHUNK + 4]
    y_ref, fin_ref = refs[CHUNK + 4:CHUNK + 6]
    r_scr, sp_scr, y_scr, sr_scr, si_scr = refs[CHUNK + 6:]

    lane_blk = lax.broadcasted_iota(jnp.int32, (1, LANES), 1) // PAIR_CH
    row_lat = lax.broadcasted_iota(jnp.int32, (ROWS_LAT, 1), 0)
    pos_ctx = lax.broadcasted_iota(jnp.int32, (ROWS_CTX, 1), 0) & (NC_CTX - 1)
    pos_lat = row_lat & (NC_LAT - 1)

    for p in range(PAIRS_PER_STEP):
        pieces = []
        for k in range(CHUNK // STEPS_PER_LANE_BLOCK):
            piece = None
            for jj in range(STEPS_PER_LANE_BLOCK):
                src = a_refs[STEPS_PER_LANE_BLOCK * k + jj][...]
                sh = (PAIR_CH * (jj - p)) % LANES
                rolled = src if sh == 0 else pltpu.roll(src, sh, axis=1)
                piece = rolled if piece is None else jnp.where(lane_blk == jj, rolled, piece)
            pieces.append(piece.astype(bf16))
        u_pair = jnp.concatenate(pieces, axis=1)
        r_scr[...] = _dot(u_pair, w1_ref[p])

        lanes_p = slice(p * PAIR_ST, (p + 1) * PAIR_ST)
        for d in range(2):
            pows = [(at_ref[2 * d:2 * d + 1, lanes_p], at_ref[2 * d + 1:2 * d + 2, lanes_p])]
            for _ in range(5):
                pows.append(_cmul(*pows[-1], *pows[-1]))
            c_re = PAIR_IN + 2 * d * PAIR_ST
            c_im = c_re + PAIR_ST
            reverse = d == 1

            sr, si = _chunk_scan(r_scr[0:ROWS_CTX, c_re:c_re + PAIR_ST],
                                 r_scr[0:ROWS_CTX, c_im:c_im + PAIR_ST],
                                 pows, ROWS_CTX, NC_CTX, reverse)
            sr_scr[...] = sr
            si_scr[...] = si
            last = 0 if reverse else NC_CTX - 1
            fin_ref[2 * d, :, lanes_p] = sr_scr[pl.ds(last, BATCH, stride=NC_CTX), :]
            fin_ref[2 * d + 1, :, lanes_p] = si_scr[pl.ds(last, BATCH, stride=NC_CTX), :]
            edge = pos_ctx == (NC_CTX - 1 if reverse else 0)
            back = ROWS_CTX - 1 if reverse else 1
            sp_scr[0:ROWS_CTX, 2 * d * PAIR_ST:(2 * d + 1) * PAIR_ST] = jnp.where(
                edge, 0.0, pltpu.roll(sr, back, axis=0))
            sp_scr[0:ROWS_CTX, (2 * d + 1) * PAIR_ST:(2 * d + 2) * PAIR_ST] = jnp.where(
                edge, 0.0, pltpu.roll(si, back, axis=0))

            h0r = jnp.where(row_lat < NC_LAT, h0_ref[2 * d, 0:1, lanes_p], h0_ref[2 * d, 1:2, lanes_p])
            h0i = jnp.where(row_lat < NC_LAT, h0_ref[2 * d + 1, 0:1, lanes_p],
                            h0_ref[2 * d + 1, 1:2, lanes_p])
            edge = pos_lat == (NC_LAT - 1 if reverse else 0)
            ahr, ahi = _cmul(*pows[0], h0r, h0i)
            pr = r_scr[ROWS_CTX:ROWS_S5, c_re:c_re + PAIR_ST] + jnp.where(edge, ahr, 0.0)
            pi = r_scr[ROWS_CTX:ROWS_S5, c_im:c_im + PAIR_ST] + jnp.where(edge, ahi, 0.0)
            sr, si = _chunk_scan(pr, pi, pows, ROWS_LAT, NC_LAT, reverse)
            back = ROWS_LAT - 1 if reverse else 1
            sp_scr[ROWS_CTX:ROWS_S5, 2 * d * PAIR_ST:(2 * d + 1) * PAIR_ST] = jnp.where(
                edge, h0r, pltpu.roll(sr, back, axis=0))
            sp_scr[ROWS_CTX:ROWS_S5, (2 * d + 1) * PAIR_ST:(2 * d + 2) * PAIR_ST] = jnp.where(
                edge, h0i, pltpu.roll(si, back, axis=0))

        y_scr[:, p * PAIR_IN:(p + 1) * PAIR_IN] = (
            r_scr[:, 0:PAIR_IN] + _dot(sp_scr[...].astype(bf16), w2_ref[p]))

    for j in range(CHUNK):
        k, jj = divmod(j, STEPS_PER_LANE_BLOCK)
        out = None
        for p in range(PAIRS_PER_STEP):
            src = y_scr[:, p * PAIR_IN + k * LANES:p * PAIR_IN + (k + 1) * LANES]
            sh = (PAIR_CH * (p - jj)) % LANES
            rolled = src if sh == 0 else pltpu.roll(src, sh, axis=1)
            out = rolled if out is None else jnp.where(lane_blk == p, rolled, out)
        y_ref[:, j, :] = out


def _s5_mixer(u_rows, w1, w2, a_chunk, h0, layer):
    n_state = SSM_GROUPS * SSM_STATE
    n_blk = SSM_WIDTH // LANES
    step_spec = lambda j: pl.BlockSpec((None, ROWS_S5, LANES), lambda g, j=j: (j, 0, g))
    w_pairs = lambda rows, cols: pl.BlockSpec((None, PAIRS_PER_STEP, rows, cols),
                                              lambda g: (layer, g, 0, 0))
    st_lanes = PAIRS_PER_STEP * PAIR_ST
    return pl.pallas_call(
        _s5_kernel,
        grid=(n_blk,),
        in_specs=[step_spec(j) for j in range(CHUNK)] + [
            w_pairs(PAIR_IN, PAIR_IN + 4 * PAIR_ST),
            w_pairs(4 * PAIR_ST, PAIR_IN),
            pl.BlockSpec((None, 4, st_lanes), lambda g: (layer, 0, g)),
            pl.BlockSpec((4, DEC_BATCH, st_lanes), lambda g: (0, 0, g)),
        ],
        out_specs=[
            pl.BlockSpec((ROWS_S5, CHUNK, LANES), lambda g: (0, 0, g)),
            pl.BlockSpec((4, BATCH, st_lanes), lambda g: (0, 0, g)),
        ],
        out_shape=[
            jax.ShapeDtypeStruct((ROWS_S5, CHUNK, SSM_WIDTH), f32),
            jax.ShapeDtypeStruct((4, BATCH, n_state), f32),
        ],
        scratch_shapes=[
            pltpu.VMEM((ROWS_S5, PAIR_IN + 4 * PAIR_ST), f32),
            pltpu.VMEM((ROWS_S5, 4 * PAIR_ST), f32),
            pltpu.VMEM((ROWS_S5, PAIRS_PER_STEP * PAIR_IN), f32),
            pltpu.VMEM((ROWS_CTX, PAIR_ST), f32),
            pltpu.VMEM((ROWS_CTX, PAIR_ST), f32),
        ],
        compiler_params=_params("arbitrary"),
        name="s5_chunked_mixer",
    )(*([u_rows] * CHUNK), w1, w2, a_chunk, h0)


def _shift_rows(v, period, n_rows, shift):
    row = lax.broadcasted_iota(jnp.int32, (n_rows, 1), 0)
    pos = row & (period - 1)
    prev = jnp.where(pos < shift, 0.0, pltpu.roll(v, shift, axis=0))
    nxt = jnp.where(pos >= period - shift, 0.0, pltpu.roll(v, n_rows - shift, axis=0))
    return prev, nxt


def _dwconv3_rows(v, w_ref, period, n_rows, shift=1):
    prev, nxt = _shift_rows(v, period, n_rows, shift)
    return prev * w_ref[0:1, :] + v * w_ref[1:2, :] + nxt * w_ref[2:3, :]


def _k1_kernel(*refs, split_input):
    if split_input:
        xp_ref, xs_ref = refs[:2]
        refs = refs[2:]
    else:
        x_ref = refs[0]
        refs = refs[1:]
    mod_ref, gpre_ref, wgb_ref, wgc_ref, whv_ref, wu_ref, cw_ref, gco_ref, cn_ref, u_ref = refs[:10]
    if split_input:
        x_ref, h_scr, u_scr = refs[10:]
    else:
        h_scr, u_scr = refs[10:]
    i = pl.program_id(0)
    n = pl.program_id(1)
    tm = cn_ref.shape[0]
    n_ctx_tiles = N_CTX // tm

    @pl.when(n == 0)
    def _():
        if split_input:
            @pl.when(i < n_ctx_tiles)
            def _():
                x_ref[...] = xp_ref[...]

            @pl.when(i >= n_ctx_tiles)
            def _():
                x_ref[...] = xs_ref[...]

        r = jnp.where(i < n_ctx_tiles, 0, 1 + (i - n_ctx_tiles) // (DEC_SEQ // tm))
        sh1 = mod_ref[pl.ds(r, 1), 0:D_MODEL]
        sc1 = mod_ref[pl.ds(r, 1), D_MODEL:2 * D_MODEL]
        y = _rms_rows(x_ref[...], gpre_ref[...])
        h_scr[...] = (y * (1.0 + sc1) + sh1).astype(bf16)

    h = h_scr[...]
    u = _dot(h, wu_ref[...])
    for t in range(TC1 // LANES):
        u_scr[t] = u[:, t * LANES:(t + 1) * LANES]
    for j in range(CHUNK):
        for t in range(TC1 // LANES):
            u_ref[j, :, t * LANES:(t + 1) * LANES] = u_scr[t, pl.ds(j, tm // CHUNK, stride=CHUNK), :]
    v = _dot(h, wgc_ref[...]) * _dot(h, whv_ref[...])
    period = jnp.where(i < n_ctx_tiles, SEQ, GRID_W)
    v = _dwconv3_rows(v, cw_ref, period, tm)
    co = _dot(h, wgb_ref[...]) * v
    rr = lax.broadcasted_iota(jnp.int32, (TC1, TC1), 0) // HEAD_DIM
    cc = lax.broadcasted_iota(jnp.int32, (TC1, TC1), 1) // HEAD_DIM
    avg = jnp.where(rr == cc, 1.0 / HEAD_DIM, 0.0).astype(bf16)
    hi, lo = _split_bf16(co * co)
    ms = _dot(hi, avg) + _dot(lo, avg)
    cn_ref[...] = (co * lax.rsqrt(ms + EPS) * gco_ref[...]).astype(bf16)


def _in_proj(xs, mod, g_pre1, w_in_bf, conv_w, g_conv_out, layer):
    nb = CONV_WIDTH // TC1
    split_input = len(xs) == 2
    tm = TM1_SPLIT if split_input else TM1
    n_ctx_tiles = N_CTX // tm
    wspec = lambda off: pl.BlockSpec((D_MODEL, TC1), lambda i, n: (0, off * nb + n))
    x_spec = pl.BlockSpec((tm, D_MODEL), lambda i, n: (i, 0))
    if split_input:
        x_specs = [pl.BlockSpec((tm, D_MODEL), lambda i, n: (jnp.minimum(i, n_ctx_tiles - 1), 0)),
                   pl.BlockSpec((tm, D_MODEL), lambda i, n: (jnp.maximum(i - n_ctx_tiles, 0), 0))]
    else:
        x_specs = [x_spec]
    return pl.pallas_call(
        functools.partial(_k1_kernel, split_input=split_input),
        grid=(N_TOK // tm, nb),
        in_specs=x_specs + [
            pl.BlockSpec((None, 8, 6 * D_MODEL), lambda i, n: (layer, 0, 0)),
            pl.BlockSpec((None, 1, D_MODEL), lambda i, n: (layer, 0, 0)),
            wspec(0), wspec(1), wspec(2), wspec(3),
            pl.BlockSpec((None, 3, TC1), lambda i, n: (layer, 0, n)),
            pl.BlockSpec((None, 1, TC1), lambda i, n: (layer, 0, n)),
        ],
        out_specs=[
            pl.BlockSpec((tm, TC1), lambda i, n: (i, n)),
            pl.BlockSpec((CHUNK, tm // CHUNK, TC1), lambda i, n: (0, i, n)),
        ] + ([x_spec] if split_input else []),
        out_shape=[
            jax.ShapeDtypeStruct((N_TOK, CONV_WIDTH), bf16),
            jax.ShapeDtypeStruct((CHUNK, ROWS_S5, SSM_WIDTH), f32),
        ] + ([jax.ShapeDtypeStruct((N_TOK, D_MODEL), f32)] if split_input else []),
        scratch_shapes=[pltpu.VMEM((tm, D_MODEL), bf16),
                        pltpu.VMEM((TC1 // LANES, tm, LANES), f32)],
        compiler_params=_params("arbitrary", "arbitrary"),
        name="in_proj_conv_mixer",
    )(*xs, mod, g_pre1, w_in_bf, w_in_bf, w_in_bf, w_in_bf, conv_w, g_conv_out)


def _k3_kernel(x_ref, cn_ref, y_ref, mod_ref, wglu_ref, bglu_ref, gsso_ref, wout_ref,
               gpost1_ref, gpre2_ref, x1_ref, h2_ref):
    i = pl.program_id(0)
    n_ctx_tiles = N_CTX // TM3
    r = jnp.where(i < n_ctx_tiles, 0, 1 + (i - n_ctx_tiles) // (DEC_SEQ // TM3))
    gt1 = mod_ref[pl.ds(r, 1), 2 * D_MODEL:3 * D_MODEL]
    sh2 = mod_ref[pl.ds(r, 1), 3 * D_MODEL:4 * D_MODEL]
    sc2 = mod_ref[pl.ds(r, 1), 4 * D_MODEL:5 * D_MODEL]

    for s in range(TM3 // SUB3):
        rows = slice(s * SUB3, (s + 1) * SUB3)
        z = jax.nn.gelu(y_ref[rows, :], approximate=True)
        gate = jax.nn.sigmoid(_dot(z.astype(bf16), wglu_ref[...]) + bglu_ref[...])
        sn = _rms_rows(z * gate, gsso_ref[...])
        mixed = (_dot(cn_ref[rows, :], wout_ref[0:CONV_WIDTH, :])
                 + _dot(sn.astype(bf16), wout_ref[CONV_WIDTH:CONV_WIDTH + SSM_WIDTH, :]))
        x1 = x_ref[rows, :] + gt1 * _rms_rows(mixed, gpost1_ref[...])
        x1_ref[rows, :] = x1
        h2_ref[rows, :] = (_rms_rows(x1, gpre2_ref[...]) * (1.0 + sc2) + sh2).astype(bf16)


def _out_proj(x, cn, y, mod, w_glu_bf, b_glu, g_ssm_out, w_out_bf, g_post1, g_pre2, layer):
    vec = lambda width: pl.BlockSpec((None, 1, width), lambda i: (layer, 0, 0))
    return pl.pallas_call(
        _k3_kernel,
        grid=(N_TOK // TM3,),
        in_specs=[
            pl.BlockSpec((TM3, D_MODEL), lambda i: (i, 0)),
            pl.BlockSpec((TM3, CONV_WIDTH), lambda i: (i, 0)),
            pl.BlockSpec((TM3, SSM_WIDTH), lambda i: (i, 0)),
            pl.BlockSpec((None, 8, 6 * D_MODEL), lambda i: (layer, 0, 0)),
            pl.BlockSpec((SSM_WIDTH, SSM_WIDTH), lambda i: (0, 0)),
            vec(SSM_WIDTH), vec(SSM_WIDTH),
            pl.BlockSpec((CONV_WIDTH + SSM_WIDTH, D_MODEL), lambda i: (0, 0)),
            vec(D_MODEL), vec(D_MODEL),
        ],
        out_specs=[
            pl.BlockSpec((TM3, D_MODEL), lambda i: (i, 0)),
            pl.BlockSpec((TM3, D_MODEL), lambda i: (i, 0)),
        ],
        out_shape=[
            jax.ShapeDtypeStruct((N_TOK, D_MODEL), f32),
            jax.ShapeDtypeStruct((N_TOK, D_MODEL), bf16),
        ],
        compiler_params=_params("arbitrary"),
        name="glu_out_proj",
    )(x, cn, y, mod, w_glu_bf, b_glu, g_ssm_out, w_out_bf, g_post1, g_pre2)


def _k4_kernel(h_ref, x1_ref, mod_ref, wa_ref, wg_ref, cwa_ref, cwg_ref, wd_ref, gpost2_ref,
               *rest, latent, aliased):
    rest = rest[1:] if aliased else rest
    n_cast = (len(rest) - 1) // 2
    o_ref = rest[n_cast]
    for src, dst in zip(rest[:n_cast], rest[n_cast + 1:]):
        dst[...] = src[...].astype(bf16)

    i = pl.program_id(0)
    j = pl.program_id(1)
    blk = o_ref.shape

    @pl.when(j == 0)
    def _():
        o_ref[...] = jnp.zeros(blk, f32)

    period, shift = (TM4, LAT_STRIP) if latent else (SEQ, 1)
    h = h_ref[...].reshape(TM4, D_MODEL)
    acts = []
    for s in range(TF4 // SUB4):
        cols = slice(s * SUB4, (s + 1) * SUB4)
        a = _dwconv3_rows(_dot(h, wa_ref[:, cols]), cwa_ref[:, cols], period, TM4, shift)
        g = _dwconv3_rows(_dot(h, wg_ref[:, cols]), cwg_ref[:, cols], period, TM4, shift)
        acts.append((g * jax.nn.sigmoid(g) * a).astype(bf16))
    o_ref[...] += _dot(jnp.concatenate(acts, axis=1), wd_ref[...]).reshape(blk)

    @pl.when(j == pl.num_programs(1) - 1)
    def _():
        r = (1 + i // (GRID_W // LAT_STRIP)) if latent else 0
        gt2 = mod_ref[pl.ds(r, 1), 5 * D_MODEL:6 * D_MODEL]
        ff = o_ref[...].reshape(TM4, D_MODEL)
        x2 = x1_ref[...].reshape(TM4, D_MODEL) + gt2 * _rms_rows(ff, gpost2_ref[...])
        o_ref[...] = x2.reshape(blk)


def _conv_ffn(h2, x1, mod, w_up_bf, ffn_conv_w, w_down_bf, g_post2, layer, next_f32=(), last=False):
    nj = D_FF // TF4
    n_ctx_tiles = N_CTX // TM4
    cast_in, cast_out, cast_shapes = [], [], []
    for w in next_f32:
        _, rows, cols = w.shape
        if cols % nj == 0:
            blk = (rows // n_ctx_tiles, cols // nj)
            idx = lambda i, j: (i, j)
        elif rows % nj == 0:
            blk = (rows // nj, cols // n_ctx_tiles)
            idx = lambda i, j: (j, i)
        else:
            blk = (rows // n_ctx_tiles, cols // n_ctx_tiles)
            idx = lambda i, j: (i, jnp.minimum(j, n_ctx_tiles - 1))
        cast_in.append(pl.BlockSpec((None,) + blk, lambda i, j, idx=idx: (layer + 1,) + idx(i, j)))
        cast_out.append(pl.BlockSpec(blk, idx))
        cast_shapes.append(jax.ShapeDtypeStruct((rows, cols), bf16))

    weight_specs = [
        pl.BlockSpec((None, 8, 6 * D_MODEL), lambda i, j: (layer, 0, 0)),
        pl.BlockSpec((D_MODEL, TF4), lambda i, j: (0, j)),
        pl.BlockSpec((D_MODEL, TF4), lambda i, j: (0, nj + j)),
        pl.BlockSpec((None, 3, TF4), lambda i, j: (layer, 0, j)),
        pl.BlockSpec((None, 3, TF4), lambda i, j: (layer, 0, nj + j)),
        pl.BlockSpec((TF4, D_MODEL), lambda i, j: (j, 0)),
        pl.BlockSpec((None, 1, D_MODEL), lambda i, j: (layer, 0, 0)),
    ]
    weights = (mod, w_up_bf, w_up_bf, ffn_conv_w, ffn_conv_w, w_down_bf, g_post2)

    row_spec = pl.BlockSpec((TM4, D_MODEL), lambda i, j: (i, 0))
    res = pl.pallas_call(
        functools.partial(_k4_kernel, latent=False, aliased=False),
        grid=(n_ctx_tiles, nj),
        in_specs=[row_spec, row_spec] + weight_specs + cast_in,
        out_specs=[row_spec] + cast_out,
        out_shape=[jax.ShapeDtypeStruct((N_CTX if last else N_TOK, D_MODEL), f32)] + cast_shapes,
        compiler_params=_params("arbitrary", "arbitrary"),
        name="conv_ffn_ctx",
    )(h2, x1, *weights, *next_f32)

    view = lambda t: t.reshape(-1, GRID_W, D_MODEL)
    rows_per_img = DEC_SEQ // GRID_W
    strips = GRID_W // LAT_STRIP
    first_img = N_CTX // DEC_SEQ
    strip_spec = pl.BlockSpec((rows_per_img, LAT_STRIP, D_MODEL),
                              lambda i, j: (first_img + i // strips, i % strips, 0))
    if last:
        out_spec = pl.BlockSpec((rows_per_img, LAT_STRIP, D_MODEL), lambda i, j: (i // strips, i % strips, 0))
        out_rows, alias_specs, alias_args, aliases = N_LAT, [], (), {}
    else:
        out_spec = strip_spec
        out_rows, alias_specs, alias_args = N_TOK, [pl.BlockSpec(memory_space=pl.ANY)], (view(res[0]),)
        aliases = {2 + len(weights): 0}
    x2_lat = pl.pallas_call(
        functools.partial(_k4_kernel, latent=True, aliased=not last),
        grid=(DEC_BATCH * strips, nj),
        in_specs=[strip_spec, strip_spec] + weight_specs + alias_specs,
        out_specs=out_spec,
        out_shape=jax.ShapeDtypeStruct((out_rows // GRID_W, GRID_W, D_MODEL), f32),
        input_output_aliases=aliases,
        compiler_params=_params("arbitrary", "arbitrary"),
        name="conv_ffn_lat",
    )(view(h2), view(x1), *weights, *alias_args)
    if last:
        return (res[0], x2_lat.reshape(N_LAT, D_MODEL)), ()
    return x2_lat.reshape(N_TOK, D_MODEL), tuple(res[1:])


def kernel(x_prompt, x_sample, state_ssm_re, state_ssm_im, c, c_ctx, w_ada, b_ada, g_pre1, w_in, conv_w, ssm_a_re, ssm_a_im, ssm_log_dt, ssm_b_re, ssm_b_im, ssm_c_re, ssm_c_im, ssm_d, w_glu, b_glu, g_conv_out, g_ssm_out, w_out, g_post1, g_pre2, w_up, ffn_conv_w, w_down, g_post2):
    f32_weights = (w_in, w_glu, w_out, w_up, w_down)

    cvec8 = jnp.concatenate([c_ctx[None, :], c, jnp.zeros((8 - 1 - DEC_BATCH, D_MODEL), f32)], axis=0)
    mod = _modulation(cvec8, w_ada, b_ada)

    abr, abi, bfr, bfi = [t.reshape(DEPTH, 2, N_PAIR, PAIR_ST)
                          for t in _discretise(ssm_a_re, ssm_a_im, ssm_log_dt)]
    prow = jnp.stack([abr, abi, bfr, bfi], axis=2)
    prow = prow.transpose(0, 3, 1, 2, 4).reshape(DEPTH, N_PAIR, 8, PAIR_ST)

    def b_rows(b):
        b = b.transpose(0, 1, 2, 4, 3).reshape(DEPTH, 2, N_PAIR, 2, SSM_GROUP, SSM_STATE)
        return b.transpose(0, 1, 2, 4, 3, 5).reshape(DEPTH, 2, N_PAIR, SSM_GROUP, PAIR_ST)

    def c_cols(cm):
        return cm.transpose(0, 1, 2, 4, 3).reshape(DEPTH, 2, N_PAIR, PAIR_ST, SSM_GROUP)

    d_vec = jnp.broadcast_to(ssm_d.reshape(DEPTH, N_PAIR, 1, PAIR_CH), (DEPTH, N_PAIR, CHUNK, PAIR_CH))
    d_vec = d_vec.reshape(DEPTH, N_PAIR, 1, PAIR_IN)
    w1, w2, a_chunk, bf_weights = _s5_matrices(
        prow, b_rows(ssm_b_re), b_rows(ssm_b_im), c_cols(ssm_c_re), c_cols(ssm_c_im), d_vec,
        first_f32=f32_weights)

    g3 = lambda g: g.reshape(DEPTH, 1, -1)
    g_pre1_, g_conv_out_, g_ssm_out_, g_post1_, g_pre2_, g_post2_, b_glu_ = map(
        g3, (g_pre1, g_conv_out, g_ssm_out, g_post1, g_pre2, g_post2, b_glu))

    xs = (x_prompt.reshape(N_CTX, D_MODEL), x_sample.reshape(N_LAT, D_MODEL))
    fins = []
    for l in range(DEPTH):
        last = l + 1 == DEPTH
        w_in_bf, w_glu_bf, w_out_bf, w_up_bf, w_down_bf = bf_weights
        res = _in_proj(xs, mod, g_pre1_, w_in_bf, conv_w, g_conv_out_, l)
        cn, u = res[:2]
        x = res[2] if len(xs) == 2 else xs[0]
        sre = state_ssm_re[:, l].reshape(DEC_BATCH, 2, -1)
        sim = state_ssm_im[:, l].reshape(DEC_BATCH, 2, -1)
        h0 = jnp.stack([sre[:, 0], sim[:, 0], sre[:, 1], sim[:, 1]], axis=0)
        y, fin = _s5_mixer(u, w1, w2, a_chunk, h0, l)
        fins.append(fin)
        x1, h2 = _out_proj(x, cn, y.reshape(N_TOK, SSM_WIDTH), mod, w_glu_bf, b_glu_, g_ssm_out_,
                           w_out_bf, g_post1_, g_pre2_, l)
        x, bf_weights = _conv_ffn(h2, x1, mod, w_up_bf, ffn_conv_w, w_down_bf, g_post2_, l,
                                  next_f32=() if last else f32_weights, last=last)
        xs = (x,)

    fin = jnp.stack(fins, axis=0).reshape(DEPTH, 2, 2, BATCH, SSM_GROUPS, SSM_STATE)
    new_re = fin[:, :, 0].transpose(2, 0, 1, 3, 4)
    new_im = fin[:, :, 1].transpose(2, 0, 1, 3, 4)
    y_prompt, y_sample = x
    return (y_prompt.reshape(BATCH, SEQ, D_MODEL), y_sample.reshape(DEC_BATCH, DEC_SEQ, D_MODEL),
            new_re, new_im)
```

```python
import functools

import jax
import jax.numpy as jnp
from jax import lax
from jax.experimental import pallas as pl
from jax.experimental.pallas import tpu as pltpu

D_MODEL = 2048
BATCH = 16
SEQ = 256
DEPTH = 4
DEC_BATCH = 2
DEC_SEQ = 1024
GRID_W = 64
CONV_WIDTH = 1024
HEAD_DIM = 64
SSM_WIDTH = 1024
SSM_GROUP = 16
SSM_GROUPS = 64
SSM_STATE = 64
D_FF = 5632
EPS = 1e-6

LANES = 128
N_CTX = BATCH * SEQ
N_LAT = DEC_BATCH * DEC_SEQ
N_TOK = N_CTX + N_LAT
CHUNK = 16
NC_CTX = SEQ // CHUNK
NC_LAT = DEC_SEQ // CHUNK
ROWS_CTX = BATCH * NC_CTX
ROWS_LAT = DEC_BATCH * NC_LAT
ROWS_S5 = ROWS_CTX + ROWS_LAT
N_PAIR = SSM_GROUPS // 2
PAIR_CH = 2 * SSM_GROUP
PAIR_IN = CHUNK * PAIR_CH
PAIR_ST = 2 * SSM_STATE
PAIRS_PER_STEP = LANES // PAIR_CH
STEPS_PER_LANE_BLOCK = LANES // PAIR_CH

TM1 = 1024
TM1_SPLIT = 512
TC1 = 256
TM3 = 512
SUB3 = 256
TM4 = 512
TF4 = 512
SUB4 = 256
LAT_STRIP = 32

VMEM_LIMIT = 56 * 1024 * 1024

f32 = jnp.float32
bf16 = jnp.bfloat16


def _dot(a, b):
    return jnp.dot(a, b, preferred_element_type=f32)


def _split_bf16(x):
    hi = x.astype(bf16)
    lo = (x - hi.astype(f32)).astype(bf16)
    return hi, lo


def _dot3(a, b):
    ah, al = _split_bf16(a)
    bh, bl = _split_bf16(b)
    return _dot(ah, bh) + (_dot(ah, bl) + _dot(al, bh))


def _dot_select(a, sel):
    a1 = a.astype(bf16)
    r1 = a - a1.astype(f32)
    a2 = r1.astype(bf16)
    a3 = (r1 - a2.astype(f32)).astype(bf16)
    return _dot(a1, sel) + (_dot(a2, sel) + _dot(a3, sel))


def _rms_rows(x, g):
    ms = jnp.mean(x * x, axis=-1, keepdims=True)
    return x * lax.rsqrt(ms + EPS) * g


def _cmul(ar, ai, br, bi):
    return ar * br - ai * bi, ar * bi + ai * br


def _params(*sem):
    return pltpu.CompilerParams(dimension_semantics=sem, vmem_limit_bytes=VMEM_LIMIT)


def _mod_kernel(cv_ref, w_ref, b_ref, o_ref):
    cv = cv_ref[...]
    s = cv * jax.nn.sigmoid(cv)
    o_ref[...] = _dot(s.astype(bf16), w_ref[...].astype(bf16)) + b_ref[...]


def _modulation(cvec8, w_ada, b_ada):
    tn = 1024
    n_out = 6 * D_MODEL
    return pl.pallas_call(
        _mod_kernel,
        grid=(DEPTH, n_out // tn),
        in_specs=[
            pl.BlockSpec((8, D_MODEL), lambda l, n: (0, 0)),
            pl.BlockSpec((None, D_MODEL, tn), lambda l, n: (l, 0, n)),
            pl.BlockSpec((None, 1, tn), lambda l, n: (l, 0, n)),
        ],
        out_specs=pl.BlockSpec((None, 8, tn), lambda l, n: (l, 0, n)),
        out_shape=jax.ShapeDtypeStruct((DEPTH, 8, n_out), f32),
        compiler_params=_params("arbitrary", "arbitrary"),
        name="adaln_modulation",
    )(cvec8, w_ada, b_ada.reshape(DEPTH, 1, n_out))


def _disc_kernel(ar_ref, ai_ref, ldt_ref, abr_ref, abi_ref, bfr_ref, bfi_ref):
    ar = ar_ref[...]
    ai = ai_ref[...]
    dt = jnp.exp(ldt_ref[...])
    mag = jnp.exp(ar * dt)
    abr = mag * jnp.cos(ai * dt)
    abi = mag * jnp.sin(ai * dt)
    nr = abr - 1.0
    den = ar * ar + ai * ai
    abr_ref[...] = abr
    abi_ref[...] = abi
    bfr_ref[...] = (nr * ar + abi * ai) / den
    bfi_ref[...] = (abi * ar - nr * ai) / den


def _discretise(a_re, a_im, log_dt):
    shape = (DEPTH * 2, SSM_GROUPS * SSM_STATE)
    spec = pl.BlockSpec(shape, lambda: (0, 0))
    out = jax.ShapeDtypeStruct(shape, f32)
    return pl.pallas_call(
        _disc_kernel,
        in_specs=[spec, spec, spec],
        out_specs=[spec, spec, spec, spec],
        out_shape=[out, out, out, out],
        name="s5_discretise",
    )(a_re.reshape(shape), a_im.reshape(shape),
      jnp.broadcast_to(log_dt[..., None], (DEPTH, 2, SSM_GROUPS, SSM_STATE)).reshape(shape))


def _pow_table(br, bi, kk, shape, bits=4):
    tr = ti = None
    pr, pi = br, bi
    for bit in range(bits):
        sel = ((kk >> bit) & 1) == 1
        fr = jnp.broadcast_to(jnp.where(sel, pr, 1.0), shape)
        fi = jnp.broadcast_to(jnp.where(sel, pi, 0.0), shape)
        if tr is None:
            tr, ti = fr, fi
        else:
            tr, ti = _cmul(tr, ti, fr, fi)
        pr, pi = _cmul(pr, pi, pr, pi)
    return tr, ti


def _prep_kernel(pr_ref, btr_ref, bti_ref, ctr_ref, cti_ref, dv_ref, *rest):
    n_cast = (len(rest) - 3) // 2
    w1_ref, w2_ref, at_ref = rest[n_cast:n_cast + 3]
    for src, dst in zip(rest[:n_cast], rest[n_cast + 3:]):
        dst[...] = src[...].astype(bf16)

    lane_in = lax.broadcasted_iota(jnp.int32, (1, PAIR_IN), 1)
    step_of_lane = lane_in >> 5
    grp_of_lane = (lane_in >> 4) & 1
    chan_of_lane = lane_in & (SSM_GROUP - 1)
    lane_st = lax.broadcasted_iota(jnp.int32, (1, PAIR_ST), 1)
    grp_of_st_lane = lane_st >> 6
    row_st = lax.broadcasted_iota(jnp.int32, (PAIR_ST, 1), 0)
    grp_of_st_row = row_st >> 6
    on_diag_st = row_st == lane_st
    chan_row =lax.broadcasted_iota(jnp.int32, (SSM_GROUP, 1), 0)

    lane_blk = lax.broadcasted_iota(jnp.int32, (1, LANES), 1)
    step_in_blk = lane_blk >> 5
    spread_ch = jnp.where((lane_blk & (SSM_GROUP - 1)) == chan_row, 1.0, 0.0).astype(bf16)
    same_grp = grp_of_st_row == ((lane_blk >> 4) & 1)
    n_blk = PAIR_IN // LANES

    taps = []
    for d in range(2):
        abr = pr_ref[4 * d + 0:4 * d + 1, :]
        abi = pr_ref[4 * d + 1:4 * d + 2, :]
        bfr = pr_ref[4 * d + 2:4 * d + 3, :]
        bfi = pr_ref[4 * d + 3:4 * d + 4, :]
        a2 = _cmul(abr, abi, abr, abi)
        a4 = _cmul(*a2, *a2)
        a8 = _cmul(*a4, *a4)
        a16 = _cmul(*a8, *a8)
        at_ref[2 * d:2 * d + 1, :] = a16[0]
        at_ref[2 * d + 1:2 * d + 2, :] = a16[1]
        bbr, bbi = _cmul(bfr, bfi, btr_ref[d], bti_ref[d])
        kk_rows = (CHUNK - 1 - chan_row) if d == 0 else chan_row
        pcr, pci = _pow_table(abr, abi, kk_rows, (CHUNK, PAIR_ST))
        pbr = jnp.concatenate([jnp.broadcast_to(pcr[j:j + 1, :], (PAIR_CH, PAIR_ST))
                               for j in range(CHUNK)], axis=0)
        pbi = jnp.concatenate([jnp.broadcast_to(pci[j:j + 1, :], (PAIR_CH, PAIR_ST))
                               for j in range(CHUNK)], axis=0)
        slab_r = jnp.concatenate([jnp.where(grp_of_st_lane == e, bbr, 0.0) for e in range(2)], axis=0)
        slab_i = jnp.concatenate([jnp.where(grp_of_st_lane == e, bbi, 0.0) for e in range(2)], axis=0)
        tile_r = jnp.concatenate([slab_r] * CHUNK, axis=0)
        tile_i = jnp.concatenate([slab_i] * CHUNK, axis=0)
        wbr, wbi = _cmul(pbr, pbi, tile_r, tile_i)
        for part, wb in ((0, wbr), (1, wbi)):
            c0 = PAIR_IN + (2 * d + part) * PAIR_ST
            w1_ref[:, c0:c0 + PAIR_ST] = wb.astype(bf16)

        acr = jnp.sum(jnp.where(on_diag_st, abr, 0.0), axis=1, keepdims=True)
        aci = jnp.sum(jnp.where(on_diag_st, abi, 0.0), axis=1, keepdims=True)
        ctr = jnp.where(same_grp, _dot_select(ctr_ref[d], spread_ch), 0.0)
        cti = jnp.where(same_grp, _dot_select(cti_ref[d], spread_ch), 0.0)
        kk_blk = (step_in_blk + 1) if d == 0 else (STEPS_PER_LANE_BLOCK - step_in_blk)
        m0r, m0i = _cmul(*_pow_table(acr, aci, kk_blk, (PAIR_ST, LANES), bits=3), ctr, cti)
        c2 = _cmul(acr, aci, acr, aci)
        c4 = _cmul(*c2, *c2)
        c8 = _cmul(*c4, *c4)
        c12 = _cmul(*c8, *c4)
        blocks = [(m0r, m0i)] + [_cmul(*cm, m0r, m0i) for cm in (c4, c8, c12)]
        if d == 1:
            blocks = blocks[::-1]
        ca1r = jnp.concatenate([b[0] for b in blocks], axis=1)
        ca1i = jnp.concatenate([b[1] for b in blocks], axis=1)
        ctr4 = jnp.concatenate([ctr] * n_blk, axis=1)
        cti4 = jnp.concatenate([cti] * n_blk, axis=1)
        if d == 0:
            ca0r = jnp.where(lane_in < PAIR_CH, ctr4, pltpu.roll(ca1r, PAIR_CH, axis=1))
            ca0i = jnp.where(lane_in < PAIR_CH, cti4, pltpu.roll(ca1i, PAIR_CH, axis=1))
        else:
            ca0r = jnp.where(lane_in >= PAIR_IN - PAIR_CH, ctr4, pltpu.roll(ca1r, PAIR_IN - PAIR_CH, axis=1))
            ca0i = jnp.where(lane_in >= PAIR_IN - PAIR_CH, cti4, pltpu.roll(ca1i, PAIR_IN - PAIR_CH, axis=1))
        w2_ref[(2 * d) * PAIR_ST:(2 * d + 1) * PAIR_ST, :] = ca1r.astype(bf16)
        w2_ref[(2 * d + 1) * PAIR_ST:(2 * d + 2) * PAIR_ST, :] = (-ca1i).astype(bf16)
        taps.append(_dot3(bbr, ca0r) - _dot3(bbi, ca0i))

    gf, gb = taps
    dvec = dv_ref[...]
    for jp in range(CHUNK):
        lo = PAIR_CH * jp
        hi = PAIR_CH * (jp + 1)
        rf = gf if jp == 0 else pltpu.roll(gf, lo, axis=1)
        rb = gb if jp == CHUNK - 1 else pltpu.roll(gb, hi, axis=1)
        blk = jnp.where(lane_in >= lo, rf, 0.0) + jnp.where(lane_in < hi, rb, 0.0)
        on_diag = (step_of_lane == jp) & (chan_of_lane == chan_row)
        blk = blk + jnp.where(on_diag, dvec, 0.0)
        for e in range(2):
            r0 = lo + e * SSM_GROUP
            w1_ref[r0:r0 + SSM_GROUP, 0:PAIR_IN] = jnp.where(grp_of_lane == e, blk, 0.0).astype(bf16)


def _s5_matrices(prow, bt_re, bt_im, ct_re, ct_im, d_vec, first_f32=()):
    n_steps = DEPTH * N_PAIR
    cast_in, cast_out, cast_shapes = [], [], []
    for w in first_f32:
        _, rows, cols = w.shape
        n_blk = next(n for n in (n_steps, n_steps // 2, n_steps // 4)
                     if rows % n == 0 and (rows // n) % 16 == 0)
        idx = lambda l, q, n_blk=n_blk: (jnp.minimum(l * N_PAIR + q, n_blk - 1), 0)
        cast_in.append(pl.BlockSpec((None, rows // n_blk, cols), lambda l, q, idx=idx: (0,) + idx(l, q)))
        cast_out.append(pl.BlockSpec((rows // n_blk, cols), idx))
        cast_shapes.append(jax.ShapeDtypeStruct((rows, cols), bf16))
    res = pl.pallas_call(
        _prep_kernel,
        grid=(DEPTH, N_PAIR),
        in_specs=[
            pl.BlockSpec((None, None, 8, PAIR_ST), lambda l, q: (l, q, 0, 0)),
            pl.BlockSpec((None, 2, None, SSM_GROUP, PAIR_ST), lambda l, q: (l, 0, q, 0, 0)),
            pl.BlockSpec((None, 2, None, SSM_GROUP, PAIR_ST), lambda l, q: (l, 0, q, 0, 0)),
            pl.BlockSpec((None, 2, None, PAIR_ST, SSM_GROUP), lambda l, q: (l, 0, q, 0, 0)),
            pl.BlockSpec((None, 2, None, PAIR_ST, SSM_GROUP), lambda l, q: (l, 0, q, 0, 0)),
            pl.BlockSpec((None, None, 1, PAIR_IN), lambda l, q: (l, q, 0, 0)),
        ] + cast_in,
        out_specs=[
            pl.BlockSpec((None, None, PAIR_IN, PAIR_IN + 4 * PAIR_ST), lambda l, q: (l, q, 0, 0)),
            pl.BlockSpec((None, None, 4 * PAIR_ST, PAIR_IN), lambda l, q: (l, q, 0, 0)),
            pl.BlockSpec((None, 4, PAIR_ST), lambda l, q: (l, 0, q)),
        ] + cast_out,
        out_shape=[
            jax.ShapeDtypeStruct((DEPTH, N_PAIR, PAIR_IN, PAIR_IN + 4 * PAIR_ST), bf16),
            jax.ShapeDtypeStruct((DEPTH, N_PAIR, 4 * PAIR_ST, PAIR_IN), bf16),
            jax.ShapeDtypeStruct((DEPTH, 4, SSM_GROUPS * SSM_STATE), f32),
        ] + cast_shapes,
        compiler_params=_params("arbitrary", "arbitrary"),
        name="s5_chunk_matrices",
    )(prow, bt_re, bt_im, ct_re, ct_im, d_vec, *first_f32)
    return res[0], res[1], res[2], tuple(res[3:])


def _chunk_scan(pr, pi, pows, n_rows, seq, reverse):
    pos = lax.broadcasted_iota(jnp.int32, (n_rows, 1), 0) & (seq - 1)
    shift, k = 1, 0
    while shift < seq:
        ar, ai = pows[k]
        if reverse:
            valid = pos < seq - shift
            sr = pltpu.roll(pr, n_rows - shift, axis=0)
            si = pltpu.roll(pi, n_rows - shift, axis=0)
        else:
            valid = pos >= shift
            sr = pltpu.roll(pr, shift, axis=0)
            si = pltpu.roll(pi, shift, axis=0)
        tr, ti = _cmul(ar, ai, sr, si)
        pr = pr + jnp.where(valid, tr, 0.0)
        pi = pi + jnp.where(valid, ti, 0.0)
        shift, k = 2 * shift, k + 1
    return pr, pi


def _s5_kernel(*refs):
    a_refs = refs[0:CHUNK]
    w1_ref, w2_ref, at_ref, h0_ref = refs[CHUNK:CHUNK + 4]
    y_ref, fin_ref = refs[CHUNK + 4:CHUNK + 6]
    r_scr, sp_scr, y_scr, sr_scr, si_scr = refs[CHUNK + 6:]

    lane_blk = lax.broadcasted_iota(jnp.int32, (1, LANES), 1) // PAIR_CH
    row_lat = lax.broadcasted_iota(jnp.int32, (ROWS_LAT, 1), 0)
    pos_ctx = lax.broadcasted_iota(jnp.int32, (ROWS_CTX, 1), 0) & (NC_CTX - 1)
    pos_lat = row_lat & (NC_LAT - 1)

    for p in range(PAIRS_PER_STEP):
        pieces = []
        for k in range(CHUNK // STEPS_PER_LANE_BLOCK):
            piece = None
            for jj in range(STEPS_PER_LANE_BLOCK):
                src = a_refs[STEPS_PER_LANE_BLOCK * k + jj][...]
                sh = (PAIR_CH * (jj - p)) % LANES
                rolled = src if sh == 0 else pltpu.roll(src, sh, axis=1)
                piece = rolled if piece is None else jnp.where(lane_blk == jj, rolled, piece)
            pieces.append(piece.astype(bf16))
        u_pair = jnp.concatenate(pieces, axis=1)
        r_scr[...] = _dot(u_pair, w1_ref[p])

        lanes_p = slice(p * PAIR_ST, (p + 1) * PAIR_ST)
        for d in range(2):
            pows = [(at_ref[2 * d:2 * d + 1, lanes_p], at_ref[2 * d + 1:2 * d + 2, lanes_p])]
            for _ in range(5):
                pows.append(_cmul(*pows[-1], *pows[-1]))
            c_re = PAIR_IN + 2 * d * PAIR_ST
            c_im = c_re + PAIR_ST
            reverse = d == 1

            sr, si = _chunk_scan(r_scr[0:ROWS_CTX, c_re:c_re + PAIR_ST],
                                 r_scr[0:ROWS_CTX, c_im:c_im + PAIR_ST],
                                 pows, ROWS_CTX, NC_CTX, reverse)
            sr_scr[...] = sr
            si_scr[...] = si
            last = 0 if reverse else NC_CTX - 1
            fin_ref[2 * d, :, lanes_p] = sr_scr[pl.ds(last, BATCH, stride=NC_CTX), :]
            fin_ref[2 * d + 1, :, lanes_p] = si_scr[pl.ds(last, BATCH, stride=NC_CTX), :]
            edge = pos_ctx == (NC_CTX - 1 if reverse else 0)
            back = ROWS_CTX - 1 if reverse else 1
            sp_scr[0:ROWS_CTX, 2 * d * PAIR_ST:(2 * d + 1) * PAIR_ST] = jnp.where(
                edge, 0.0, pltpu.roll(sr, back, axis=0))
            sp_scr[0:ROWS_CTX, (2 * d + 1) * PAIR_ST:(2 * d + 2) * PAIR_ST] = jnp.where(
                edge, 0.0, pltpu.roll(si, back, axis=0))

            h0r = jnp.where(row_lat < NC_LAT, h0_ref[2 * d, 0:1, lanes_p], h0_ref[2 * d, 1:2, lanes_p])
            h0i = jnp.where(row_lat < NC_LAT, h0_ref[2 * d + 1, 0:1, lanes_p],
                            h0_ref[2 * d + 1, 1:2, lanes_p])
            edge = pos_lat == (NC_LAT - 1 if reverse else 0)
            ahr, ahi = _cmul(*pows[0], h0r, h0i)
            pr = r_scr[ROWS_CTX:ROWS_S5, c_re:c_re + PAIR_ST] + jnp.where(edge, ahr, 0.0)
            pi = r_scr[ROWS_CTX:ROWS_S5, c_im:c_im + PAIR_ST] + jnp.where(edge, ahi, 0.0)
            sr, si = _chunk_scan(pr, pi, pows, ROWS_LAT, NC_LAT, reverse)
            back = ROWS_LAT - 1 if reverse else 1
            sp_scr[ROWS_CTX:ROWS_S5, 2 * d * PAIR_ST:(2 * d + 1) * PAIR_ST] = jnp.where(
                edge, h0r, pltpu.roll(sr, back, axis=0))
            sp_scr[ROWS_CTX:ROWS_S5, (2 * d + 1) * PAIR_ST:(2 * d + 2) * PAIR_ST] = jnp.where(
                edge, h0i, pltpu.roll(si, back, axis=0))

        y_scr[:, p * PAIR_IN:(p + 1) * PAIR_IN] = (
            r_scr[:, 0:PAIR_IN] + _dot(sp_scr[...].astype(bf16), w2_ref[p]))

    for j in range(CHUNK):
        k, jj = divmod(j, STEPS_PER_LANE_BLOCK)
        out = None
        for p in range(PAIRS_PER_STEP):
            src = y_scr[:, p * PAIR_IN + k * LANES:p * PAIR_IN + (k + 1) * LANES]
            sh = (PAIR_CH * (p - jj)) % LANES
            rolled = src if sh == 0 else pltpu.roll(src, sh, axis=1)
            out = rolled if out is None else jnp.where(lane_blk == p, rolled, out)
        y_ref[:, j, :] = out


def _s5_mixer(u_rows, w1, w2, a_chunk, h0, layer):
    n_state = SSM_GROUPS * SSM_STATE
    n_blk = SSM_WIDTH // LANES
    step_spec = lambda j: pl.BlockSpec((None, ROWS_S5, LANES), lambda g, j=j: (j, 0, g))
    w_pairs = lambda rows, cols: pl.BlockSpec((None, PAIRS_PER_STEP, rows, cols),
                                              lambda g: (layer, g, 0, 0))
    st_lanes = PAIRS_PER_STEP * PAIR_ST
    return pl.pallas_call(
        _s5_kernel,
        grid=(n_blk,),
        in_specs=[step_spec(j) for j in range(CHUNK)] + [
            w_pairs(PAIR_IN, PAIR_IN + 4 * PAIR_ST),
            w_pairs(4 * PAIR_ST, PAIR_IN),
            pl.BlockSpec((None, 4, st_lanes), lambda g: (layer, 0, g)),
            pl.BlockSpec((4, DEC_BATCH, st_lanes), lambda g: (0, 0, g)),
        ],
        out_specs=[
            pl.BlockSpec((ROWS_S5, CHUNK, LANES), lambda g: (0, 0, g)),
            pl.BlockSpec((4, BATCH, st_lanes), lambda g: (0, 0, g)),
        ],
        out_shape=[
            jax.ShapeDtypeStruct((ROWS_S5, CHUNK, SSM_WIDTH), f32),
            jax.ShapeDtypeStruct((4, BATCH, n_state), f32),
        ],
        scratch_shapes=[
            pltpu.VMEM((ROWS_S5, PAIR_IN + 4 * PAIR_ST), f32),
            pltpu.VMEM((ROWS_S5, 4 * PAIR_ST), f32),
            pltpu.VMEM((ROWS_S5, PAIRS_PER_STEP * PAIR_IN), f32),
            pltpu.VMEM((ROWS_CTX, PAIR_ST), f32),
            pltpu.VMEM((ROWS_CTX, PAIR_ST), f32),
        ],
        compiler_params=_params("arbitrary"),
        name="s5_chunked_mixer",
    )(*([u_rows] * CHUNK), w1, w2, a_chunk, h0)


def _shift_rows(v, period, n_rows, shift):
    row = lax.broadcasted_iota(jnp.int32, (n_rows, 1), 0)
    pos = row & (period - 1)
    prev = jnp.where(pos < shift, 0.0, pltpu.roll(v, shift, axis=0))
    nxt = jnp.where(pos >= period - shift, 0.0, pltpu.roll(v, n_rows - shift, axis=0))
    return prev, nxt


def _dwconv3_rows(v, w_ref, period, n_rows, shift=1):
    prev, nxt = _shift_rows(v, period, n_rows, shift)
    return prev * w_ref[0:1, :] + v * w_ref[1:2, :] + nxt * w_ref[2:3, :]


def _k1_kernel(*refs, split_input):
    if split_input:
        xp_ref, xs_ref = refs[:2]
        refs = refs[2:]
    else:
        x_ref = refs[0]
        refs = refs[1:]
    mod_ref, gpre_ref, wgb_ref, wgc_ref, whv_ref, wu_ref, cw_ref, gco_ref, cn_ref, u_ref = refs[:10]
    if split_input:
        x_ref, h_scr, u_scr = refs[10:]
    else:
        h_scr, u_scr = refs[10:]
    i = pl.program_id(0)
    n = pl.program_id(1)
    tm = cn_ref.shape[0]
    n_ctx_tiles = N_CTX // tm

    @pl.when(n == 0)
    def _():
        if split_input:
            @pl.when(i < n_ctx_tiles)
            def _():
                x_ref[...] = xp_ref[...]

            @pl.when(i >= n_ctx_tiles)
            def _():
                x_ref[...] = xs_ref[...]

        r = jnp.where(i < n_ctx_tiles, 0, 1 + (i - n_ctx_tiles) // (DEC_SEQ // tm))
        sh1 = mod_ref[pl.ds(r, 1), 0:D_MODEL]
        sc1 = mod_ref[pl.ds(r, 1), D_MODEL:2 * D_MODEL]
        y = _rms_rows(x_ref[...], gpre_ref[...])
        h_scr[...] = (y * (1.0 + sc1) + sh1).astype(bf16)

    h = h_scr[...]
    u = _dot(h, wu_ref[...])
    for t in range(TC1 // LANES):
        u_scr[t] = u[:, t * LANES:(t + 1) * LANES]
    for j in range(CHUNK):
        for t in range(TC1 // LANES):
            u_ref[j, :, t * LANES:(t + 1) * LANES] = u_scr[t, pl.ds(j, tm // CHUNK, stride=CHUNK), :]
    v = _dot(h, wgc_ref[...]) * _dot(h, whv_ref[...])
    period = jnp.where(i < n_ctx_tiles, SEQ, GRID_W)
    v = _dwconv3_rows(v, cw_ref, period, tm)
    co = _dot(h, wgb_ref[...]) * v
    rr = lax.broadcasted_iota(jnp.int32, (TC1, TC1), 0) // HEAD_DIM
    cc = lax.broadcasted_iota(jnp.int32, (TC1, TC1), 1) // HEAD_DIM
    avg = jnp.where(rr == cc, 1.0 / HEAD_DIM, 0.0).astype(bf16)
    hi, lo = _split_bf16(co * co)
    ms = _dot(hi, avg) + _dot(lo, avg)
    cn_ref[...] = (co * lax.rsqrt(ms + EPS) * gco_ref[...]).astype(bf16)


def _in_proj(xs, mod, g_pre1, w_in_bf, conv_w, g_conv_out, layer):
    nb = CONV_WIDTH // TC1
    split_input = len(xs) == 2
    tm = TM1_SPLIT if split_input else TM1
    n_ctx_tiles = N_CTX // tm
    wspec = lambda off: pl.BlockSpec((D_MODEL, TC1), lambda i, n: (0, off * nb + n))
    x_spec = pl.BlockSpec((tm, D_MODEL), lambda i, n: (i, 0))
    if split_input:
        x_specs = [pl.BlockSpec((tm, D_MODEL), lambda i, n: (jnp.minimum(i, n_ctx_tiles - 1), 0)),
                   pl.BlockSpec((tm, D_MODEL), lambda i, n: (jnp.maximum(i - n_ctx_tiles, 0), 0))]
    else:
        x_specs = [x_spec]
    return pl.pallas_call(
        functools.partial(_k1_kernel, split_input=split_input),
        grid=(N_TOK // tm, nb),
        in_specs=x_specs + [
            pl.BlockSpec((None, 8, 6 * D_MODEL), lambda i, n: (layer, 0, 0)),
            pl.BlockSpec((None, 1, D_MODEL), lambda i, n: (layer, 0, 0)),
            wspec(0), wspec(1), wspec(2), wspec(3),
            pl.BlockSpec((None, 3, TC1), lambda i, n: (layer, 0, n)),
            pl.BlockSpec((None, 1, TC1), lambda i, n: (layer, 0, n)),
        ],
        out_specs=[
            pl.BlockSpec((tm, TC1), lambda i, n: (i, n)),
            pl.BlockSpec((CHUNK, tm // CHUNK, TC1), lambda i, n: (0, i, n)),
        ] + ([x_spec] if split_input else []),
        out_shape=[
            jax.ShapeDtypeStruct((N_TOK, CONV_WIDTH), bf16),
            jax.ShapeDtypeStruct((CHUNK, ROWS_S5, SSM_WIDTH), f32),
        ] + ([jax.ShapeDtypeStruct((N_TOK, D_MODEL), f32)] if split_input else []),
        scratch_shapes=[pltpu.VMEM((tm, D_MODEL), bf16),
                        pltpu.VMEM((TC1 // LANES, tm, LANES), f32)],
        compiler_params=_params("arbitrary", "arbitrary"),
        name="in_proj_conv_mixer",
    )(*xs, mod, g_pre1, w_in_bf, w_in_bf, w_in_bf, w_in_bf, conv_w, g_conv_out)


def _k3_kernel(x_ref, cn_ref, y_ref, mod_ref, wglu_ref, bglu_ref, gsso_ref, wout_ref,
               gpost1_ref, gpre2_ref, x1_ref, h2_ref):
    i = pl.program_id(0)
    n_ctx_tiles = N_CTX // TM3
    r = jnp.where(i < n_ctx_tiles, 0, 1 + (i - n_ctx_tiles) // (DEC_SEQ // TM3))
    gt1 = mod_ref[pl.ds(r, 1), 2 * D_MODEL:3 * D_MODEL]
    sh2 = mod_ref[pl.ds(r, 1), 3 * D_MODEL:4 * D_MODEL]
    sc2 = mod_ref[pl.ds(r, 1), 4 * D_MODEL:5 * D_MODEL]

    for s in range(TM3 // SUB3):
        rows = slice(s * SUB3, (s + 1) * SUB3)
        z = jax.nn.gelu(y_ref[rows, :], approximate=True)
        gate = jax.nn.sigmoid(_dot(z.astype(bf16), wglu_ref[...]) + bglu_ref[...])
        sn = _rms_rows(z * gate, gsso_ref[...])
        mixed = (_dot(cn_ref[rows, :], wout_ref[0:CONV_WIDTH, :])
                 + _dot(sn.astype(bf16), wout_ref[CONV_WIDTH:CONV_WIDTH + SSM_WIDTH, :]))
        x1 = x_ref[rows, :] + gt1 * _rms_rows(mixed, gpost1_ref[...])
        x1_ref[rows, :] = x1
        h2_ref[rows, :] = (_rms_rows(x1, gpre2_ref[...]) * (1.0 + sc2) + sh2).astype(bf16)


def _out_proj(x, cn, y, mod, w_glu_bf, b_glu, g_ssm_out, w_out_bf, g_post1, g_pre2, layer):
    vec = lambda width: pl.BlockSpec((None, 1, width), lambda i: (layer, 0, 0))
    return pl.pallas_call(
        _k3_kernel,
        grid=(N_TOK // TM3,),
        in_specs=[
            pl.BlockSpec((TM3, D_MODEL), lambda i: (i, 0)),
            pl.BlockSpec((TM3, CONV_WIDTH), lambda i: (i, 0)),
            pl.BlockSpec((TM3, SSM_WIDTH), lambda i: (i, 0)),
            pl.BlockSpec((None, 8, 6 * D_MODEL), lambda i: (layer, 0, 0)),
            pl.BlockSpec((SSM_WIDTH, SSM_WIDTH), lambda i: (0, 0)),
            vec(SSM_WIDTH), vec(SSM_WIDTH),
            pl.BlockSpec((CONV_WIDTH + SSM_WIDTH, D_MODEL), lambda i: (0, 0)),
            vec(D_MODEL), vec(D_MODEL),
        ],
        out_specs=[
            pl.BlockSpec((TM3, D_MODEL), lambda i: (i, 0)),
            pl.BlockSpec((TM3, D_MODEL), lambda i: (i, 0)),
        ],
        out_shape=[
            jax.ShapeDtypeStruct((N_TOK, D_MODEL), f32),
            jax.ShapeDtypeStruct((N_TOK, D_MODEL), bf16),
        ],
        compiler_params=_params("arbitrary"),
        name="glu_out_proj",
    )(x, cn, y, mod, w_glu_bf, b_glu, g_ssm_out, w_out_bf, g_post1, g_pre2)


def _k4_kernel(h_ref, x1_ref, mod_ref, wa_ref, wg_ref, cwa_ref, cwg_ref, wd_ref, gpost2_ref,
               *rest, latent):
    n_cast = (len(rest) - 1) // 2
    o_ref = rest[n_cast]
    for src, dst in zip(rest[:n_cast], rest[n_cast + 1:]):
        dst[...] = src[...].astype(bf16)

    i = pl.program_id(0)
    j = pl.program_id(1)
    blk = o_ref.shape

    @pl.when(j == 0)
    def _():
        o_ref[...] = jnp.zeros(blk, f32)

    period, shift = (TM4, LAT_STRIP) if latent else (SEQ, 1)
    h = h_ref[...].reshape(TM4, D_MODEL)
    acts = []
    for sub in range(TF4 // SUB4):
        cols = slice(sub * SUB4, (sub + 1) * SUB4)
        a = _dwconv3_rows(_dot(h, wa_ref[:, cols]), cwa_ref[:, cols], period, TM4, shift)
        g = _dwconv3_rows(_dot(h, wg_ref[:, cols]), cwg_ref[:, cols], period, TM4, shift)
        acts.append((g * jax.nn.sigmoid(g) * a).astype(bf16))
    o_ref[...] += _dot(jnp.concatenate(acts, axis=1), wd_ref[...]).reshape(blk)

    @pl.when(j == pl.num_programs(1) - 1)
    def _():
        r = (1 + i // (GRID_W // LAT_STRIP)) if latent else 0
        gt2 = mod_ref[pl.ds(r, 1), 5 * D_MODEL:6 * D_MODEL]
        ff = o_ref[...].reshape(TM4, D_MODEL)
        x2 = x1_ref[...].reshape(TM4, D_MODEL) + gt2 * _rms_rows(ff, gpost2_ref[...])
        o_ref[...] = x2.reshape(blk)


def _conv_ffn(h2, x1, mod, w_up_bf, ffn_conv_w, w_down_bf, g_post2, layer, next_f32=(), last=False):
    nj = D_FF // TF4
    n_ctx_tiles = N_CTX // TM4

    weight_specs = [
        pl.BlockSpec((None, 8, 6 * D_MODEL), lambda i, j: (layer, 0, 0)),
        pl.BlockSpec((D_MODEL, TF4), lambda i, j: (0, j)),
        pl.BlockSpec((D_MODEL, TF4), lambda i, j: (0, nj + j)),
        pl.BlockSpec((None, 3, TF4), lambda i, j: (layer, 0, j)),
        pl.BlockSpec((None, 3, TF4), lambda i, j: (layer, 0, nj + j)),
        pl.BlockSpec((TF4, D_MODEL), lambda i, j: (j, 0)),
        pl.BlockSpec((None, 1, D_MODEL), lambda i, j: (layer, 0, 0)),
    ]
    weights = (mod, w_up_bf, w_up_bf, ffn_conv_w, ffn_conv_w, w_down_bf, g_post2)

    cast_in, cast_out, cast_shapes = [], [], []
    for w in next_f32:
        _, rows, cols = w.shape
        if cols % nj == 0:
            blk = (rows // n_ctx_tiles, cols // nj)
            idx = lambda i, j: (i, j)
        elif rows % nj == 0:
            blk = (rows // nj, cols // n_ctx_tiles)
            idx = lambda i, j: (j, i)
        else:
            blk = (rows // n_ctx_tiles, cols // n_ctx_tiles)
            idx = lambda i, j: (i, jnp.minimum(j, n_ctx_tiles - 1))
        cast_in.append(pl.BlockSpec((None,) + blk, lambda i, j, idx=idx: (layer + 1,) + idx(i, j)))
        cast_out.append(pl.BlockSpec(blk, idx))
        cast_shapes.append(jax.ShapeDtypeStruct((rows, cols), bf16))
    row_spec = pl.BlockSpec((TM4, D_MODEL), lambda i, j: (i, 0))
    res = pl.pallas_call(
        functools.partial(_k4_kernel, latent=False),
        grid=(n_ctx_tiles, nj),
        in_specs=[row_spec, row_spec] + weight_specs + cast_in,
        out_specs=[row_spec] + cast_out,
        out_shape=[jax.ShapeDtypeStruct((N_CTX if last else N_TOK, D_MODEL), f32)] + cast_shapes,
        input_output_aliases={} if last else {1: 0},
        compiler_params=_params("arbitrary", "arbitrary"),
        name="conv_ffn_ctx",
    )(h2, x1, *weights, *next_f32)

    view = lambda t: t.reshape(-1, GRID_W, D_MODEL)
    rows_per_img = DEC_SEQ // GRID_W
    strips = GRID_W // LAT_STRIP
    first_img = N_CTX // DEC_SEQ
    strip = (rows_per_img, LAT_STRIP, D_MODEL)
    strip_spec = lambda img0: pl.BlockSpec(strip, lambda i, j: (img0 + i // strips, i % strips, 0))
    x2_lat = pl.pallas_call(
        functools.partial(_k4_kernel, latent=True),
        grid=(DEC_BATCH * strips, nj),
        in_specs=[strip_spec(first_img), strip_spec(first_img)] + weight_specs,
        out_specs=strip_spec(0 if last else first_img),
        out_shape=jax.ShapeDtypeStruct(((N_LAT if last else N_TOK) // GRID_W, GRID_W, D_MODEL), f32),
        input_output_aliases={} if last else {1: 0},
        compiler_params=_params("arbitrary", "arbitrary"),
        name="conv_ffn_lat",
    )(view(h2), view(x1 if last else res[0]), *weights)
    if last:
        return (res[0], x2_lat.reshape(N_LAT, D_MODEL)), ()
    return x2_lat.reshape(N_TOK, D_MODEL), tuple(res[1:])


def kernel(x_prompt, x_sample, state_ssm_re, state_ssm_im, c, c_ctx, w_ada, b_ada, g_pre1, w_in, conv_w, ssm_a_re, ssm_a_im, ssm_log_dt, ssm_b_re, ssm_b_im, ssm_c_re, ssm_c_im, ssm_d, w_glu, b_glu, g_conv_out, g_ssm_out, w_out, g_post1, g_pre2, w_up, ffn_conv_w, w_down, g_post2):
    f32_weights = (w_in, w_glu, w_out, w_up, w_down)

    cvec8 = jnp.concatenate([c_ctx[None, :], c, jnp.zeros((8 - 1 - DEC_BATCH, D_MODEL), f32)], axis=0)
    mod = _modulation(cvec8, w_ada, b_ada)

    abr, abi, bfr, bfi = [t.reshape(DEPTH, 2, N_PAIR, PAIR_ST)
                          for t in _discretise(ssm_a_re, ssm_a_im, ssm_log_dt)]
    prow = jnp.stack([abr, abi, bfr, bfi], axis=2)
    prow = prow.transpose(0, 3, 1, 2, 4).reshape(DEPTH, N_PAIR, 8, PAIR_ST)

    def b_rows(b):
        b = b.transpose(0, 1, 2, 4, 3).reshape(DEPTH, 2, N_PAIR, 2, SSM_GROUP, SSM_STATE)
        return b.transpose(0, 1, 2, 4, 3, 5).reshape(DEPTH, 2, N_PAIR, SSM_GROUP, PAIR_ST)

    def c_cols(cm):
        return cm.transpose(0, 1, 2, 4, 3).reshape(DEPTH, 2, N_PAIR, PAIR_ST, SSM_GROUP)

    d_vec = jnp.broadcast_to(ssm_d.reshape(DEPTH, N_PAIR, 1, PAIR_CH), (DEPTH, N_PAIR, CHUNK, PAIR_CH))
    d_vec = d_vec.reshape(DEPTH, N_PAIR, 1, PAIR_IN)
    w1, w2, a_chunk, bf_weights = _s5_matrices(
        prow, b_rows(ssm_b_re), b_rows(ssm_b_im), c_cols(ssm_c_re), c_cols(ssm_c_im), d_vec,
        first_f32=f32_weights)

    g3 = lambda g: g.reshape(DEPTH, 1, -1)
    g_pre1_, g_conv_out_, g_ssm_out_, g_post1_, g_pre2_, g_post2_, b_glu_ = map(
        g3, (g_pre1, g_conv_out, g_ssm_out, g_post1, g_pre2, g_post2, b_glu))

    xs = (x_prompt.reshape(N_CTX, D_MODEL), x_sample.reshape(N_LAT, D_MODEL))
    fins = []
    for l in range(DEPTH):
        last = l + 1 == DEPTH
        w_in_bf, w_glu_bf, w_out_bf, w_up_bf, w_down_bf = bf_weights
        res = _in_proj(xs, mod, g_pre1_, w_in_bf, conv_w, g_conv_out_, l)
        cn, u = res[:2]
        x = res[2] if len(xs) == 2 else xs[0]
        sre = state_ssm_re[:, l].reshape(DEC_BATCH, 2, -1)
        sim = state_ssm_im[:, l].reshape(DEC_BATCH, 2, -1)
        h0 = jnp.stack([sre[:, 0], sim[:, 0], sre[:, 1], sim[:, 1]], axis=0)
        y, fin = _s5_mixer(u, w1, w2, a_chunk, h0, l)
        fins.append(fin)
        x1, h2 = _out_proj(x, cn, y.reshape(N_TOK, SSM_WIDTH), mod, w_glu_bf, b_glu_, g_ssm_out_,
                           w_out_bf, g_post1_, g_pre2_, l)
        x, bf_weights = _conv_ffn(h2, x1, mod, w_up_bf, ffn_conv_w, w_down_bf, g_post2_, l,
                                  next_f32=() if last else f32_weights, last=last)
        xs = (x,)

    fin = jnp.stack(fins, axis=0).reshape(DEPTH, 2, 2, BATCH, SSM_GROUPS, SSM_STATE)
    new_re = fin[:, :, 0].transpose(2, 0, 1, 3, 4)
    new_im = fin[:, :, 1].transpose(2, 0, 1, 3, 4)
    y_prompt, y_sample = x
    return (y_prompt.reshape(BATCH, SEQ, D_MODEL), y_sample.reshape(DEC_BATCH, DEC_SEQ, D_MODEL),
            new_re, new_im)
```

```python
import functools

import jax
import jax.numpy as jnp
from jax import lax
from jax.experimental import pallas as pl
from jax.experimental.pallas import tpu as pltpu

D_MODEL = 2048
BATCH = 16
SEQ = 256
DEPTH = 4
DEC_BATCH = 2
DEC_SEQ = 1024
GRID_W = 64
CONV_WIDTH = 1024
HEAD_DIM = 64
SSM_WIDTH = 1024
SSM_GROUP = 16
SSM_GROUPS = 64
SSM_STATE = 64
D_FF = 5632
EPS = 1e-6

LANES = 128
N_CTX = BATCH * SEQ
N_LAT = DEC_BATCH * DEC_SEQ
N_TOK = N_CTX + N_LAT
CHUNK = 16
NC_CTX = SEQ // CHUNK
NC_LAT = DEC_SEQ // CHUNK
ROWS_CTX = BATCH * NC_CTX
ROWS_LAT = DEC_BATCH * NC_LAT
ROWS_S5 = ROWS_CTX + ROWS_LAT
N_PAIR = SSM_GROUPS // 2
PAIR_CH = 2 * SSM_GROUP
PAIR_IN = CHUNK * PAIR_CH
PAIR_ST = 2 * SSM_STATE
PAIRS_PER_STEP = LANES // PAIR_CH
STEPS_PER_LANE_BLOCK = LANES // PAIR_CH

TM1 = 1024
TM1_SPLIT = 512
TC1 = 256
TM3 = 512
SUB3 = 256
TM4 = 512
TF4 = 512
SUB4 = 256
LAT_STRIP = 32

VMEM_LIMIT = 56 * 1024 * 1024

f32 = jnp.float32
bf16 = jnp.bfloat16


def _dot(a, b):
    return jnp.dot(a, b, preferred_element_type=f32)


def _split_bf16(x):
    hi = x.astype(bf16)
    lo = (x - hi.astype(f32)).astype(bf16)
    return hi, lo


def _dot3(a, b):
    ah, al = _split_bf16(a)
    bh, bl = _split_bf16(b)
    return _dot(ah, bh) + (_dot(ah, bl) + _dot(al, bh))


def _dot_select(a, sel):
    a1 = a.astype(bf16)
    r1 = a - a1.astype(f32)
    a2 = r1.astype(bf16)
    a3 = (r1 - a2.astype(f32)).astype(bf16)
    return _dot(a1, sel) + (_dot(a2, sel) + _dot(a3, sel))


def _rms_rows(x, g):
    ms = jnp.mean(x * x, axis=-1, keepdims=True)
    return x * lax.rsqrt(ms + EPS) * g


def _cmul(ar, ai, br, bi):
    return ar * br - ai * bi, ar * bi + ai * br


def _params(*sem):
    return pltpu.CompilerParams(dimension_semantics=sem, vmem_limit_bytes=VMEM_LIMIT)


def _mod_kernel(cv_ref, w_ref, b_ref, o_ref):
    cv = cv_ref[...]
    s = cv * jax.nn.sigmoid(cv)
    o_ref[...] = _dot(s.astype(bf16), w_ref[...].astype(bf16)) + b_ref[...]


def _modulation(cvec8, w_ada, b_ada):
    tn = 1024
    n_out = 6 * D_MODEL
    return pl.pallas_call(
        _mod_kernel,
        grid=(DEPTH, n_out // tn),
        in_specs=[
            pl.BlockSpec((8, D_MODEL), lambda l, n: (0, 0)),
            pl.BlockSpec((None, D_MODEL, tn), lambda l, n: (l, 0, n)),
            pl.BlockSpec((None, 1, tn), lambda l, n: (l, 0, n)),
        ],
        out_specs=pl.BlockSpec((None, 8, tn), lambda l, n: (l, 0, n)),
        out_shape=jax.ShapeDtypeStruct((DEPTH, 8, n_out), f32),
        compiler_params=_params("arbitrary", "arbitrary"),
        name="adaln_modulation",
    )(cvec8, w_ada, b_ada.reshape(DEPTH, 1, n_out))


def _disc_kernel(ar_ref, ai_ref, ldt_ref, abr_ref, abi_ref, bfr_ref, bfi_ref):
    ar = ar_ref[...]
    ai = ai_ref[...]
    dt = jnp.exp(ldt_ref[...])
    mag = jnp.exp(ar * dt)
    abr = mag * jnp.cos(ai * dt)
    abi = mag * jnp.sin(ai * dt)
    nr = abr - 1.0
    den = ar * ar + ai * ai
    abr_ref[...] = abr
    abi_ref[...] = abi
    bfr_ref[...] = (nr * ar + abi * ai) / den
    bfi_ref[...] = (abi * ar - nr * ai) / den


def _discretise(a_re, a_im, log_dt):
    shape = (DEPTH * 2, SSM_GROUPS * SSM_STATE)
    spec = pl.BlockSpec(shape, lambda: (0, 0))
    out = jax.ShapeDtypeStruct(shape, f32)
    return pl.pallas_call(
        _disc_kernel,
        in_specs=[spec, spec, spec],
        out_specs=[spec, spec, spec, spec],
        out_shape=[out, out, out, out],
        name="s5_discretise",
    )(a_re.reshape(shape), a_im.reshape(shape),
      jnp.broadcast_to(log_dt[..., None], (DEPTH, 2, SSM_GROUPS, SSM_STATE)).reshape(shape))


def _pow_table(br, bi, kk, shape, bits=4):
    tr = ti = None
    pr, pi = br, bi
    for bit in range(bits):
        sel = ((kk >> bit) & 1) == 1
        fr = jnp.broadcast_to(jnp.where(sel, pr, 1.0), shape)
        fi = jnp.broadcast_to(jnp.where(sel, pi, 0.0), shape)
        if tr is None:
            tr, ti = fr, fi
        else:
            tr, ti = _cmul(tr, ti, fr, fi)
        pr, pi = _cmul(pr, pi, pr, pi)
    return tr, ti


def _prep_kernel(pr_ref, btr_ref, bti_ref, ctr_ref, cti_ref, dv_ref, *rest):
    n_cast = (len(rest) - 3) // 2
    w1_ref, w2_ref, at_ref = rest[n_cast:n_cast + 3]
    for src, dst in zip(rest[:n_cast], rest[n_cast + 3:]):
        dst[...] = src[...].astype(bf16)

    lane_in = lax.broadcasted_iota(jnp.int32, (1, PAIR_IN), 1)
    step_of_lane = lane_in >> 5
    grp_of_lane = (lane_in >> 4) & 1
    chan_of_lane = lane_in & (SSM_GROUP - 1)
    lane_st = lax.broadcasted_iota(jnp.int32, (1, PAIR_ST), 1)
    grp_of_st_lane = lane_st >> 6
    row_st = lax.broadcasted_iota(jnp.int32, (PAIR_ST, 1), 0)
    grp_of_st_row = row_st >> 6
    on_diag_st = row_st == lane_st
    chan_row =lax.broadcasted_iota(jnp.int32, (SSM_GROUP, 1), 0)

    lane_blk = lax.broadcasted_iota(jnp.int32, (1, LANES), 1)
    step_in_blk = lane_blk >> 5
    spread_ch = jnp.where((lane_blk & (SSM_GROUP - 1)) == chan_row, 1.0, 0.0).astype(bf16)
    same_grp = grp_of_st_row == ((lane_blk >> 4) & 1)
    n_blk = PAIR_IN // LANES

    taps = []
    for d in range(2):
        abr = pr_ref[4 * d + 0:4 * d + 1, :]
        abi = pr_ref[4 * d + 1:4 * d + 2, :]
        bfr = pr_ref[4 * d + 2:4 * d + 3, :]
        bfi = pr_ref[4 * d + 3:4 * d + 4, :]
        a2 = _cmul(abr, abi, abr, abi)
        a4 = _cmul(*a2, *a2)
        a8 = _cmul(*a4, *a4)
        a16 = _cmul(*a8, *a8)
        at_ref[2 * d:2 * d + 1, :] = a16[0]
        at_ref[2 * d + 1:2 * d + 2, :] = a16[1]
        bbr, bbi = _cmul(bfr, bfi, btr_ref[d], bti_ref[d])
        kk_rows = (CHUNK - 1 - chan_row) if d == 0 else chan_row
        pcr, pci = _pow_table(abr, abi, kk_rows, (CHUNK, PAIR_ST))
        pbr = jnp.concatenate([jnp.broadcast_to(pcr[j:j + 1, :], (PAIR_CH, PAIR_ST))
                               for j in range(CHUNK)], axis=0)
        pbi = jnp.concatenate([jnp.broadcast_to(pci[j:j + 1, :], (PAIR_CH, PAIR_ST))
                               for j in range(CHUNK)], axis=0)
        slab_r = jnp.concatenate([jnp.where(grp_of_st_lane == e, bbr, 0.0) for e in range(2)], axis=0)
        slab_i = jnp.concatenate([jnp.where(grp_of_st_lane == e, bbi, 0.0) for e in range(2)], axis=0)
        tile_r = jnp.concatenate([slab_r] * CHUNK, axis=0)
        tile_i = jnp.concatenate([slab_i] * CHUNK, axis=0)
        wbr, wbi = _cmul(pbr, pbi, tile_r, tile_i)
        for part, wb in ((0, wbr), (1, wbi)):
            c0 = PAIR_IN + (2 * d + part) * PAIR_ST
            w1_ref[:, c0:c0 + PAIR_ST] = wb.astype(bf16)

        acr = jnp.sum(jnp.where(on_diag_st, abr, 0.0), axis=1, keepdims=True)
        aci = jnp.sum(jnp.where(on_diag_st, abi, 0.0), axis=1, keepdims=True)
        ctr = jnp.where(same_grp, _dot_select(ctr_ref[d], spread_ch), 0.0)
        cti = jnp.where(same_grp, _dot_select(cti_ref[d], spread_ch), 0.0)
        kk_blk = (step_in_blk + 1) if d == 0 else (STEPS_PER_LANE_BLOCK - step_in_blk)
        m0r, m0i = _cmul(*_pow_table(acr, aci, kk_blk, (PAIR_ST, LANES), bits=3), ctr, cti)
        c2 = _cmul(acr, aci, acr, aci)
        c4 = _cmul(*c2, *c2)
        c8 = _cmul(*c4, *c4)
        c12 = _cmul(*c8, *c4)
        blocks = [(m0r, m0i)] + [_cmul(*cm, m0r, m0i) for cm in (c4, c8, c12)]
        if d == 1:
            blocks = blocks[::-1]
        ca1r = jnp.concatenate([b[0] for b in blocks], axis=1)
        ca1i = jnp.concatenate([b[1] for b in blocks], axis=1)
        ctr4 = jnp.concatenate([ctr] * n_blk, axis=1)
        cti4 = jnp.concatenate([cti] * n_blk, axis=1)
        if d == 0:
            ca0r = jnp.where(lane_in < PAIR_CH, ctr4, pltpu.roll(ca1r, PAIR_CH, axis=1))
            ca0i = jnp.where(lane_in < PAIR_CH, cti4, pltpu.roll(ca1i, PAIR_CH, axis=1))
        else:
            ca0r = jnp.where(lane_in >= PAIR_IN - PAIR_CH, ctr4, pltpu.roll(ca1r, PAIR_IN - PAIR_CH, axis=1))
            ca0i = jnp.where(lane_in >= PAIR_IN - PAIR_CH, cti4, pltpu.roll(ca1i, PAIR_IN - PAIR_CH, axis=1))
        w2_ref[(2 * d) * PAIR_ST:(2 * d + 1) * PAIR_ST, :] = ca1r.astype(bf16)
        w2_ref[(2 * d + 1) * PAIR_ST:(2 * d + 2) * PAIR_ST, :] = (-ca1i).astype(bf16)
        taps.append(_dot3(bbr, ca0r) - _dot3(bbi, ca0i))

    gf, gb = taps
    dvec = dv_ref[...]
    for jp in range(CHUNK):
        lo = PAIR_CH * jp
        hi = PAIR_CH * (jp + 1)
        rf = gf if jp == 0 else pltpu.roll(gf, lo, axis=1)
        rb = gb if jp == CHUNK - 1 else pltpu.roll(gb, hi, axis=1)
        blk = jnp.where(lane_in >= lo, rf, 0.0) + jnp.where(lane_in < hi, rb, 0.0)
        on_diag = (step_of_lane == jp) & (chan_of_lane == chan_row)
        blk = blk + jnp.where(on_diag, dvec, 0.0)
        for e in range(2):
            r0 = lo + e * SSM_GROUP
            w1_ref[r0:r0 + SSM_GROUP, 0:PAIR_IN] = jnp.where(grp_of_lane == e, blk, 0.0).astype(bf16)


def _s5_matrices(prow, bt_re, bt_im, ct_re, ct_im, d_vec, first_f32=()):
    n_steps = DEPTH * N_PAIR
    cast_in, cast_out, cast_shapes = [], [], []
    for w in first_f32:
        _, rows, cols = w.shape
        n_blk = next(n for n in (n_steps, n_steps // 2, n_steps // 4)
                     if rows % n == 0 and (rows // n) % 16 == 0)
        idx = lambda l, q, n_blk=n_blk: (jnp.minimum(l * N_PAIR + q, n_blk - 1), 0)
        cast_in.append(pl.BlockSpec((None, rows // n_blk, cols), lambda l, q, idx=idx: (0,) + idx(l, q)))
        cast_out.append(pl.BlockSpec((rows // n_blk, cols), idx))
        cast_shapes.append(jax.ShapeDtypeStruct((rows, cols), bf16))
    res = pl.pallas_call(
        _prep_kernel,
        grid=(DEPTH, N_PAIR),
        in_specs=[
            pl.BlockSpec((None, None, 8, PAIR_ST), lambda l, q: (l, q, 0, 0)),
            pl.BlockSpec((None, 2, None, SSM_GROUP, PAIR_ST), lambda l, q: (l, 0, q, 0, 0)),
            pl.BlockSpec((None, 2, None, SSM_GROUP, PAIR_ST), lambda l, q: (l, 0, q, 0, 0)),
            pl.BlockSpec((None, 2, None, PAIR_ST, SSM_GROUP), lambda l, q: (l, 0, q, 0, 0)),
            pl.BlockSpec((None, 2, None, PAIR_ST, SSM_GROUP), lambda l, q: (l, 0, q, 0, 0)),
            pl.BlockSpec((None, None, 1, PAIR_IN), lambda l, q: (l, q, 0, 0)),
        ] + cast_in,
        out_specs=[
            pl.BlockSpec((None, None, PAIR_IN, PAIR_IN + 4 * PAIR_ST), lambda l, q: (l, q, 0, 0)),
            pl.BlockSpec((None, None, 4 * PAIR_ST, PAIR_IN), lambda l, q: (l, q, 0, 0)),
            pl.BlockSpec((None, 4, PAIR_ST), lambda l, q: (l, 0, q)),
        ] + cast_out,
        out_shape=[
            jax.ShapeDtypeStruct((DEPTH, N_PAIR, PAIR_IN, PAIR_IN + 4 * PAIR_ST), bf16),
            jax.ShapeDtypeStruct((DEPTH, N_PAIR, 4 * PAIR_ST, PAIR_IN), bf16),
            jax.ShapeDtypeStruct((DEPTH, 4, SSM_GROUPS * SSM_STATE), f32),
        ] + cast_shapes,
        compiler_params=_params("arbitrary", "arbitrary"),
        name="s5_chunk_matrices",
    )(prow, bt_re, bt_im, ct_re, ct_im, d_vec, *first_f32)
    return res[0], res[1], res[2], tuple(res[3:])


def _chunk_scan(pr, pi, pows, n_rows, seq, reverse):
    pos = lax.broadcasted_iota(jnp.int32, (n_rows, 1), 0) & (seq - 1)
    shift, k = 1, 0
    while shift < seq:
        ar, ai = pows[k]
        if reverse:
            valid = pos < seq - shift
            sr = pltpu.roll(pr, n_rows - shift, axis=0)
            si = pltpu.roll(pi, n_rows - shift, axis=0)
        else:
            valid = pos >= shift
            sr = pltpu.roll(pr, shift, axis=0)
            si = pltpu.roll(pi, shift, axis=0)
        tr, ti = _cmul(ar, ai, sr, si)
        pr = pr + jnp.where(valid, tr, 0.0)
        pi = pi + jnp.where(valid, ti, 0.0)
        shift, k = 2 * shift, k + 1
    return pr, pi


def _s5_kernel(*refs):
    a_refs = refs[0:CHUNK]
    w1_ref, w2_ref, at_ref, h0_ref = refs[CHUNK:CHUNK + 4]
    y_ref, fin_ref = refs[CHUNK + 4:CHUNK + 6]
    r_scr, sp_scr, y_scr, sr_scr, si_scr = refs[CHUNK + 6:]

    lane_blk = lax.broadcasted_iota(jnp.int32, (1, LANES), 1) // PAIR_CH
    row_lat = lax.broadcasted_iota(jnp.int32, (ROWS_LAT, 1), 0)
    pos_ctx = lax.broadcasted_iota(jnp.int32, (ROWS_CTX, 1), 0) & (NC_CTX - 1)
    pos_lat = row_lat & (NC_LAT - 1)

    for p in range(PAIRS_PER_STEP):
        pieces = []
        for k in range(CHUNK // STEPS_PER_LANE_BLOCK):
            piece = None
            for jj in range(STEPS_PER_LANE_BLOCK):
                src = a_refs[STEPS_PER_LANE_BLOCK * k + jj][...]
                sh = (PAIR_CH * (jj - p)) % LANES
                rolled = src if sh == 0 else pltpu.roll(src, sh, axis=1)
                piece = rolled if piece is None else jnp.where(lane_blk == jj, rolled, piece)
            pieces.append(piece.astype(bf16))
        u_pair = jnp.concatenate(pieces, axis=1)
        r_scr[...] = _dot(u_pair, w1_ref[p])

        lanes_p = slice(p * PAIR_ST, (p + 1) * PAIR_ST)
        for d in range(2):
            pows = [(at_ref[2 * d:2 * d + 1, lanes_p], at_ref[2 * d + 1:2 * d + 2, lanes_p])]
            for _ in range(5):
                pows.append(_cmul(*pows[-1], *pows[-1]))
            c_re = PAIR_IN + 2 * d * PAIR_ST
            c_im = c_re + PAIR_ST
            reverse = d == 1

            sr, si = _chunk_scan(r_scr[0:ROWS_CTX, c_re:c_re + PAIR_ST],
                                 r_scr[0:ROWS_CTX, c_im:c_im + PAIR_ST],
                                 pows, ROWS_CTX, NC_CTX, reverse)
            sr_scr[...] = sr
            si_scr[...] = si
            last = 0 if reverse else NC_CTX - 1
            fin_ref[2 * d, :, lanes_p] = sr_scr[pl.ds(last, BATCH, stride=NC_CTX), :]
            fin_ref[2 * d + 1, :, lanes_p] = si_scr[pl.ds(last, BATCH, stride=NC_CTX), :]
            edge = pos_ctx == (NC_CTX - 1 if reverse else 0)
            back = ROWS_CTX - 1 if reverse else 1
            sp_scr[0:ROWS_CTX, 2 * d * PAIR_ST:(2 * d + 1) * PAIR_ST] = jnp.where(
                edge, 0.0, pltpu.roll(sr, back, axis=0))
            sp_scr[0:ROWS_CTX, (2 * d + 1) * PAIR_ST:(2 * d + 2) * PAIR_ST] = jnp.where(
                edge, 0.0, pltpu.roll(si, back, axis=0))

            h0r = jnp.where(row_lat < NC_LAT, h0_ref[2 * d, 0:1, lanes_p], h0_ref[2 * d, 1:2, lanes_p])
            h0i = jnp.where(row_lat < NC_LAT, h0_ref[2 * d + 1, 0:1, lanes_p],
                            h0_ref[2 * d + 1, 1:2, lanes_p])
            edge = pos_lat == (NC_LAT - 1 if reverse else 0)
            ahr, ahi = _cmul(*pows[0], h0r, h0i)
            pr = r_scr[ROWS_CTX:ROWS_S5, c_re:c_re + PAIR_ST] + jnp.where(edge, ahr, 0.0)
            pi = r_scr[ROWS_CTX:ROWS_S5, c_im:c_im + PAIR_ST] + jnp.where(edge, ahi, 0.0)
            sr, si = _chunk_scan(pr, pi, pows, ROWS_LAT, NC_LAT, reverse)
            back = ROWS_LAT - 1 if reverse else 1
            sp_scr[ROWS_CTX:ROWS_S5, 2 * d * PAIR_ST:(2 * d + 1) * PAIR_ST] = jnp.where(
                edge, h0r, pltpu.roll(sr, back, axis=0))
            sp_scr[ROWS_CTX:ROWS_S5, (2 * d + 1) * PAIR_ST:(2 * d + 2) * PAIR_ST] = jnp.where(
                edge, h0i, pltpu.roll(si, back, axis=0))

        y_scr[:, p * PAIR_IN:(p + 1) * PAIR_IN] = (
            r_scr[:, 0:PAIR_IN] + _dot(sp_scr[...].astype(bf16), w2_ref[p]))

    for j in range(CHUNK):
        k, jj = divmod(j, STEPS_PER_LANE_BLOCK)
        out = None
        for p in range(PAIRS_PER_STEP):
            src = y_scr[:, p * PAIR_IN + k * LANES:p * PAIR_IN + (k + 1) * LANES]
            sh = (PAIR_CH * (p - jj)) % LANES
            rolled = src if sh == 0 else pltpu.roll(src, sh, axis=1)
            out = rolled if out is None else jnp.where(lane_blk == p, rolled, out)
        y_ref[:, j, :] = out


def _s5_mixer(u_rows, w1, w2, a_chunk, h0, layer):
    n_state = SSM_GROUPS * SSM_STATE
    n_blk = SSM_WIDTH // LANES
    step_spec = lambda j: pl.BlockSpec((None, ROWS_S5, LANES), lambda g, j=j: (j, 0, g))
    w_pairs = lambda rows, cols: pl.BlockSpec((None, PAIRS_PER_STEP, rows, cols),
                                              lambda g: (layer, g, 0, 0))
    st_lanes = PAIRS_PER_STEP * PAIR_ST
    return pl.pallas_call(
        _s5_kernel,
        grid=(n_blk,),
        in_specs=[step_spec(j) for j in range(CHUNK)] + [
            w_pairs(PAIR_IN, PAIR_IN + 4 * PAIR_ST),
            w_pairs(4 * PAIR_ST, PAIR_IN),
            pl.BlockSpec((None, 4, st_lanes), lambda g: (layer, 0, g)),
            pl.BlockSpec((4, DEC_BATCH, st_lanes), lambda g: (0, 0, g)),
        ],
        out_specs=[
            pl.BlockSpec((ROWS_S5, CHUNK, LANES), lambda g: (0, 0, g)),
            pl.BlockSpec((4, BATCH, st_lanes), lambda g: (0, 0, g)),
        ],
        out_shape=[
            jax.ShapeDtypeStruct((ROWS_S5, CHUNK, SSM_WIDTH), f32),
            jax.ShapeDtypeStruct((4, BATCH, n_state), f32),
        ],
        scratch_shapes=[
            pltpu.VMEM((ROWS_S5, PAIR_IN + 4 * PAIR_ST), f32),
            pltpu.VMEM((ROWS_S5, 4 * PAIR_ST), f32),
            pltpu.VMEM((ROWS_S5, PAIRS_PER_STEP * PAIR_IN), f32),
            pltpu.VMEM((ROWS_CTX, PAIR_ST), f32),
            pltpu.VMEM((ROWS_CTX, PAIR_ST), f32),
        ],
        compiler_params=_params("arbitrary"),
        name="s5_chunked_mixer",
    )(*([u_rows] * CHUNK), w1, w2, a_chunk, h0)


def _shift_rows(v, period, n_rows, shift):
    row = lax.broadcasted_iota(jnp.int32, (n_rows, 1), 0)
    pos = row & (period - 1)
    prev = jnp.where(pos < shift, 0.0, pltpu.roll(v, shift, axis=0))
    nxt = jnp.where(pos >= period - shift, 0.0, pltpu.roll(v, n_rows - shift, axis=0))
    return prev, nxt


def _dwconv3_rows(v, w_ref, period, n_rows, shift=1):
    prev, nxt = _shift_rows(v, period, n_rows, shift)
    return prev * w_ref[0:1, :] + v * w_ref[1:2, :] + nxt * w_ref[2:3, :]


def _dwconv3_rows_static(v, w, period, n_rows, shift):
    w0, w1, w2 = w[0:1, :], w[1:2, :], w[2:3, :]
    if shift % 8 == 0 and period == n_rows:
        zeros = jnp.zeros((shift, v.shape[1]), v.dtype)
        prev = jnp.concatenate([zeros, v[:n_rows - shift]], axis=0)
        nxt = jnp.concatenate([v[shift:], zeros], axis=0)
        return prev * w0 + v * w1 + nxt * w2
    prev = pltpu.roll(v, shift, axis=0)
    nxt = pltpu.roll(v, n_rows - shift, axis=0)
    out = prev * w0 + v * w1 + nxt * w2
    sub = lax.broadcasted_iota(jnp.int32, (8, 1), 0)
    pieces = []
    for lo in range(0, n_rows, period):
        hi = lo + period
        first = (jnp.where(sub < shift, 0.0, prev[lo:lo + 8]) * w0 + v[lo:lo + 8] * w1
                 + nxt[lo:lo + 8] * w2)
        last = (prev[hi - 8:hi] * w0 + v[hi - 8:hi] * w1
                + jnp.where(sub >= 8 - shift, 0.0, nxt[hi - 8:hi]) * w2)
        pieces += [first, out[lo + 8:hi - 8], last]
    return jnp.concatenate(pieces, axis=0)


def _k1_kernel(*refs, split_input):
    if split_input:
        xp_ref, xs_ref = refs[:2]
        refs = refs[2:]
    else:
        x_ref = refs[0]
        refs = refs[1:]
    mod_ref, gpre_ref, wgb_ref, wgc_ref, whv_ref, wu_ref, cw_ref, gco_ref, cn_ref, u_ref = refs[:10]
    if split_input:
        x_ref, h_scr, u_scr = refs[10:]
    else:
        h_scr, u_scr = refs[10:]
    i = pl.program_id(0)
    n = pl.program_id(1)
    tm = cn_ref.shape[0]
    n_ctx_tiles = N_CTX // tm

    @pl.when(n == 0)
    def _():
        if split_input:
            @pl.when(i < n_ctx_tiles)
            def _():
                x_ref[...] = xp_ref[...]

            @pl.when(i >= n_ctx_tiles)
            def _():
                x_ref[...] = xs_ref[...]

        r = jnp.where(i < n_ctx_tiles, 0, 1 + (i - n_ctx_tiles) // (DEC_SEQ // tm))
        sh1 = mod_ref[pl.ds(r, 1), 0:D_MODEL]
        sc1 = mod_ref[pl.ds(r, 1), D_MODEL:2 * D_MODEL]
        h_scr[...] = (_rms_rows(x_ref[...], gpre_ref[...] * (1.0 + sc1)) + sh1).astype(bf16)

    h = h_scr[...]
    u = _dot(h, wu_ref[...])
    for t in range(TC1 // LANES):
        u_scr[t] = u[:, t * LANES:(t + 1) * LANES]
    for j in range(CHUNK):
        for t in range(TC1 // LANES):
            u_ref[j, :, t * LANES:(t + 1) * LANES] = u_scr[t, pl.ds(j, tm // CHUNK, stride=CHUNK), :]
    v = _dot(h, wgc_ref[...]) * _dot(h, whv_ref[...])
    period = jnp.where(i < n_ctx_tiles, SEQ, GRID_W)
    v = _dwconv3_rows(v, cw_ref, period, tm)
    co = _dot(h, wgb_ref[...]) * v
    rr = lax.broadcasted_iota(jnp.int32, (TC1, TC1), 0) // HEAD_DIM
    cc = lax.broadcasted_iota(jnp.int32, (TC1, TC1), 1) // HEAD_DIM
    avg = jnp.where(rr == cc, 1.0 / HEAD_DIM, 0.0).astype(bf16)
    hi, lo = _split_bf16(co * co)
    ms = _dot(hi, avg) + _dot(lo, avg)
    cn_ref[...] = (co * lax.rsqrt(ms + EPS) * gco_ref[...]).astype(bf16)


def _in_proj(xs, mod, g_pre1, w_in_bf, conv_w, g_conv_out, layer):
    nb = CONV_WIDTH // TC1
    split_input = len(xs) == 2
    tm = TM1_SPLIT if split_input else TM1
    n_ctx_tiles = N_CTX // tm
    wspec = lambda off: pl.BlockSpec((D_MODEL, TC1), lambda i, n: (0, off * nb + n))
    x_spec = pl.BlockSpec((tm, D_MODEL), lambda i, n: (i, 0))
    if split_input:
        x_specs = [pl.BlockSpec((tm, D_MODEL), lambda i, n: (jnp.minimum(i, n_ctx_tiles - 1), 0)),
                   pl.BlockSpec((tm, D_MODEL), lambda i, n: (jnp.maximum(i - n_ctx_tiles, 0), 0))]
    else:
        x_specs = [x_spec]
    return pl.pallas_call(
        functools.partial(_k1_kernel, split_input=split_input),
        grid=(N_TOK // tm, nb),
        in_specs=x_specs + [
            pl.BlockSpec((None, 8, 6 * D_MODEL), lambda i, n: (layer, 0, 0)),
            pl.BlockSpec((None, 1, D_MODEL), lambda i, n: (layer, 0, 0)),
            wspec(0), wspec(1), wspec(2), wspec(3),
            pl.BlockSpec((None, 3, TC1), lambda i, n: (layer, 0, n)),
            pl.BlockSpec((None, 1, TC1), lambda i, n: (layer, 0, n)),
        ],
        out_specs=[
            pl.BlockSpec((tm, TC1), lambda i, n: (i, n)),
            pl.BlockSpec((CHUNK, tm // CHUNK, TC1), lambda i, n: (0, i, n)),
        ] + ([x_spec] if split_input else []),
        out_shape=[
            jax.ShapeDtypeStruct((N_TOK, CONV_WIDTH), bf16),
            jax.ShapeDtypeStruct((CHUNK, ROWS_S5, SSM_WIDTH), f32),
        ] + ([jax.ShapeDtypeStruct((N_TOK, D_MODEL), f32)] if split_input else []),
        scratch_shapes=[pltpu.VMEM((tm, D_MODEL), bf16),
                        pltpu.VMEM((TC1 // LANES, tm, LANES), f32)],
        compiler_params=_params("arbitrary", "arbitrary"),
        name="in_proj_conv_mixer",
    )(*xs, mod, g_pre1, w_in_bf, w_in_bf, w_in_bf, w_in_bf, conv_w, g_conv_out)


def _k3_kernel(x_ref, cn_ref, y_ref, mod_ref, wglu_ref, bglu_ref, gsso_ref, wout_ref,
               gpost1_ref, gpre2_ref, x1_ref, h2_ref):
    i = pl.program_id(0)
    n_ctx_tiles = N_CTX // TM3
    r = jnp.where(i < n_ctx_tiles, 0, 1 + (i - n_ctx_tiles) // (DEC_SEQ // TM3))
    gt1 = mod_ref[pl.ds(r, 1), 2 * D_MODEL:3 * D_MODEL]
    sh2 = mod_ref[pl.ds(r, 1), 3 * D_MODEL:4 * D_MODEL]
    sc2 = mod_ref[pl.ds(r, 1), 4 * D_MODEL:5 * D_MODEL]

    gain1 = gt1 * gpost1_ref[...]
    gain2 = gpre2_ref[...] * (1.0 + sc2)
    for s in range(TM3 // SUB3):
        rows = slice(s * SUB3, (s + 1) * SUB3)
        z = jax.nn.gelu(y_ref[rows, :], approximate=True)
        gate = jax.nn.sigmoid(_dot(z.astype(bf16), wglu_ref[...]) + bglu_ref[...])
        sn = _rms_rows(z * gate, gsso_ref[...])
        mixed = (_dot(cn_ref[rows, :], wout_ref[0:CONV_WIDTH, :])
                 + _dot(sn.astype(bf16), wout_ref[CONV_WIDTH:CONV_WIDTH + SSM_WIDTH, :]))
        x1 = x_ref[rows, :] + _rms_rows(mixed, gain1)
        x1_ref[rows, :] = x1
        h2_ref[rows, :] = (_rms_rows(x1, gain2) + sh2).astype(bf16)


def _out_proj(x, cn, y, mod, w_glu_bf, b_glu, g_ssm_out, w_out_bf, g_post1, g_pre2, layer):
    vec = lambda width: pl.BlockSpec((None, 1, width), lambda i: (layer, 0, 0))
    return pl.pallas_call(
        _k3_kernel,
        grid=(N_TOK // TM3,),
        in_specs=[
            pl.BlockSpec((TM3, D_MODEL), lambda i: (i, 0)),
            pl.BlockSpec((TM3, CONV_WIDTH), lambda i: (i, 0)),
            pl.BlockSpec((TM3, SSM_WIDTH), lambda i: (i, 0)),
            pl.BlockSpec((None, 8, 6 * D_MODEL), lambda i: (layer, 0, 0)),
            pl.BlockSpec((SSM_WIDTH, SSM_WIDTH), lambda i: (0, 0)),
            vec(SSM_WIDTH), vec(SSM_WIDTH),
            pl.BlockSpec((CONV_WIDTH + SSM_WIDTH, D_MODEL), lambda i: (0, 0)),
            vec(D_MODEL), vec(D_MODEL),
        ],
        out_specs=[
            pl.BlockSpec((TM3, D_MODEL), lambda i: (i, 0)),
            pl.BlockSpec((TM3, D_MODEL), lambda i: (i, 0)),
        ],
        out_shape=[
            jax.ShapeDtypeStruct((N_TOK, D_MODEL), f32),
            jax.ShapeDtypeStruct((N_TOK, D_MODEL), bf16),
        ],
        compiler_params=_params("arbitrary"),
        name="glu_out_proj",
    )(x, cn, y, mod, w_glu_bf, b_glu, g_ssm_out, w_out_bf, g_post1, g_pre2)


def _k4_kernel(h_ref, x1_ref, mod_ref, wa_ref, wg_ref, cwa_ref, cwg_ref, wd_ref, gpost2_ref,
               *rest, latent):
    n_cast = (len(rest) - 1) // 2
    o_ref = rest[n_cast]
    for src, dst in zip(rest[:n_cast], rest[n_cast + 1:]):
        dst[...] = src[...].astype(bf16)

    i = pl.program_id(0)
    j = pl.program_id(1)
    blk = o_ref.shape

    @pl.when(j == 0)
    def _():
        o_ref[...] = jnp.zeros(blk, f32)

    period, shift = (TM4, LAT_STRIP) if latent else (SEQ, 1)
    h = h_ref[...].reshape(TM4, D_MODEL)
    acts = []
    for sub in range(TF4 // SUB4):
        cols = slice(sub * SUB4, (sub + 1) * SUB4)
        a = _dwconv3_rows_static(_dot(h, wa_ref[:, cols]), cwa_ref[:, cols], period, TM4, shift)
        g = _dwconv3_rows_static(_dot(h, wg_ref[:, cols]), cwg_ref[:, cols], period, TM4, shift)
        acts.append((g * jax.nn.sigmoid(g) * a).astype(bf16))
    o_ref[...] += _dot(jnp.concatenate(acts, axis=1), wd_ref[...]).reshape(blk)

    @pl.when(j == pl.num_programs(1) - 1)
    def _():
        r = (1 + i // (GRID_W // LAT_STRIP)) if latent else 0
        gt2 = mod_ref[pl.ds(r, 1), 5 * D_MODEL:6 * D_MODEL]
        ff = o_ref[...].reshape(TM4, D_MODEL)
        x2 = x1_ref[...].reshape(TM4, D_MODEL) + _rms_rows(ff, gt2 * gpost2_ref[...])
        o_ref[...] = x2.reshape(blk)


def _conv_ffn(h2, x1, mod, w_up_bf, ffn_conv_w, w_down_bf, g_post2, layer, next_f32=(), last=False):
    nj = D_FF // TF4
    n_ctx_tiles = N_CTX // TM4

    weight_specs = [
        pl.BlockSpec((None, 8, 6 * D_MODEL), lambda i, j: (layer, 0, 0)),
        pl.BlockSpec((D_MODEL, TF4), lambda i, j: (0, j)),
        pl.BlockSpec((D_MODEL, TF4), lambda i, j: (0, nj + j)),
        pl.BlockSpec((None, 3, TF4), lambda i, j: (layer, 0, j)),
        pl.BlockSpec((None, 3, TF4), lambda i, j: (layer, 0, nj + j)),
        pl.BlockSpec((TF4, D_MODEL), lambda i, j: (j, 0)),
        pl.BlockSpec((None, 1, D_MODEL), lambda i, j: (layer, 0, 0)),
    ]
    weights = (mod, w_up_bf, w_up_bf, ffn_conv_w, ffn_conv_w, w_down_bf, g_post2)

    cast_in, cast_out, cast_shapes = [], [], []
    for w in next_f32:
        _, rows, cols = w.shape
        if cols % nj == 0:
            blk = (rows // n_ctx_tiles, cols // nj)
            idx = lambda i, j: (i, j)
        elif rows % nj == 0:
            blk = (rows // nj, cols // n_ctx_tiles)
            idx = lambda i, j: (j, i)
        else:
            blk = (rows // n_ctx_tiles, cols // n_ctx_tiles)
            idx = lambda i, j: (i, jnp.minimum(j, n_ctx_tiles - 1))
        cast_in.append(pl.BlockSpec((None,) + blk, lambda i, j, idx=idx: (layer + 1,) + idx(i, j)))
        cast_out.append(pl.BlockSpec(blk, idx))
        cast_shapes.append(jax.ShapeDtypeStruct((rows, cols), bf16))
    row_spec = pl.BlockSpec((TM4, D_MODEL), lambda i, j: (i, 0))
    res = pl.pallas_call(
        functools.partial(_k4_kernel, latent=False),
        grid=(n_ctx_tiles, nj),
        in_specs=[row_spec, row_spec] + weight_specs + cast_in,
        out_specs=[row_spec] + cast_out,
        out_shape=[jax.ShapeDtypeStruct((N_CTX if last else N_TOK, D_MODEL), f32)] + cast_shapes,
        input_output_aliases={} if last else {1: 0},
        compiler_params=_params("arbitrary", "arbitrary"),
        name="conv_ffn_ctx",
    )(h2, x1, *weights, *next_f32)

    view = lambda t: t.reshape(-1, GRID_W, D_MODEL)
    rows_per_img = DEC_SEQ // GRID_W
    strips = GRID_W // LAT_STRIP
    first_img = N_CTX // DEC_SEQ
    strip = (rows_per_img, LAT_STRIP, D_MODEL)
    strip_spec = lambda img0: pl.BlockSpec(strip, lambda i, j: (img0 + i // strips, i % strips, 0))
    x2_lat = pl.pallas_call(
        functools.partial(_k4_kernel, latent=True),
        grid=(DEC_BATCH * strips, nj),
        in_specs=[strip_spec(first_img), strip_spec(first_img)] + weight_specs,
        out_specs=strip_spec(0 if last else first_img),
        out_shape=jax.ShapeDtypeStruct(((N_LAT if last else N_TOK) // GRID_W, GRID_W, D_MODEL), f32),
        input_output_aliases={} if last else {1: 0},
        compiler_params=_params("arbitrary", "arbitrary"),
        name="conv_ffn_lat",
    )(view(h2), view(x1 if last else res[0]), *weights)
    if last:
        return (res[0], x2_lat.reshape(N_LAT, D_MODEL)), ()
    return x2_lat.reshape(N_TOK, D_MODEL), tuple(res[1:])


def kernel(x_prompt, x_sample, state_ssm_re, state_ssm_im, c, c_ctx, w_ada, b_ada, g_pre1, w_in, conv_w, ssm_a_re, ssm_a_im, ssm_log_dt, ssm_b_re, ssm_b_im, ssm_c_re, ssm_c_im, ssm_d, w_glu, b_glu, g_conv_out, g_ssm_out, w_out, g_post1, g_pre2, w_up, ffn_conv_w, w_down, g_post2):
    f32_weights = (w_in, w_glu, w_out, w_up, w_down)

    cvec8 = jnp.concatenate([c_ctx[None, :], c, jnp.zeros((8 - 1 - DEC_BATCH, D_MODEL), f32)], axis=0)
    mod = _modulation(cvec8, w_ada, b_ada)

    abr, abi, bfr, bfi = [t.reshape(DEPTH, 2, N_PAIR, PAIR_ST)
                          for t in _discretise(ssm_a_re, ssm_a_im, ssm_log_dt)]
    prow = jnp.stack([abr, abi, bfr, bfi], axis=2)
    prow = prow.transpose(0, 3, 1, 2, 4).reshape(DEPTH, N_PAIR, 8, PAIR_ST)

    def b_rows(b):
        b = b.transpose(0, 1, 2, 4, 3).reshape(DEPTH, 2, N_PAIR, 2, SSM_GROUP, SSM_STATE)
        return b.transpose(0, 1, 2, 4, 3, 5).reshape(DEPTH, 2, N_PAIR, SSM_GROUP, PAIR_ST)

    def c_cols(cm):
        return cm.transpose(0, 1, 2, 4, 3).reshape(DEPTH, 2, N_PAIR, PAIR_ST, SSM_GROUP)

    d_vec = jnp.broadcast_to(ssm_d.reshape(DEPTH, N_PAIR, 1, PAIR_CH), (DEPTH, N_PAIR, CHUNK, PAIR_CH))
    d_vec = d_vec.reshape(DEPTH, N_PAIR, 1, PAIR_IN)
    w1, w2, a_chunk, bf_weights = _s5_matrices(
        prow, b_rows(ssm_b_re), b_rows(ssm_b_im), c_cols(ssm_c_re), c_cols(ssm_c_im), d_vec,
        first_f32=f32_weights)

    g3 = lambda g: g.reshape(DEPTH, 1, -1)
    g_pre1_, g_conv_out_, g_ssm_out_, g_post1_, g_pre2_, g_post2_, b_glu_ = map(
        g3, (g_pre1, g_conv_out, g_ssm_out, g_post1, g_pre2, g_post2, b_glu))

    xs = (x_prompt.reshape(N_CTX, D_MODEL), x_sample.reshape(N_LAT, D_MODEL))
    fins = []
    for l in range(DEPTH):
        last = l + 1 == DEPTH
        w_in_bf, w_glu_bf, w_out_bf, w_up_bf, w_down_bf = bf_weights
        res = _in_proj(xs, mod, g_pre1_, w_in_bf, conv_w, g_conv_out_, l)
        cn, u = res[:2]
        x = res[2] if len(xs) == 2 else xs[0]
        sre = state_ssm_re[:, l].reshape(DEC_BATCH, 2, -1)
        sim = state_ssm_im[:, l].reshape(DEC_BATCH, 2, -1)
        h0 = jnp.stack([sre[:, 0], sim[:, 0], sre[:, 1], sim[:, 1]], axis=0)
        y, fin = _s5_mixer(u, w1, w2, a_chunk, h0, l)
        fins.append(fin)
        x1, h2 = _out_proj(x, cn, y.reshape(N_TOK, SSM_WIDTH), mod, w_glu_bf, b_glu_, g_ssm_out_,
                           w_out_bf, g_post1_, g_pre2_, l)
        x, bf_weights = _conv_ffn(h2, x1, mod, w_up_bf, ffn_conv_w, w_down_bf, g_post2_, l,
                                  next_f32=() if last else f32_weights, last=last)
        xs = (x,)

    fin = jnp.stack(fins, axis=0).reshape(DEPTH, 2, 2, BATCH, SSM_GROUPS, SSM_STATE)
    new_re = fin[:, :, 0].transpose(2, 0, 1, 3, 4)
    new_im = fin[:, :, 1].transpose(2, 0, 1, 3, 4)
    y_prompt, y_sample = x
    return (y_prompt.reshape(BATCH, SEQ, D_MODEL), y_sample.reshape(DEC_BATCH, DEC_SEQ, D_MODEL),
            new_re, new_im)
```

```python
import functools

import jax
import jax.numpy as jnp
from jax import lax
from jax.experimental import pallas as pl
from jax.experimental.pallas import tpu as pltpu

D_MODEL = 2048
BATCH = 16
SEQ = 256
DEPTH = 4
DEC_BATCH = 2
DEC_SEQ = 1024
GRID_W = 64
CONV_WIDTH = 1024
HEAD_DIM = 64
SSM_WIDTH = 1024
SSM_GROUP = 16
SSM_GROUPS = 64
SSM_STATE = 64
D_FF = 5632
EPS = 1e-6

LANES = 128
N_CTX = BATCH * SEQ
N_LAT = DEC_BATCH * DEC_SEQ
N_TOK = N_CTX + N_LAT
CHUNK = 16
NC_CTX = SEQ // CHUNK
NC_LAT = DEC_SEQ // CHUNK
ROWS_CTX = BATCH * NC_CTX
ROWS_LAT = DEC_BATCH * NC_LAT
ROWS_S5 = ROWS_CTX + ROWS_LAT
N_PAIR = SSM_GROUPS // 2
PAIR_CH = 2 * SSM_GROUP
PAIR_IN = CHUNK * PAIR_CH
PAIR_ST = 2 * SSM_STATE
PAIRS_PER_STEP = LANES // PAIR_CH
STEPS_PER_LANE_BLOCK = LANES // PAIR_CH
PREP_PAIRS = 4

TM1 = 1024
TM1_SPLIT = 512
TC1 = 256
TM3 = 512
SUB3 = 256
TM4 = 512
TF4 = 512
SUB4 = 256
LAT_STRIP = 32

VMEM_LIMIT = 56 * 1024 * 1024

f32 = jnp.float32
bf16 = jnp.bfloat16


def _dot(a, b):
    return jnp.dot(a, b, preferred_element_type=f32)


def _split_bf16(x):
    hi = x.astype(bf16)
    lo = (x - hi.astype(f32)).astype(bf16)
    return hi, lo


def _dot3(a, b):
    ah, al = _split_bf16(a)
    bh, bl = _split_bf16(b)
    return _dot(ah, bh) + (_dot(ah, bl) + _dot(al, bh))


def _dot_select(a, sel):
    a1 = a.astype(bf16)
    r1 = a - a1.astype(f32)
    a2 = r1.astype(bf16)
    a3 = (r1 - a2.astype(f32)).astype(bf16)
    return _dot(a1, sel) + (_dot(a2, sel) + _dot(a3, sel))


def _rms_rows(x, g):
    ms = jnp.mean(x * x, axis=-1, keepdims=True)
    return x * lax.rsqrt(ms + EPS) * g


def _cmul(ar, ai, br, bi):
    return ar * br - ai * bi, ar * bi + ai * br


def _params(*sem):
    return pltpu.CompilerParams(dimension_semantics=sem, vmem_limit_bytes=VMEM_LIMIT)


def _mod_kernel(cv_ref, w_ref, b_ref, o_ref):
    cv = cv_ref[...]
    s = cv * jax.nn.sigmoid(cv)
    o_ref[...] = _dot(s.astype(bf16), w_ref[...].astype(bf16)) + b_ref[...]


def _modulation(cvec8, w_ada, b_ada):
    tn = 1024
    n_out = 6 * D_MODEL
    return pl.pallas_call(
        _mod_kernel,
        grid=(DEPTH, n_out // tn),
        in_specs=[
            pl.BlockSpec((8, D_MODEL), lambda l, n: (0, 0)),
            pl.BlockSpec((None, D_MODEL, tn), lambda l, n: (l, 0, n)),
            pl.BlockSpec((None, 1, tn), lambda l, n: (l, 0, n)),
        ],
        out_specs=pl.BlockSpec((None, 8, tn), lambda l, n: (l, 0, n)),
        out_shape=jax.ShapeDtypeStruct((DEPTH, 8, n_out), f32),
        compiler_params=_params("arbitrary", "arbitrary"),
        name="adaln_modulation",
    )(cvec8, w_ada, b_ada.reshape(DEPTH, 1, n_out))


def _disc_kernel(ar_ref, ai_ref, ldt_ref, abr_ref, abi_ref, bfr_ref, bfi_ref):
    ar = ar_ref[...]
    ai = ai_ref[...]
    dt = jnp.exp(ldt_ref[...])
    mag = jnp.exp(ar * dt)
    abr = mag * jnp.cos(ai * dt)
    abi = mag * jnp.sin(ai * dt)
    nr = abr - 1.0
    den = ar * ar + ai * ai
    abr_ref[...] = abr
    abi_ref[...] = abi
    bfr_ref[...] = (nr * ar + abi * ai) / den
    bfi_ref[...] = (abi * ar - nr * ai) / den


def _discretise(a_re, a_im, log_dt):
    shape = (DEPTH * 2, SSM_GROUPS * SSM_STATE)
    spec = pl.BlockSpec(shape, lambda: (0, 0))
    out = jax.ShapeDtypeStruct(shape, f32)
    return pl.pallas_call(
        _disc_kernel,
        in_specs=[spec, spec, spec],
        out_specs=[spec, spec, spec, spec],
        out_shape=[out, out, out, out],
        name="s5_discretise",
    )(a_re.reshape(shape), a_im.reshape(shape),
      jnp.broadcast_to(log_dt[..., None], (DEPTH, 2, SSM_GROUPS, SSM_STATE)).reshape(shape))


def _pow_table(br, bi, kk, shape, bits=4):
    tr = ti = None
    pr, pi = br, bi
    for bit in range(bits):
        sel = ((kk >> bit) & 1) == 1
        fr = jnp.broadcast_to(jnp.where(sel, pr, 1.0), shape)
        fi = jnp.broadcast_to(jnp.where(sel, pi, 0.0), shape)
        if tr is None:
            tr, ti = fr, fi
        else:
            tr, ti = _cmul(tr, ti, fr, fi)
        pr, pi = _cmul(pr, pi, pr, pi)
    return tr, ti


def _prep_kernel(pr_ref, btr_ref, bti_ref, ctr_ref, cti_ref, dv_ref, *rest):
    n_cast = (len(rest) - 3) // 2
    w1_ref, w2_ref, at_ref = rest[n_cast:n_cast + 3]
    for src, dst in zip(rest[:n_cast], rest[n_cast + 3:]):
        dst[...] = src[...].astype(bf16)
    for pp in range(PREP_PAIRS):
        _prep_pair(pr_ref.at[pp], btr_ref.at[:, pp], bti_ref.at[:, pp], ctr_ref.at[:, pp],
                   cti_ref.at[:, pp], dv_ref.at[pp], w1_ref.at[pp], w2_ref.at[pp],
                   at_ref.at[:, pl.ds(pp * PAIR_ST, PAIR_ST)])


def _prep_pair(pr_ref, btr_ref, bti_ref, ctr_ref, cti_ref, dv_ref, w1_ref, w2_ref, at_ref):
    lane_in = lax.broadcasted_iota(jnp.int32, (1, PAIR_IN), 1)
    step_of_lane = lane_in >> 5
    grp_of_lane = (lane_in >> 4) & 1
    chan_of_lane = lane_in & (SSM_GROUP - 1)
    lane_st = lax.broadcasted_iota(jnp.int32, (1, PAIR_ST), 1)
    grp_of_st_lane = lane_st >> 6
    row_st = lax.broadcasted_iota(jnp.int32, (PAIR_ST, 1), 0)
    grp_of_st_row = row_st >> 6
    on_diag_st = row_st == lane_st
    chan_row =lax.broadcasted_iota(jnp.int32, (SSM_GROUP, 1), 0)

    lane_blk = lax.broadcasted_iota(jnp.int32, (1, LANES), 1)
    step_in_blk = lane_blk >> 5
    spread_ch = jnp.where((lane_blk & (SSM_GROUP - 1)) == chan_row, 1.0, 0.0).astype(bf16)
    same_grp = grp_of_st_row == ((lane_blk >> 4) & 1)
    n_blk = PAIR_IN // LANES

    taps = []
    for d in range(2):
        abr = pr_ref[4 * d + 0:4 * d + 1, :]
        abi = pr_ref[4 * d + 1:4 * d + 2, :]
        bfr = pr_ref[4 * d + 2:4 * d + 3, :]
        bfi = pr_ref[4 * d + 3:4 * d + 4, :]
        a2 = _cmul(abr, abi, abr, abi)
        a4 = _cmul(*a2, *a2)
        a8 = _cmul(*a4, *a4)
        a16 = _cmul(*a8, *a8)
        at_ref[2 * d:2 * d + 1, :] = a16[0]
        at_ref[2 * d + 1:2 * d + 2, :] = a16[1]
        bbr, bbi = _cmul(bfr, bfi, btr_ref[d], bti_ref[d])
        kk_rows = (CHUNK - 1 - chan_row) if d == 0 else chan_row
        pcr, pci = _pow_table(abr, abi, kk_rows, (CHUNK, PAIR_ST))
        pbr = jnp.concatenate([jnp.broadcast_to(pcr[j:j + 1, :], (PAIR_CH, PAIR_ST))
                               for j in range(CHUNK)], axis=0)
        pbi = jnp.concatenate([jnp.broadcast_to(pci[j:j + 1, :], (PAIR_CH, PAIR_ST))
                               for j in range(CHUNK)], axis=0)
        slab_r = jnp.concatenate([jnp.where(grp_of_st_lane == e, bbr, 0.0) for e in range(2)], axis=0)
        slab_i = jnp.concatenate([jnp.where(grp_of_st_lane == e, bbi, 0.0) for e in range(2)], axis=0)
        tile_r = jnp.concatenate([slab_r] * CHUNK, axis=0)
        tile_i = jnp.concatenate([slab_i] * CHUNK, axis=0)
        wbr, wbi = _cmul(pbr, pbi, tile_r, tile_i)
        for part, wb in ((0, wbr), (1, wbi)):
            c0 = PAIR_IN + (2 * d + part) * PAIR_ST
            w1_ref[:, c0:c0 + PAIR_ST] = wb.astype(bf16)

        acr = jnp.sum(jnp.where(on_diag_st, abr, 0.0), axis=1, keepdims=True)
        aci = jnp.sum(jnp.where(on_diag_st, abi, 0.0), axis=1, keepdims=True)
        ctr = jnp.where(same_grp, _dot_select(ctr_ref[d], spread_ch), 0.0)
        cti = jnp.where(same_grp, _dot_select(cti_ref[d], spread_ch), 0.0)
        kk_blk = (step_in_blk + 1) if d == 0 else (STEPS_PER_LANE_BLOCK - step_in_blk)
        m0r, m0i = _cmul(*_pow_table(acr, aci, kk_blk, (PAIR_ST, LANES), bits=3), ctr, cti)
        c2 = _cmul(acr, aci, acr, aci)
        c4 = _cmul(*c2, *c2)
        c8 = _cmul(*c4, *c4)
        c12 = _cmul(*c8, *c4)
        blocks = [(m0r, m0i)] + [_cmul(*cm, m0r, m0i) for cm in (c4, c8, c12)]
        if d == 1:
            blocks = blocks[::-1]
        ca1r = jnp.concatenate([b[0] for b in blocks], axis=1)
        ca1i = jnp.concatenate([b[1] for b in blocks], axis=1)
        ctr4 = jnp.concatenate([ctr] * n_blk, axis=1)
        cti4 = jnp.concatenate([cti] * n_blk, axis=1)
        if d == 0:
            ca0r = jnp.where(lane_in < PAIR_CH, ctr4, pltpu.roll(ca1r, PAIR_CH, axis=1))
            ca0i = jnp.where(lane_in < PAIR_CH, cti4, pltpu.roll(ca1i, PAIR_CH, axis=1))
        else:
            ca0r = jnp.where(lane_in >= PAIR_IN - PAIR_CH, ctr4, pltpu.roll(ca1r, PAIR_IN - PAIR_CH, axis=1))
            ca0i = jnp.where(lane_in >= PAIR_IN - PAIR_CH, cti4, pltpu.roll(ca1i, PAIR_IN - PAIR_CH, axis=1))
        w2_ref[(2 * d) * PAIR_ST:(2 * d + 1) * PAIR_ST, :] = ca1r.astype(bf16)
        w2_ref[(2 * d + 1) * PAIR_ST:(2 * d + 2) * PAIR_ST, :] = (-ca1i).astype(bf16)
        taps.append(_dot3(bbr, ca0r) - _dot3(bbi, ca0i))

    gf, gb = taps
    dvec = dv_ref[...]
    for jp in range(CHUNK):
        lo = PAIR_CH * jp
        hi = PAIR_CH * (jp + 1)
        rf = gf if jp == 0 else pltpu.roll(gf, lo, axis=1)
        rb = gb if jp == CHUNK - 1 else pltpu.roll(gb, hi, axis=1)
        blk = jnp.where(lane_in >= lo, rf, 0.0) + jnp.where(lane_in < hi, rb, 0.0)
        on_diag = (step_of_lane == jp) & (chan_of_lane == chan_row)
        blk = blk + jnp.where(on_diag, dvec, 0.0)
        for e in range(2):
            r0 = lo + e * SSM_GROUP
            w1_ref[r0:r0 + SSM_GROUP, 0:PAIR_IN] = jnp.where(grp_of_lane == e, blk, 0.0).astype(bf16)


def _s5_matrices(prow, bt_re, bt_im, ct_re, ct_im, d_vec, first_f32=()):
    n_q = N_PAIR // PREP_PAIRS
    n_steps = DEPTH * n_q
    cast_in, cast_out, cast_shapes = [], [], []
    for w in first_f32:
        _, rows, cols = w.shape
        n_blk = next(n for n in (n_steps, n_steps // 2, n_steps // 4)
                     if rows % n == 0 and (rows // n) % 16 == 0)
        idx = lambda l, q, n_blk=n_blk: (jnp.minimum(l * n_q + q, n_blk - 1), 0)
        cast_in.append(pl.BlockSpec((None, rows // n_blk, cols), lambda l, q, idx=idx: (0,) + idx(l, q)))
        cast_out.append(pl.BlockSpec((rows // n_blk, cols), idx))
        cast_shapes.append(jax.ShapeDtypeStruct((rows, cols), bf16))
    pp = PREP_PAIRS
    res = pl.pallas_call(
        _prep_kernel,
        grid=(DEPTH, n_q),
        in_specs=[
            pl.BlockSpec((None, pp, 8, PAIR_ST), lambda l, q: (l, q, 0, 0)),
            pl.BlockSpec((None, 2, pp, SSM_GROUP, PAIR_ST), lambda l, q: (l, 0, q, 0, 0)),
            pl.BlockSpec((None, 2, pp, SSM_GROUP, PAIR_ST), lambda l, q: (l, 0, q, 0, 0)),
            pl.BlockSpec((None, 2, pp, PAIR_ST, SSM_GROUP), lambda l, q: (l, 0, q, 0, 0)),
            pl.BlockSpec((None, 2, pp, PAIR_ST, SSM_GROUP), lambda l, q: (l, 0, q, 0, 0)),
            pl.BlockSpec((None, pp, 1, PAIR_IN), lambda l, q: (l, q, 0, 0)),
        ] + cast_in,
        out_specs=[
            pl.BlockSpec((None, pp, PAIR_IN, PAIR_IN + 4 * PAIR_ST), lambda l, q: (l, q, 0, 0)),
            pl.BlockSpec((None, pp, 4 * PAIR_ST, PAIR_IN), lambda l, q: (l, q, 0, 0)),
            pl.BlockSpec((None, 4, pp * PAIR_ST), lambda l, q: (l, 0, q)),
        ] + cast_out,
        out_shape=[
            jax.ShapeDtypeStruct((DEPTH, N_PAIR, PAIR_IN, PAIR_IN + 4 * PAIR_ST), bf16),
            jax.ShapeDtypeStruct((DEPTH, N_PAIR, 4 * PAIR_ST, PAIR_IN), bf16),
            jax.ShapeDtypeStruct((DEPTH, 4, SSM_GROUPS * SSM_STATE), f32),
        ] + cast_shapes,
        compiler_params=_params("arbitrary", "arbitrary"),
        name="s5_chunk_matrices",
    )(prow, bt_re, bt_im, ct_re, ct_im, d_vec, *first_f32)
    return res[0], res[1], res[2], tuple(res[3:])


def _chunk_scan(pr, pi, pows, n_rows, seq, reverse):
    pos = lax.broadcasted_iota(jnp.int32, (n_rows, 1), 0) & (seq - 1)
    shift, k = 1, 0
    while shift < seq:
        ar, ai = pows[k]
        if reverse:
            valid = pos < seq - shift
            sr = pltpu.roll(pr, n_rows - shift, axis=0)
            si = pltpu.roll(pi, n_rows - shift, axis=0)
        else:
            valid = pos >= shift
            sr = pltpu.roll(pr, shift, axis=0)
            si = pltpu.roll(pi, shift, axis=0)
        tr, ti = _cmul(ar, ai, sr, si)
        pr = pr + jnp.where(valid, tr, 0.0)
        pi = pi + jnp.where(valid, ti, 0.0)
        shift, k = 2 * shift, k + 1
    return pr, pi


def _s5_kernel(*refs):
    a_refs = refs[0:CHUNK]
    w1_ref, w2_ref, at_ref, h0_ref = refs[CHUNK:CHUNK + 4]
    y_ref, fin_ref = refs[CHUNK + 4:CHUNK + 6]
    r_scr, sp_scr, y_scr, sr_scr, si_scr = refs[CHUNK + 6:]

    lane_blk = lax.broadcasted_iota(jnp.int32, (1, LANES), 1) // PAIR_CH
    row_lat = lax.broadcasted_iota(jnp.int32, (ROWS_LAT, 1), 0)
    pos_ctx = lax.broadcasted_iota(jnp.int32, (ROWS_CTX, 1), 0) & (NC_CTX - 1)
    pos_lat = row_lat & (NC_LAT - 1)

    for p in range(PAIRS_PER_STEP):
        pieces = []
        for k in range(CHUNK // STEPS_PER_LANE_BLOCK):
            piece = None
            for jj in range(STEPS_PER_LANE_BLOCK):
                src = a_refs[STEPS_PER_LANE_BLOCK * k + jj][...]
                sh = (PAIR_CH * (jj - p)) % LANES
                rolled = src if sh == 0 else pltpu.roll(src, sh, axis=1)
                piece = rolled if piece is None else jnp.where(lane_blk == jj, rolled, piece)
            pieces.append(piece.astype(bf16))
        u_pair = jnp.concatenate(pieces, axis=1)
        r_scr[...] = _dot(u_pair, w1_ref[p])

        lanes_p = slice(p * PAIR_ST, (p + 1) * PAIR_ST)
        for d in range(2):
            pows = [(at_ref[2 * d:2 * d + 1, lanes_p], at_ref[2 * d + 1:2 * d + 2, lanes_p])]
            for _ in range(5):
                pows.append(_cmul(*pows[-1], *pows[-1]))
            c_re = PAIR_IN + 2 * d * PAIR_ST
            c_im = c_re + PAIR_ST
            reverse = d == 1

            sr, si = _chunk_scan(r_scr[0:ROWS_CTX, c_re:c_re + PAIR_ST],
                                 r_scr[0:ROWS_CTX, c_im:c_im + PAIR_ST],
                                 pows, ROWS_CTX, NC_CTX, reverse)
            sr_scr[...] = sr
            si_scr[...] = si
            last = 0 if reverse else NC_CTX - 1
            fin_ref[2 * d, :, lanes_p] = sr_scr[pl.ds(last, BATCH, stride=NC_CTX), :]
            fin_ref[2 * d + 1, :, lanes_p] = si_scr[pl.ds(last, BATCH, stride=NC_CTX), :]
            edge = pos_ctx == (NC_CTX - 1 if reverse else 0)
            back = ROWS_CTX - 1 if reverse else 1
            sp_scr[0:ROWS_CTX, 2 * d * PAIR_ST:(2 * d + 1) * PAIR_ST] = jnp.where(
                edge, 0.0, pltpu.roll(sr, back, axis=0))
            sp_scr[0:ROWS_CTX, (2 * d + 1) * PAIR_ST:(2 * d + 2) * PAIR_ST] = jnp.where(
                edge, 0.0, pltpu.roll(si, back, axis=0))

            h0r = jnp.where(row_lat < NC_LAT, h0_ref[2 * d, 0:1, lanes_p], h0_ref[2 * d, 1:2, lanes_p])
            h0i = jnp.where(row_lat < NC_LAT, h0_ref[2 * d + 1, 0:1, lanes_p],
                            h0_ref[2 * d + 1, 1:2, lanes_p])
            edge = pos_lat == (NC_LAT - 1 if reverse else 0)
            ahr, ahi = _cmul(*pows[0], h0r, h0i)
            pr = r_scr[ROWS_CTX:ROWS_S5, c_re:c_re + PAIR_ST] + jnp.where(edge, ahr, 0.0)
            pi = r_scr[ROWS_CTX:ROWS_S5, c_im:c_im + PAIR_ST] + jnp.where(edge, ahi, 0.0)
            sr, si = _chunk_scan(pr, pi, pows, ROWS_LAT, NC_LAT, reverse)
            back = ROWS_LAT - 1 if reverse else 1
            sp_scr[ROWS_CTX:ROWS_S5, 2 * d * PAIR_ST:(2 * d + 1) * PAIR_ST] = jnp.where(
                edge, h0r, pltpu.roll(sr, back, axis=0))
            sp_scr[ROWS_CTX:ROWS_S5, (2 * d + 1) * PAIR_ST:(2 * d + 2) * PAIR_ST] = jnp.where(
                edge, h0i, pltpu.roll(si, back, axis=0))

        y_scr[:, p * PAIR_IN:(p + 1) * PAIR_IN] = (
            r_scr[:, 0:PAIR_IN] + _dot(sp_scr[...].astype(bf16), w2_ref[p]))

    for j in range(CHUNK):
        k, jj = divmod(j, STEPS_PER_LANE_BLOCK)
        out = None
        for p in range(PAIRS_PER_STEP):
            src = y_scr[:, p * PAIR_IN + k * LANES:p * PAIR_IN + (k + 1) * LANES]
            sh = (PAIR_CH * (p - jj)) % LANES
            rolled = src if sh == 0 else pltpu.roll(src, sh, axis=1)
            out = rolled if out is None else jnp.where(lane_blk == p, rolled, out)
        y_ref[:, j, :] = out


def _s5_mixer(u_rows, w1, w2, a_chunk, h0, layer):
    n_state = SSM_GROUPS * SSM_STATE
    n_blk = SSM_WIDTH // LANES
    step_spec = lambda j: pl.BlockSpec((None, ROWS_S5, LANES), lambda g, j=j: (j, 0, g))
    w_pairs = lambda rows, cols: pl.BlockSpec((None, PAIRS_PER_STEP, rows, cols),
                                              lambda g: (layer, g, 0, 0))
    st_lanes = PAIRS_PER_STEP * PAIR_ST
    return pl.pallas_call(
        _s5_kernel,
        grid=(n_blk,),
        in_specs=[step_spec(j) for j in range(CHUNK)] + [
            w_pairs(PAIR_IN, PAIR_IN + 4 * PAIR_ST),
            w_pairs(4 * PAIR_ST, PAIR_IN),
            pl.BlockSpec((None, 4, st_lanes), lambda g: (layer, 0, g)),
            pl.BlockSpec((4, DEC_BATCH, st_lanes), lambda g: (0, 0, g)),
        ],
        out_specs=[
            pl.BlockSpec((ROWS_S5, CHUNK, LANES), lambda g: (0, 0, g)),
            pl.BlockSpec((4, BATCH, st_lanes), lambda g: (0, 0, g)),
        ],
        out_shape=[
            jax.ShapeDtypeStruct((ROWS_S5, CHUNK, SSM_WIDTH), f32),
            jax.ShapeDtypeStruct((4, BATCH, n_state), f32),
        ],
        scratch_shapes=[
            pltpu.VMEM((ROWS_S5, PAIR_IN + 4 * PAIR_ST), f32),
            pltpu.VMEM((ROWS_S5, 4 * PAIR_ST), f32),
            pltpu.VMEM((ROWS_S5, PAIRS_PER_STEP * PAIR_IN), f32),
            pltpu.VMEM((ROWS_CTX, PAIR_ST), f32),
            pltpu.VMEM((ROWS_CTX, PAIR_ST), f32),
        ],
        compiler_params=_params("arbitrary"),
        name="s5_chunked_mixer",
    )(*([u_rows] * CHUNK), w1, w2, a_chunk, h0)


def _shift_rows(v, period, n_rows, shift):
    row = lax.broadcasted_iota(jnp.int32, (n_rows, 1), 0)
    pos = row & (period - 1)
    prev = jnp.where(pos < shift, 0.0, pltpu.roll(v, shift, axis=0))
    nxt = jnp.where(pos >= period - shift, 0.0, pltpu.roll(v, n_rows - shift, axis=0))
    return prev, nxt


def _dwconv3_rows(v, w_ref, period, n_rows, shift=1):
    prev, nxt = _shift_rows(v, period, n_rows, shift)
    return prev * w_ref[0:1, :] + v * w_ref[1:2, :] + nxt * w_ref[2:3, :]


def _dwconv3_rows_static(v, w, period, n_rows, shift):
    w0, w1, w2 = w[0:1, :], w[1:2, :], w[2:3, :]
    if shift % 8 == 0 and period == n_rows:
        zeros = jnp.zeros((shift, v.shape[1]), v.dtype)
        prev = jnp.concatenate([zeros, v[:n_rows - shift]], axis=0)
        nxt = jnp.concatenate([v[shift:], zeros], axis=0)
        return prev * w0 + v * w1 + nxt * w2
    prev = pltpu.roll(v, shift, axis=0)
    nxt = pltpu.roll(v, n_rows - shift, axis=0)
    out = prev * w0 + v * w1 + nxt * w2
    sub = lax.broadcasted_iota(jnp.int32, (8, 1), 0)
    pieces = []
    for lo in range(0, n_rows, period):
        hi = lo + period
        first = (jnp.where(sub < shift, 0.0, prev[lo:lo + 8]) * w0 + v[lo:lo + 8] * w1
                 + nxt[lo:lo + 8] * w2)
        last = (prev[hi - 8:hi] * w0 + v[hi - 8:hi] * w1
                + jnp.where(sub >= 8 - shift, 0.0, nxt[hi - 8:hi]) * w2)
        pieces += [first, out[lo + 8:hi - 8], last]
    return jnp.concatenate(pieces, axis=0)


def _k1_kernel(*refs, split_input):
    if split_input:
        xp_ref, xs_ref = refs[:2]
        refs = refs[2:]
    else:
        x_ref = refs[0]
        refs = refs[1:]
    mod_ref, gpre_ref, wgb_ref, wgc_ref, whv_ref, wu_ref, cw_ref, gco_ref, cn_ref, u_ref = refs[:10]
    if split_input:
        x_ref, h_scr, u_scr = refs[10:]
    else:
        h_scr, u_scr = refs[10:]
    i = pl.program_id(0)
    n = pl.program_id(1)
    tm = cn_ref.shape[0]
    n_ctx_tiles = N_CTX // tm

    @pl.when(n == 0)
    def _():
        if split_input:
            @pl.when(i < n_ctx_tiles)
            def _():
                x_ref[...] = xp_ref[...]

            @pl.when(i >= n_ctx_tiles)
            def _():
                x_ref[...] = xs_ref[...]

        r = jnp.where(i < n_ctx_tiles, 0, 1 + (i - n_ctx_tiles) // (DEC_SEQ // tm))
        sh1 = mod_ref[pl.ds(r, 1), 0:D_MODEL]
        sc1 = mod_ref[pl.ds(r, 1), D_MODEL:2 * D_MODEL]
        h_scr[...] = (_rms_rows(x_ref[...], gpre_ref[...] * (1.0 + sc1)) + sh1).astype(bf16)

    h = h_scr[...]
    u = _dot(h, wu_ref[...])
    for t in range(TC1 // LANES):
        u_scr[t] = u[:, t * LANES:(t + 1) * LANES]
    for j in range(CHUNK):
        for t in range(TC1 // LANES):
            u_ref[j, :, t * LANES:(t + 1) * LANES] = u_scr[t, pl.ds(j, tm // CHUNK, stride=CHUNK), :]
    v = _dot(h, wgc_ref[...]) * _dot(h, whv_ref[...])
    period = jnp.where(i < n_ctx_tiles, SEQ, GRID_W)
    v = _dwconv3_rows(v, cw_ref, period, tm)
    co = _dot(h, wgb_ref[...]) * v
    rr = lax.broadcasted_iota(jnp.int32, (TC1, TC1), 0) // HEAD_DIM
    cc = lax.broadcasted_iota(jnp.int32, (TC1, TC1), 1) // HEAD_DIM
    avg = jnp.where(rr == cc, 1.0 / HEAD_DIM, 0.0).astype(bf16)
    hi, lo = _split_bf16(co * co)
    ms = _dot(hi, avg) + _dot(lo, avg)
    cn_ref[...] = (co * lax.rsqrt(ms + EPS) * gco_ref[...]).astype(bf16)


def _in_proj(xs, mod, g_pre1, w_in_bf, conv_w, g_conv_out, layer):
    nb = CONV_WIDTH // TC1
    split_input = len(xs) == 2
    tm = TM1_SPLIT if split_input else TM1
    n_ctx_tiles = N_CTX // tm
    wspec = lambda off: pl.BlockSpec((D_MODEL, TC1), lambda i, n: (0, off * nb + n))
    x_spec = pl.BlockSpec((tm, D_MODEL), lambda i, n: (i, 0))
    if split_input:
        x_specs = [pl.BlockSpec((tm, D_MODEL), lambda i, n: (jnp.minimum(i, n_ctx_tiles - 1), 0)),
                   pl.BlockSpec((tm, D_MODEL), lambda i, n: (jnp.maximum(i - n_ctx_tiles, 0), 0))]
    else:
        x_specs = [x_spec]
    return pl.pallas_call(
        functools.partial(_k1_kernel, split_input=split_input),
        grid=(N_TOK // tm, nb),
        in_specs=x_specs + [
            pl.BlockSpec((None, 8, 6 * D_MODEL), lambda i, n: (layer, 0, 0)),
            pl.BlockSpec((None, 1, D_MODEL), lambda i, n: (layer, 0, 0)),
            wspec(0), wspec(1), wspec(2), wspec(3),
            pl.BlockSpec((None, 3, TC1), lambda i, n: (layer, 0, n)),
            pl.BlockSpec((None, 1, TC1), lambda i, n: (layer, 0, n)),
        ],
        out_specs=[
            pl.BlockSpec((tm, TC1), lambda i, n: (i, n)),
            pl.BlockSpec((CHUNK, tm // CHUNK, TC1), lambda i, n: (0, i, n)),
        ] + ([x_spec] if split_input else []),
        out_shape=[
            jax.ShapeDtypeStruct((N_TOK, CONV_WIDTH), bf16),
            jax.ShapeDtypeStruct((CHUNK, ROWS_S5, SSM_WIDTH), f32),
        ] + ([jax.ShapeDtypeStruct((N_TOK, D_MODEL), f32)] if split_input else []),
        scratch_shapes=[pltpu.VMEM((tm, D_MODEL), bf16),
                        pltpu.VMEM((TC1 // LANES, tm, LANES), f32)],
        compiler_params=_params("arbitrary", "arbitrary"),
        name="in_proj_conv_mixer",
    )(*xs, mod, g_pre1, w_in_bf, w_in_bf, w_in_bf, w_in_bf, conv_w, g_conv_out)


def _k3_kernel(x_ref, cn_ref, y_ref, mod_ref, wglu_ref, bglu_ref, gsso_ref, wout_ref,
               gpost1_ref, gpre2_ref, x1_ref, h2_ref):
    i = pl.program_id(0)
    n_ctx_tiles = N_CTX // TM3
    r = jnp.where(i < n_ctx_tiles, 0, 1 + (i - n_ctx_tiles) // (DEC_SEQ // TM3))
    gt1 = mod_ref[pl.ds(r, 1), 2 * D_MODEL:3 * D_MODEL]
    sh2 = mod_ref[pl.ds(r, 1), 3 * D_MODEL:4 * D_MODEL]
    sc2 = mod_ref[pl.ds(r, 1), 4 * D_MODEL:5 * D_MODEL]

    gain1 = gt1 * gpost1_ref[...]
    gain2 = gpre2_ref[...] * (1.0 + sc2)
    for s in range(TM3 // SUB3):
        rows = slice(s * SUB3, (s + 1) * SUB3)
        z = jax.nn.gelu(y_ref[rows, :], approximate=True)
        gate = jax.nn.sigmoid(_dot(z.astype(bf16), wglu_ref[...]) + bglu_ref[...])
        sn = _rms_rows(z * gate, gsso_ref[...])
        mixed = (_dot(cn_ref[rows, :], wout_ref[0:CONV_WIDTH, :])
                 + _dot(sn.astype(bf16), wout_ref[CONV_WIDTH:CONV_WIDTH + SSM_WIDTH, :]))
        x1 = x_ref[rows, :] + _rms_rows(mixed, gain1)
        x1_ref[rows, :] = x1
        h2_ref[rows, :] = (_rms_rows(x1, gain2) + sh2).astype(bf16)


def _out_proj(x, cn, y, mod, w_glu_bf, b_glu, g_ssm_out, w_out_bf, g_post1, g_pre2, layer):
    vec = lambda width: pl.BlockSpec((None, 1, width), lambda i: (layer, 0, 0))
    return pl.pallas_call(
        _k3_kernel,
        grid=(N_TOK // TM3,),
        in_specs=[
            pl.BlockSpec((TM3, D_MODEL), lambda i: (i, 0)),
            pl.BlockSpec((TM3, CONV_WIDTH), lambda i: (i, 0)),
            pl.BlockSpec((TM3, SSM_WIDTH), lambda i: (i, 0)),
            pl.BlockSpec((None, 8, 6 * D_MODEL), lambda i: (layer, 0, 0)),
            pl.BlockSpec((SSM_WIDTH, SSM_WIDTH), lambda i: (0, 0)),
            vec(SSM_WIDTH), vec(SSM_WIDTH),
            pl.BlockSpec((CONV_WIDTH + SSM_WIDTH, D_MODEL), lambda i: (0, 0)),
            vec(D_MODEL), vec(D_MODEL),
        ],
        out_specs=[
            pl.BlockSpec((TM3, D_MODEL), lambda i: (i, 0)),
            pl.BlockSpec((TM3, D_MODEL), lambda i: (i, 0)),
        ],
        out_shape=[
            jax.ShapeDtypeStruct((N_TOK, D_MODEL), f32),
            jax.ShapeDtypeStruct((N_TOK, D_MODEL), bf16),
        ],
        compiler_params=_params("arbitrary"),
        name="glu_out_proj",
    )(x, cn, y, mod, w_glu_bf, b_glu, g_ssm_out, w_out_bf, g_post1, g_pre2)


def _k4_kernel(h_ref, x1_ref, mod_ref, wa_ref, wg_ref, cwa_ref, cwg_ref, wd_ref, gpost2_ref,
               *rest, latent):
    n_cast = (len(rest) - 1) // 2
    o_ref = rest[n_cast]
    for src, dst in zip(rest[:n_cast], rest[n_cast + 1:]):
        dst[...] = src[...].astype(bf16)

    i = pl.program_id(0)
    j = pl.program_id(1)
    blk = o_ref.shape

    @pl.when(j == 0)
    def _():
        o_ref[...] = jnp.zeros(blk, f32)

    period, shift = (TM4, LAT_STRIP) if latent else (SEQ, 1)
    h = h_ref[...].reshape(TM4, D_MODEL)
    acts = []
    for sub in range(TF4 // SUB4):
        cols = slice(sub * SUB4, (sub + 1) * SUB4)
        a = _dwconv3_rows_static(_dot(h, wa_ref[:, cols]), cwa_ref[:, cols], period, TM4, shift)
        g = _dwconv3_rows_static(_dot(h, wg_ref[:, cols]), cwg_ref[:, cols], period, TM4, shift)
        acts.append((g * jax.nn.sigmoid(g) * a).astype(bf16))
    o_ref[...] += _dot(jnp.concatenate(acts, axis=1), wd_ref[...]).reshape(blk)

    @pl.when(j == pl.num_programs(1) - 1)
    def _():
        r = (1 + i // (GRID_W // LAT_STRIP)) if latent else 0
        gt2 = mod_ref[pl.ds(r, 1), 5 * D_MODEL:6 * D_MODEL]
        ff = o_ref[...].reshape(TM4, D_MODEL)
        x2 = x1_ref[...].reshape(TM4, D_MODEL) + _rms_rows(ff, gt2 * gpost2_ref[...])
        o_ref[...] = x2.reshape(blk)


def _conv_ffn(h2, x1, mod, w_up_bf, ffn_conv_w, w_down_bf, g_post2, layer, next_f32=(), last=False):
    nj = D_FF // TF4
    n_ctx_tiles = N_CTX // TM4

    weight_specs = [
        pl.BlockSpec((None, 8, 6 * D_MODEL), lambda i, j: (layer, 0, 0)),
        pl.BlockSpec((D_MODEL, TF4), lambda i, j: (0, j)),
        pl.BlockSpec((D_MODEL, TF4), lambda i, j: (0, nj + j)),
        pl.BlockSpec((None, 3, TF4), lambda i, j: (layer, 0, j)),
        pl.BlockSpec((None, 3, TF4), lambda i, j: (layer, 0, nj + j)),
        pl.BlockSpec((TF4, D_MODEL), lambda i, j: (j, 0)),
        pl.BlockSpec((None, 1, D_MODEL), lambda i, j: (layer, 0, 0)),
    ]
    weights = (mod, w_up_bf, w_up_bf, ffn_conv_w, ffn_conv_w, w_down_bf, g_post2)

    cast_in, cast_out, cast_shapes = [], [], []
    for w in next_f32:
        _, rows, cols = w.shape
        if cols % nj == 0:
            blk = (rows // n_ctx_tiles, cols // nj)
            idx = lambda i, j: (i, j)
        elif rows % nj == 0:
            blk = (rows // nj, cols // n_ctx_tiles)
            idx = lambda i, j: (j, i)
        else:
            blk = (rows // n_ctx_tiles, cols // n_ctx_tiles)
            idx = lambda i, j: (i, jnp.minimum(j, n_ctx_tiles - 1))
        cast_in.append(pl.BlockSpec((None,) + blk, lambda i, j, idx=idx: (layer + 1,) + idx(i, j)))
        cast_out.append(pl.BlockSpec(blk, idx))
        cast_shapes.append(jax.ShapeDtypeStruct((rows, cols), bf16))
    row_spec = pl.BlockSpec((TM4, D_MODEL), lambda i, j: (i, 0))
    res = pl.pallas_call(
        functools.partial(_k4_kernel, latent=False),
        grid=(n_ctx_tiles, nj),
        in_specs=[row_spec, row_spec] + weight_specs + cast_in,
        out_specs=[row_spec] + cast_out,
        out_shape=[jax.ShapeDtypeStruct((N_CTX if last else N_TOK, D_MODEL), f32)] + cast_shapes,
        input_output_aliases={} if last else {1: 0},
        compiler_params=_params("arbitrary", "arbitrary"),
        name="conv_ffn_ctx",
    )(h2, x1, *weights, *next_f32)

    view = lambda t: t.reshape(-1, GRID_W, D_MODEL)
    rows_per_img = DEC_SEQ // GRID_W
    strips = GRID_W // LAT_STRIP
    first_img = N_CTX // DEC_SEQ
    strip = (rows_per_img, LAT_STRIP, D_MODEL)
    strip_spec = lambda img0: pl.BlockSpec(strip, lambda i, j: (img0 + i // strips, i % strips, 0))
    x2_lat = pl.pallas_call(
        functools.partial(_k4_kernel, latent=True),
        grid=(DEC_BATCH * strips, nj),
        in_specs=[strip_spec(first_img), strip_spec(first_img)] + weight_specs,
        out_specs=strip_spec(0 if last else first_img),
        out_shape=jax.ShapeDtypeStruct(((N_LAT if last else N_TOK) // GRID_W, GRID_W, D_MODEL), f32),
        input_output_aliases={} if last else {1: 0},
        compiler_params=_params("arbitrary", "arbitrary"),
        name="conv_ffn_lat",
    )(view(h2), view(x1 if last else res[0]), *weights)
    if last:
        return (res[0], x2_lat.reshape(N_LAT, D_MODEL)), ()
    return x2_lat.reshape(N_TOK, D_MODEL), tuple(res[1:])


def kernel(x_prompt, x_sample, state_ssm_re, state_ssm_im, c, c_ctx, w_ada, b_ada, g_pre1, w_in, conv_w, ssm_a_re, ssm_a_im, ssm_log_dt, ssm_b_re, ssm_b_im, ssm_c_re, ssm_c_im, ssm_d, w_glu, b_glu, g_conv_out, g_ssm_out, w_out, g_post1, g_pre2, w_up, ffn_conv_w, w_down, g_post2):
    f32_weights = (w_in, w_glu, w_out, w_up, w_down)

    cvec8 = jnp.concatenate([c_ctx[None, :], c, jnp.zeros((8 - 1 - DEC_BATCH, D_MODEL), f32)], axis=0)
    mod = _modulation(cvec8, w_ada, b_ada)

    abr, abi, bfr, bfi = [t.reshape(DEPTH, 2, N_PAIR, PAIR_ST)
                          for t in _discretise(ssm_a_re, ssm_a_im, ssm_log_dt)]
    prow = jnp.stack([abr, abi, bfr, bfi], axis=2)
    prow = prow.transpose(0, 3, 1, 2, 4).reshape(DEPTH, N_PAIR, 8, PAIR_ST)

    def b_rows(b):
        b = b.transpose(0, 1, 2, 4, 3).reshape(DEPTH, 2, N_PAIR, 2, SSM_GROUP, SSM_STATE)
        return b.transpose(0, 1, 2, 4, 3, 5).reshape(DEPTH, 2, N_PAIR, SSM_GROUP, PAIR_ST)

    def c_cols(cm):
        return cm.transpose(0, 1, 2, 4, 3).reshape(DEPTH, 2, N_PAIR, PAIR_ST, SSM_GROUP)

    d_vec = jnp.broadcast_to(ssm_d.reshape(DEPTH, N_PAIR, 1, PAIR_CH), (DEPTH, N_PAIR, CHUNK, PAIR_CH))
    d_vec = d_vec.reshape(DEPTH, N_PAIR, 1, PAIR_IN)
    w1, w2, a_chunk, bf_weights = _s5_matrices(
        prow, b_rows(ssm_b_re), b_rows(ssm_b_im), c_cols(ssm_c_re), c_cols(ssm_c_im), d_vec,
        first_f32=f32_weights)

    g3 = lambda g: g.reshape(DEPTH, 1, -1)
    g_pre1_, g_conv_out_, g_ssm_out_, g_post1_, g_pre2_, g_post2_, b_glu_ = map(
        g3, (g_pre1, g_conv_out, g_ssm_out, g_post1, g_pre2, g_post2, b_glu))

    xs = (x_prompt.reshape(N_CTX, D_MODEL), x_sample.reshape(N_LAT, D_MODEL))
    fins = []
    for l in range(DEPTH):
        last = l + 1 == DEPTH
        w_in_bf, w_glu_bf, w_out_bf, w_up_bf, w_down_bf = bf_weights
        res = _in_proj(xs, mod, g_pre1_, w_in_bf, conv_w, g_conv_out_, l)
        cn, u = res[:2]
        x = res[2] if len(xs) == 2 else xs[0]
        sre = state_ssm_re[:, l].reshape(DEC_BATCH, 2, -1)
        sim = state_ssm_im[:, l].reshape(DEC_BATCH, 2, -1)
        h0 = jnp.stack([sre[:, 0], sim[:, 0], sre[:, 1], sim[:, 1]], axis=0)
        y, fin = _s5_mixer(u, w1, w2, a_chunk, h0, l)
        fins.append(fin)
        x1, h2 = _out_proj(x, cn, y.reshape(N_TOK, SSM_WIDTH), mod, w_glu_bf, b_glu_, g_ssm_out_,
                           w_out_bf, g_post1_, g_pre2_, l)
        x, bf_weights = _conv_ffn(h2, x1, mod, w_up_bf, ffn_conv_w, w_down_bf, g_post2_, l,
                                  next_f32=() if last else f32_weights, last=last)
        xs = (x,)

    fin = jnp.stack(fins, axis=0).reshape(DEPTH, 2, 2, BATCH, SSM_GROUPS, SSM_STATE)
    new_re = fin[:, :, 0].transpose(2, 0, 1, 3, 4)
    new_im = fin[:, :, 1].transpose(2, 0, 1, 3, 4)
    y_prompt, y_sample = x
    return (y_prompt.reshape(BATCH, SEQ, D_MODEL), y_sample.reshape(DEC_BATCH, DEC_SEQ, D_MODEL),
            new_re, new_im)
```

```python
import functools

import jax
import jax.numpy as jnp
from jax import lax
from jax.experimental import pallas as pl
from jax.experimental.pallas import tpu as pltpu

D_MODEL = 2048
BATCH = 16
SEQ = 256
DEPTH = 4
DEC_BATCH = 2
DEC_SEQ = 1024
GRID_W = 64
CONV_WIDTH = 1024
HEAD_DIM = 64
SSM_WIDTH = 1024
SSM_GROUP = 16
SSM_GROUPS = 64
SSM_STATE = 64
D_FF = 5632
EPS = 1e-6

LANES = 128
N_CTX = BATCH * SEQ
N_LAT = DEC_BATCH * DEC_SEQ
N_TOK = N_CTX + N_LAT
CHUNK = 16
NC_CTX = SEQ // CHUNK
NC_LAT = DEC_SEQ // CHUNK
ROWS_CTX = BATCH * NC_CTX
ROWS_LAT = DEC_BATCH * NC_LAT
ROWS_S5 = ROWS_CTX + ROWS_LAT
N_PAIR = SSM_GROUPS // 2
PAIR_CH = 2 * SSM_GROUP
PAIR_IN = CHUNK * PAIR_CH
PAIR_ST = 2 * SSM_STATE
PAIRS_PER_STEP = LANES // PAIR_CH
STEPS_PER_LANE_BLOCK = LANES // PAIR_CH
PREP_PAIRS = 4

TM1 = 1024
TM1_SPLIT = 512
TC1 = 256
TM3 = 512
SUB3 = 256
TM4 = 512
TF4 = 512
SUB4 = 256
LAT_STRIP = 32

VMEM_LIMIT = 56 * 1024 * 1024

f32 = jnp.float32
bf16 = jnp.bfloat16


def _dot(a, b):
    return jnp.dot(a, b, preferred_element_type=f32)


def _split_bf16(x):
    hi = x.astype(bf16)
    lo = (x - hi.astype(f32)).astype(bf16)
    return hi, lo


def _dot3(a, b):
    ah, al = _split_bf16(a)
    bh, bl = _split_bf16(b)
    return _dot(ah, bh) + (_dot(ah, bl) + _dot(al, bh))


def _dot_select(a, sel):
    a1 = a.astype(bf16)
    r1 = a - a1.astype(f32)
    a2 = r1.astype(bf16)
    a3 = (r1 - a2.astype(f32)).astype(bf16)
    return _dot(a1, sel) + (_dot(a2, sel) + _dot(a3, sel))


def _rms_rows(x, g):
    ms = jnp.mean(x * x, axis=-1, keepdims=True)
    return x * lax.rsqrt(ms + EPS) * g


def _cmul(ar, ai, br, bi):
    return ar * br - ai * bi, ar * bi + ai * br


def _params(*sem):
    return pltpu.CompilerParams(dimension_semantics=sem, vmem_limit_bytes=VMEM_LIMIT)


def _mod_kernel(cv_ref, w_ref, b_ref, o_ref):
    cv = cv_ref[...]
    s = cv * jax.nn.sigmoid(cv)
    o_ref[...] = _dot(s.astype(bf16), w_ref[...].astype(bf16)) + b_ref[...]


def _modulation(cvec8, w_ada, b_ada):
    tn = 1024
    n_out = 6 * D_MODEL
    return pl.pallas_call(
        _mod_kernel,
        grid=(DEPTH, n_out // tn),
        in_specs=[
            pl.BlockSpec((8, D_MODEL), lambda l, n: (0, 0)),
            pl.BlockSpec((None, D_MODEL, tn), lambda l, n: (l, 0, n)),
            pl.BlockSpec((None, 1, tn), lambda l, n: (l, 0, n)),
        ],
        out_specs=pl.BlockSpec((None, 8, tn), lambda l, n: (l, 0, n)),
        out_shape=jax.ShapeDtypeStruct((DEPTH, 8, n_out), f32),
        compiler_params=_params("arbitrary", "arbitrary"),
        name="adaln_modulation",
    )(cvec8, w_ada, b_ada.reshape(DEPTH, 1, n_out))


def _disc_kernel(ar_ref, ai_ref, ldt_ref, abr_ref, abi_ref, bfr_ref, bfi_ref):
    ar = ar_ref[...]
    ai = ai_ref[...]
    dt = jnp.exp(ldt_ref[...])
    mag = jnp.exp(ar * dt)
    abr = mag * jnp.cos(ai * dt)
    abi = mag * jnp.sin(ai * dt)
    nr = abr - 1.0
    den = ar * ar + ai * ai
    abr_ref[...] = abr
    abi_ref[...] = abi
    bfr_ref[...] = (nr * ar + abi * ai) / den
    bfi_ref[...] = (abi * ar - nr * ai) / den


def _discretise(a_re, a_im, log_dt):
    shape = (DEPTH * 2, SSM_GROUPS * SSM_STATE)
    spec = pl.BlockSpec(shape, lambda: (0, 0))
    out = jax.ShapeDtypeStruct(shape, f32)
    return pl.pallas_call(
        _disc_kernel,
        in_specs=[spec, spec, spec],
        out_specs=[spec, spec, spec, spec],
        out_shape=[out, out, out, out],
        name="s5_discretise",
    )(a_re.reshape(shape), a_im.reshape(shape),
      jnp.broadcast_to(log_dt[..., None], (DEPTH, 2, SSM_GROUPS, SSM_STATE)).reshape(shape))


def _pow_table(br, bi, kk, shape, bits=4):
    tr = ti = None
    pr, pi = br, bi
    for bit in range(bits):
        sel = ((kk >> bit) & 1) == 1
        fr = jnp.broadcast_to(jnp.where(sel, pr, 1.0), shape)
        fi = jnp.broadcast_to(jnp.where(sel, pi, 0.0), shape)
        if tr is None:
            tr, ti = fr, fi
        else:
            tr, ti = _cmul(tr, ti, fr, fi)
        pr, pi = _cmul(pr, pi, pr, pi)
    return tr, ti


def _prep_kernel(pr_ref, btr_ref, bti_ref, ctr_ref, cti_ref, dv_ref, *rest):
    n_cast = (len(rest) - 3) // 2
    w1_ref, w2_ref, at_ref = rest[n_cast:n_cast + 3]
    for src, dst in zip(rest[:n_cast], rest[n_cast + 3:]):
        dst[...] = src[...].astype(bf16)
    for pp in range(PREP_PAIRS):
        _prep_pair(pr_ref.at[pp], btr_ref.at[:, pp], bti_ref.at[:, pp], ctr_ref.at[:, pp],
                   cti_ref.at[:, pp], dv_ref.at[pp], w1_ref.at[pp], w2_ref.at[pp],
                   at_ref.at[:, pl.ds(pp * PAIR_ST, PAIR_ST)])


def _prep_pair(pr_ref, btr_ref, bti_ref, ctr_ref, cti_ref, dv_ref, w1_ref, w2_ref, at_ref):
    lane_in = lax.broadcasted_iota(jnp.int32, (1, PAIR_IN), 1)
    step_of_lane = lane_in >> 5
    grp_of_lane = (lane_in >> 4) & 1
    chan_of_lane = lane_in & (SSM_GROUP - 1)
    lane_st = lax.broadcasted_iota(jnp.int32, (1, PAIR_ST), 1)
    grp_of_st_lane = lane_st >> 6
    row_st = lax.broadcasted_iota(jnp.int32, (PAIR_ST, 1), 0)
    grp_of_st_row = row_st >> 6
    on_diag_st = row_st == lane_st
    chan_row =lax.broadcasted_iota(jnp.int32, (SSM_GROUP, 1), 0)

    lane_blk = lax.broadcasted_iota(jnp.int32, (1, LANES), 1)
    step_in_blk = lane_blk >> 5
    spread_ch = jnp.where((lane_blk & (SSM_GROUP - 1)) == chan_row, 1.0, 0.0).astype(bf16)
    same_grp = grp_of_st_row == ((lane_blk >> 4) & 1)
    n_blk = PAIR_IN // LANES

    taps = []
    for d in range(2):
        abr = pr_ref[4 * d + 0:4 * d + 1, :]
        abi = pr_ref[4 * d + 1:4 * d + 2, :]
        bfr = pr_ref[4 * d + 2:4 * d + 3, :]
        bfi = pr_ref[4 * d + 3:4 * d + 4, :]
        a2 = _cmul(abr, abi, abr, abi)
        a4 = _cmul(*a2, *a2)
        a8 = _cmul(*a4, *a4)
        a16 = _cmul(*a8, *a8)
        at_ref[2 * d:2 * d + 1, :] = a16[0]
        at_ref[2 * d + 1:2 * d + 2, :] = a16[1]
        bbr, bbi = _cmul(bfr, bfi, btr_ref[d], bti_ref[d])
        kk_rows = (CHUNK - 1 - chan_row) if d == 0 else chan_row
        pcr, pci = _pow_table(abr, abi, kk_rows, (CHUNK, PAIR_ST))
        pbr = jnp.concatenate([jnp.broadcast_to(pcr[j:j + 1, :], (PAIR_CH, PAIR_ST))
                               for j in range(CHUNK)], axis=0)
        pbi = jnp.concatenate([jnp.broadcast_to(pci[j:j + 1, :], (PAIR_CH, PAIR_ST))
                               for j in range(CHUNK)], axis=0)
        slab_r = jnp.concatenate([jnp.where(grp_of_st_lane == e, bbr, 0.0) for e in range(2)], axis=0)
        slab_i = jnp.concatenate([jnp.where(grp_of_st_lane == e, bbi, 0.0) for e in range(2)], axis=0)
        tile_r = jnp.concatenate([slab_r] * CHUNK, axis=0)
        tile_i = jnp.concatenate([slab_i] * CHUNK, axis=0)
        wbr, wbi = _cmul(pbr, pbi, tile_r, tile_i)
        for part, wb in ((0, wbr), (1, wbi)):
            c0 = PAIR_IN + (2 * d + part) * PAIR_ST
            w1_ref[:, c0:c0 + PAIR_ST] = wb.astype(bf16)

        acr = jnp.sum(jnp.where(on_diag_st, abr, 0.0), axis=1, keepdims=True)
        aci = jnp.sum(jnp.where(on_diag_st, abi, 0.0), axis=1, keepdims=True)
        ctr = jnp.where(same_grp, _dot_select(ctr_ref[d], spread_ch), 0.0)
        cti = jnp.where(same_grp, _dot_select(cti_ref[d], spread_ch), 0.0)
        kk_blk = (step_in_blk + 1) if d == 0 else (STEPS_PER_LANE_BLOCK - step_in_blk)
        m0r, m0i = _cmul(*_pow_table(acr, aci, kk_blk, (PAIR_ST, LANES), bits=3), ctr, cti)
        c2 = _cmul(acr, aci, acr, aci)
        c4 = _cmul(*c2, *c2)
        c8 = _cmul(*c4, *c4)
        c12 = _cmul(*c8, *c4)
        blocks = [(m0r, m0i)] + [_cmul(*cm, m0r, m0i) for cm in (c4, c8, c12)]
        if d == 1:
            blocks = blocks[::-1]
        ca1r = jnp.concatenate([b[0] for b in blocks], axis=1)
        ca1i = jnp.concatenate([b[1] for b in blocks], axis=1)
        ctr4 = jnp.concatenate([ctr] * n_blk, axis=1)
        cti4 = jnp.concatenate([cti] * n_blk, axis=1)
        if d == 0:
            ca0r = jnp.where(lane_in < PAIR_CH, ctr4, pltpu.roll(ca1r, PAIR_CH, axis=1))
            ca0i = jnp.where(lane_in < PAIR_CH, cti4, pltpu.roll(ca1i, PAIR_CH, axis=1))
        else:
            ca0r = jnp.where(lane_in >= PAIR_IN - PAIR_CH, ctr4, pltpu.roll(ca1r, PAIR_IN - PAIR_CH, axis=1))
            ca0i = jnp.where(lane_in >= PAIR_IN - PAIR_CH, cti4, pltpu.roll(ca1i, PAIR_IN - PAIR_CH, axis=1))
        w2_ref[(2 * d) * PAIR_ST:(2 * d + 1) * PAIR_ST, :] = ca1r.astype(bf16)
        w2_ref[(2 * d + 1) * PAIR_ST:(2 * d + 2) * PAIR_ST, :] = (-ca1i).astype(bf16)
        taps.append(_dot3(bbr, ca0r) - _dot3(bbi, ca0i))

    gf, gb = taps
    dvec = dv_ref[...]
    for jp in range(CHUNK):
        lo = PAIR_CH * jp
        hi = PAIR_CH * (jp + 1)
        rf = gf if jp == 0 else pltpu.roll(gf, lo, axis=1)
        rb = gb if jp == CHUNK - 1 else pltpu.roll(gb, hi, axis=1)
        blk = jnp.where(lane_in >= lo, rf, 0.0) + jnp.where(lane_in < hi, rb, 0.0)
        on_diag = (step_of_lane == jp) & (chan_of_lane == chan_row)
        blk = blk + jnp.where(on_diag, dvec, 0.0)
        for e in range(2):
            r0 = lo + e * SSM_GROUP
            w1_ref[r0:r0 + SSM_GROUP, 0:PAIR_IN] = jnp.where(grp_of_lane == e, blk, 0.0).astype(bf16)


def _s5_matrices(prow, bt_re, bt_im, ct_re, ct_im, d_vec, first_f32=()):
    n_q = N_PAIR // PREP_PAIRS
    n_steps = DEPTH * n_q
    cast_in, cast_out, cast_shapes = [], [], []
    for w in first_f32:
        _, rows, cols = w.shape
        n_blk = next(n for n in (n_steps, n_steps // 2, n_steps // 4)
                     if rows % n == 0 and (rows // n) % 16 == 0)
        idx = lambda l, q, n_blk=n_blk: (jnp.minimum(l * n_q + q, n_blk - 1), 0)
        cast_in.append(pl.BlockSpec((None, rows // n_blk, cols), lambda l, q, idx=idx: (0,) + idx(l, q)))
        cast_out.append(pl.BlockSpec((rows // n_blk, cols), idx))
        cast_shapes.append(jax.ShapeDtypeStruct((rows, cols), bf16))
    pp = PREP_PAIRS
    res = pl.pallas_call(
        _prep_kernel,
        grid=(DEPTH, n_q),
        in_specs=[
            pl.BlockSpec((None, pp, 8, PAIR_ST), lambda l, q: (l, q, 0, 0)),
            pl.BlockSpec((None, 2, pp, SSM_GROUP, PAIR_ST), lambda l, q: (l, 0, q, 0, 0)),
            pl.BlockSpec((None, 2, pp, SSM_GROUP, PAIR_ST), lambda l, q: (l, 0, q, 0, 0)),
            pl.BlockSpec((None, 2, pp, PAIR_ST, SSM_GROUP), lambda l, q: (l, 0, q, 0, 0)),
            pl.BlockSpec((None, 2, pp, PAIR_ST, SSM_GROUP), lambda l, q: (l, 0, q, 0, 0)),
            pl.BlockSpec((None, pp, 1, PAIR_IN), lambda l, q: (l, q, 0, 0)),
        ] + cast_in,
        out_specs=[
            pl.BlockSpec((None, pp, PAIR_IN, PAIR_IN + 4 * PAIR_ST), lambda l, q: (l, q, 0, 0)),
            pl.BlockSpec((None, pp, 4 * PAIR_ST, PAIR_IN), lambda l, q: (l, q, 0, 0)),
            pl.BlockSpec((None, 4, pp * PAIR_ST), lambda l, q: (l, 0, q)),
        ] + cast_out,
        out_shape=[
            jax.ShapeDtypeStruct((DEPTH, N_PAIR, PAIR_IN, PAIR_IN + 4 * PAIR_ST), bf16),
            jax.ShapeDtypeStruct((DEPTH, N_PAIR, 4 * PAIR_ST, PAIR_IN), bf16),
            jax.ShapeDtypeStruct((DEPTH, 4, SSM_GROUPS * SSM_STATE), f32),
        ] + cast_shapes,
        compiler_params=_params("arbitrary", "arbitrary"),
        name="s5_chunk_matrices",
    )(prow, bt_re, bt_im, ct_re, ct_im, d_vec, *first_f32)
    return res[0], res[1], res[2], tuple(res[3:])


def _chunk_scan(pr, pi, pows, n_rows, seq, reverse):
    pos = lax.broadcasted_iota(jnp.int32, (n_rows, 1), 0) & (seq - 1)
    shift, k = 1, 0
    while shift < seq:
        ar, ai = pows[k]
        if reverse:
            valid = pos < seq - shift
            sr = pltpu.roll(pr, n_rows - shift, axis=0)
            si = pltpu.roll(pi, n_rows - shift, axis=0)
        else:
            valid = pos >= shift
            sr = pltpu.roll(pr, shift, axis=0)
            si = pltpu.roll(pi, shift, axis=0)
        tr, ti = _cmul(ar, ai, sr, si)
        pr = pr + jnp.where(valid, tr, 0.0)
        pi = pi + jnp.where(valid, ti, 0.0)
        shift, k = 2 * shift, k + 1
    return pr, pi


def _s5_kernel(*refs):
    a_refs = refs[0:CHUNK]
    w1_ref, w2_ref, at_ref, h0_ref = refs[CHUNK:CHUNK + 4]
    y_ref, fin_ref = refs[CHUNK + 4:CHUNK + 6]
    r_scr, sp_scr, y_scr, sr_scr, si_scr = refs[CHUNK + 6:]

    lane_blk = lax.broadcasted_iota(jnp.int32, (1, LANES), 1) // PAIR_CH
    row_lat = lax.broadcasted_iota(jnp.int32, (ROWS_LAT, 1), 0)
    pos_ctx = lax.broadcasted_iota(jnp.int32, (ROWS_CTX, 1), 0) & (NC_CTX - 1)
    pos_lat = row_lat & (NC_LAT - 1)

    for p in range(PAIRS_PER_STEP):
        pieces = []
        for k in range(CHUNK // STEPS_PER_LANE_BLOCK):
            piece = None
            for jj in range(STEPS_PER_LANE_BLOCK):
                src = a_refs[STEPS_PER_LANE_BLOCK * k + jj][...]
                sh = (PAIR_CH * (jj - p)) % LANES
                rolled = src if sh == 0 else pltpu.roll(src, sh, axis=1)
                piece = rolled if piece is None else jnp.where(lane_blk == jj, rolled, piece)
            pieces.append(piece.astype(bf16))
        u_pair = jnp.concatenate(pieces, axis=1)
        r_scr[...] = _dot(u_pair, w1_ref[p])

        lanes_p = slice(p * PAIR_ST, (p + 1) * PAIR_ST)
        for d in range(2):
            pows = [(at_ref[2 * d:2 * d + 1, lanes_p], at_ref[2 * d + 1:2 * d + 2, lanes_p])]
            for _ in range(5):
                pows.append(_cmul(*pows[-1], *pows[-1]))
            c_re = PAIR_IN + 2 * d * PAIR_ST
            c_im = c_re + PAIR_ST
            reverse = d == 1

            sr, si = _chunk_scan(r_scr[0:ROWS_CTX, c_re:c_re + PAIR_ST],
                                 r_scr[0:ROWS_CTX, c_im:c_im + PAIR_ST],
                                 pows, ROWS_CTX, NC_CTX, reverse)
            sr_scr[...] = sr
            si_scr[...] = si
            last = 0 if reverse else NC_CTX - 1
            fin_ref[2 * d, :, lanes_p] = sr_scr[pl.ds(last, BATCH, stride=NC_CTX), :]
            fin_ref[2 * d + 1, :, lanes_p] = si_scr[pl.ds(last, BATCH, stride=NC_CTX), :]
            edge = pos_ctx == (NC_CTX - 1 if reverse else 0)
            back = ROWS_CTX - 1 if reverse else 1
            sp_scr[0:ROWS_CTX, 2 * d * PAIR_ST:(2 * d + 1) * PAIR_ST] = jnp.where(
                edge, 0.0, pltpu.roll(sr, back, axis=0))
            sp_scr[0:ROWS_CTX, (2 * d + 1) * PAIR_ST:(2 * d + 2) * PAIR_ST] = jnp.where(
                edge, 0.0, pltpu.roll(si, back, axis=0))

            h0r = jnp.where(row_lat < NC_LAT, h0_ref[2 * d, 0:1, lanes_p], h0_ref[2 * d, 1:2, lanes_p])
            h0i = jnp.where(row_lat < NC_LAT, h0_ref[2 * d + 1, 0:1, lanes_p],
                            h0_ref[2 * d + 1, 1:2, lanes_p])
            edge = pos_lat == (NC_LAT - 1 if reverse else 0)
            ahr, ahi = _cmul(*pows[0], h0r, h0i)
            pr = r_scr[ROWS_CTX:ROWS_S5, c_re:c_re + PAIR_ST] + jnp.where(edge, ahr, 0.0)
            pi = r_scr[ROWS_CTX:ROWS_S5, c_im:c_im + PAIR_ST] + jnp.where(edge, ahi, 0.0)
            sr, si = _chunk_scan(pr, pi, pows, ROWS_LAT, NC_LAT, reverse)
            back = ROWS_LAT - 1 if reverse else 1
            sp_scr[ROWS_CTX:ROWS_S5, 2 * d * PAIR_ST:(2 * d + 1) * PAIR_ST] = jnp.where(
                edge, h0r, pltpu.roll(sr, back, axis=0))
            sp_scr[ROWS_CTX:ROWS_S5, (2 * d + 1) * PAIR_ST:(2 * d + 2) * PAIR_ST] = jnp.where(
                edge, h0i, pltpu.roll(si, back, axis=0))

        y_scr[:, p * PAIR_IN:(p + 1) * PAIR_IN] = (
            r_scr[:, 0:PAIR_IN] + _dot(sp_scr[...].astype(bf16), w2_ref[p]))

    for j in range(CHUNK):
        k, jj = divmod(j, STEPS_PER_LANE_BLOCK)
        out = None
        for p in range(PAIRS_PER_STEP):
            src = y_scr[:, p * PAIR_IN + k * LANES:p * PAIR_IN + (k + 1) * LANES]
            sh = (PAIR_CH * (p - jj)) % LANES
            rolled = src if sh == 0 else pltpu.roll(src, sh, axis=1)
            out = rolled if out is None else jnp.where(lane_blk == p, rolled, out)
        y_ref[:, j, :] = out


def _s5_mixer(u_rows, w1, w2, a_chunk, h0, layer):
    n_state = SSM_GROUPS * SSM_STATE
    n_blk = SSM_WIDTH // LANES
    step_spec = lambda j: pl.BlockSpec((None, ROWS_S5, LANES), lambda g, j=j: (j, 0, g))
    w_pairs = lambda rows, cols: pl.BlockSpec((None, PAIRS_PER_STEP, rows, cols),
                                              lambda g: (layer, g, 0, 0))
    st_lanes = PAIRS_PER_STEP * PAIR_ST
    return pl.pallas_call(
        _s5_kernel,
        grid=(n_blk,),
        in_specs=[step_spec(j) for j in range(CHUNK)] + [
            w_pairs(PAIR_IN, PAIR_IN + 4 * PAIR_ST),
            w_pairs(4 * PAIR_ST, PAIR_IN),
            pl.BlockSpec((None, 4, st_lanes), lambda g: (layer, 0, g)),
            pl.BlockSpec((4, DEC_BATCH, st_lanes), lambda g: (0, 0, g)),
        ],
        out_specs=[
            pl.BlockSpec((ROWS_S5, CHUNK, LANES), lambda g: (0, 0, g)),
            pl.BlockSpec((4, BATCH, st_lanes), lambda g: (0, 0, g)),
        ],
        out_shape=[
            jax.ShapeDtypeStruct((ROWS_S5, CHUNK, SSM_WIDTH), f32),
            jax.ShapeDtypeStruct((4, BATCH, n_state), f32),
        ],
        scratch_shapes=[
            pltpu.VMEM((ROWS_S5, PAIR_IN + 4 * PAIR_ST), f32),
            pltpu.VMEM((ROWS_S5, 4 * PAIR_ST), f32),
            pltpu.VMEM((ROWS_S5, PAIRS_PER_STEP * PAIR_IN), f32),
            pltpu.VMEM((ROWS_CTX, PAIR_ST), f32),
            pltpu.VMEM((ROWS_CTX, PAIR_ST), f32),
        ],
        compiler_params=_params("arbitrary"),
        name="s5_chunked_mixer",
    )(*([u_rows] * CHUNK), w1, w2, a_chunk, h0)


def _shift_rows(v, period, n_rows, shift):
    row = lax.broadcasted_iota(jnp.int32, (n_rows, 1), 0)
    pos = row & (period - 1)
    prev = jnp.where(pos < shift, 0.0, pltpu.roll(v, shift, axis=0))
    nxt = jnp.where(pos >= period - shift, 0.0, pltpu.roll(v, n_rows - shift, axis=0))
    return prev, nxt


def _dwconv3_rows(v, w_ref, period, n_rows, shift=1):
    prev, nxt = _shift_rows(v, period, n_rows, shift)
    return prev * w_ref[0:1, :] + v * w_ref[1:2, :] + nxt * w_ref[2:3, :]


def _dwconv3_rows_static(v, w, period, n_rows, shift, window=None):
    w0, w1, w2 = w[0:1, :], w[1:2, :], w[2:3, :]
    if shift % 8 == 0 and period == n_rows:
        zeros = jnp.zeros((shift, v.shape[1]), v.dtype)
        prev = jnp.concatenate([zeros, v[:n_rows - shift]], axis=0)
        nxt = jnp.concatenate([v[shift:], zeros], axis=0)
        return prev * w0 + v * w1 + nxt * w2
    if window is not None:
        pad = jnp.zeros((8, v.shape[1]), v.dtype)
        window[0:8, :] = pad
        window[8 + n_rows:16 + n_rows, :] = pad
        window[8:8 + n_rows, :] = v
        prev = window[8 - shift:8 - shift + n_rows, :]
        nxt = window[8 + shift:8 + shift + n_rows, :]
    else:
        prev = pltpu.roll(v, shift, axis=0)
        nxt = pltpu.roll(v, n_rows - shift, axis=0)
    out = prev * w0 + v * w1 + nxt * w2
    sub = lax.broadcasted_iota(jnp.int32, (8, 1), 0)
    pieces = []
    for lo in range(0, n_rows, period):
        hi = lo + period
        first = (jnp.where(sub < shift, 0.0, prev[lo:lo + 8]) * w0 + v[lo:lo + 8] * w1
                 + nxt[lo:lo + 8] * w2)
        last = (prev[hi - 8:hi] * w0 + v[hi - 8:hi] * w1
                + jnp.where(sub >= 8 - shift, 0.0, nxt[hi - 8:hi]) * w2)
        pieces += [first, out[lo + 8:hi - 8], last]
    return jnp.concatenate(pieces, axis=0)


def _k1_kernel(*refs, split_input):
    if split_input:
        xp_ref, xs_ref = refs[:2]
        refs = refs[2:]
    else:
        x_ref = refs[0]
        refs = refs[1:]
    mod_ref, gpre_ref, wgb_ref, wgc_ref, whv_ref, wu_ref, cw_ref, gco_ref, cn_ref, u_ref = refs[:10]
    if split_input:
        x_ref, h_scr, u_scr = refs[10:]
    else:
        h_scr, u_scr = refs[10:]
    i = pl.program_id(0)
    n = pl.program_id(1)
    tm = cn_ref.shape[0]
    n_ctx_tiles = N_CTX // tm

    @pl.when(n == 0)
    def _():
        if split_input:
            @pl.when(i < n_ctx_tiles)
            def _():
                x_ref[...] = xp_ref[...]

            @pl.when(i >= n_ctx_tiles)
            def _():
                x_ref[...] = xs_ref[...]

        r = jnp.where(i < n_ctx_tiles, 0, 1 + (i - n_ctx_tiles) // (DEC_SEQ // tm))
        sh1 = mod_ref[pl.ds(r, 1), 0:D_MODEL]
        sc1 = mod_ref[pl.ds(r, 1), D_MODEL:2 * D_MODEL]
        h_scr[...] = (_rms_rows(x_ref[...], gpre_ref[...] * (1.0 + sc1)) + sh1).astype(bf16)

    h = h_scr[...]
    u = _dot(h, wu_ref[...])
    for t in range(TC1 // LANES):
        u_scr[t] = u[:, t * LANES:(t + 1) * LANES]
    for j in range(CHUNK):
        for t in range(TC1 // LANES):
            u_ref[j, :, t * LANES:(t + 1) * LANES] = u_scr[t, pl.ds(j, tm // CHUNK, stride=CHUNK), :]
    v = _dot(h, wgc_ref[...]) * _dot(h, whv_ref[...])
    period = jnp.where(i < n_ctx_tiles, SEQ, GRID_W)
    v = _dwconv3_rows(v, cw_ref, period, tm)
    co = _dot(h, wgb_ref[...]) * v
    rr = lax.broadcasted_iota(jnp.int32, (TC1, TC1), 0) // HEAD_DIM
    cc = lax.broadcasted_iota(jnp.int32, (TC1, TC1), 1) // HEAD_DIM
    avg = jnp.where(rr == cc, 1.0 / HEAD_DIM, 0.0).astype(bf16)
    hi, lo = _split_bf16(co * co)
    ms = _dot(hi, avg) + _dot(lo, avg)
    cn_ref[...] = (co * lax.rsqrt(ms + EPS) * gco_ref[...]).astype(bf16)


def _in_proj(xs, mod, g_pre1, w_in_bf, conv_w, g_conv_out, layer):
    nb = CONV_WIDTH // TC1
    split_input = len(xs) == 2
    tm = TM1_SPLIT if split_input else TM1
    n_ctx_tiles = N_CTX // tm
    wspec = lambda off: pl.BlockSpec((D_MODEL, TC1), lambda i, n: (0, off * nb + n))
    x_spec = pl.BlockSpec((tm, D_MODEL), lambda i, n: (i, 0))
    if split_input:
        x_specs = [pl.BlockSpec((tm, D_MODEL), lambda i, n: (jnp.minimum(i, n_ctx_tiles - 1), 0)),
                   pl.BlockSpec((tm, D_MODEL), lambda i, n: (jnp.maximum(i - n_ctx_tiles, 0), 0))]
    else:
        x_specs = [x_spec]
    return pl.pallas_call(
        functools.partial(_k1_kernel, split_input=split_input),
        grid=(N_TOK // tm, nb),
        in_specs=x_specs + [
            pl.BlockSpec((None, 8, 6 * D_MODEL), lambda i, n: (layer, 0, 0)),
            pl.BlockSpec((None, 1, D_MODEL), lambda i, n: (layer, 0, 0)),
            wspec(0), wspec(1), wspec(2), wspec(3),
            pl.BlockSpec((None, 3, TC1), lambda i, n: (layer, 0, n)),
            pl.BlockSpec((None, 1, TC1), lambda i, n: (layer, 0, n)),
        ],
        out_specs=[
            pl.BlockSpec((tm, TC1), lambda i, n: (i, n)),
            pl.BlockSpec((CHUNK, tm // CHUNK, TC1), lambda i, n: (0, i, n)),
        ] + ([x_spec] if split_input else []),
        out_shape=[
            jax.ShapeDtypeStruct((N_TOK, CONV_WIDTH), bf16),
            jax.ShapeDtypeStruct((CHUNK, ROWS_S5, SSM_WIDTH), f32),
        ] + ([jax.ShapeDtypeStruct((N_TOK, D_MODEL), f32)] if split_input else []),
        scratch_shapes=[pltpu.VMEM((tm, D_MODEL), bf16),
                        pltpu.VMEM((TC1 // LANES, tm, LANES), f32)],
        compiler_params=_params("arbitrary", "arbitrary"),
        name="in_proj_conv_mixer",
    )(*xs, mod, g_pre1, w_in_bf, w_in_bf, w_in_bf, w_in_bf, conv_w, g_conv_out)


def _k3_kernel(x_ref, cn_ref, y_ref, mod_ref, wglu_ref, bglu_ref, gsso_ref, wout_ref,
               gpost1_ref, gpre2_ref, x1_ref, h2_ref):
    i = pl.program_id(0)
    n_ctx_tiles = N_CTX // TM3
    r = jnp.where(i < n_ctx_tiles, 0, 1 + (i - n_ctx_tiles) // (DEC_SEQ // TM3))
    gt1 = mod_ref[pl.ds(r, 1), 2 * D_MODEL:3 * D_MODEL]
    sh2 = mod_ref[pl.ds(r, 1), 3 * D_MODEL:4 * D_MODEL]
    sc2 = mod_ref[pl.ds(r, 1), 4 * D_MODEL:5 * D_MODEL]

    gain1 = gt1 * gpost1_ref[...]
    gain2 = gpre2_ref[...] * (1.0 + sc2)
    for s in range(TM3 // SUB3):
        rows = slice(s * SUB3, (s + 1) * SUB3)
        z = jax.nn.gelu(y_ref[rows, :], approximate=True)
        gate = jax.nn.sigmoid(_dot(z.astype(bf16), wglu_ref[...]) + bglu_ref[...])
        sn = _rms_rows(z * gate, gsso_ref[...])
        mixed = (_dot(cn_ref[rows, :], wout_ref[0:CONV_WIDTH, :])
                 + _dot(sn.astype(bf16), wout_ref[CONV_WIDTH:CONV_WIDTH + SSM_WIDTH, :]))
        x1 = x_ref[rows, :] + _rms_rows(mixed, gain1)
        x1_ref[rows, :] = x1
        h2_ref[rows, :] = (_rms_rows(x1, gain2) + sh2).astype(bf16)


def _out_proj(x, cn, y, mod, w_glu_bf, b_glu, g_ssm_out, w_out_bf, g_post1, g_pre2, layer):
    vec = lambda width: pl.BlockSpec((None, 1, width), lambda i: (layer, 0, 0))
    return pl.pallas_call(
        _k3_kernel,
        grid=(N_TOK // TM3,),
        in_specs=[
            pl.BlockSpec((TM3, D_MODEL), lambda i: (i, 0)),
            pl.BlockSpec((TM3, CONV_WIDTH), lambda i: (i, 0)),
            pl.BlockSpec((TM3, SSM_WIDTH), lambda i: (i, 0)),
            pl.BlockSpec((None, 8, 6 * D_MODEL), lambda i: (layer, 0, 0)),
            pl.BlockSpec((SSM_WIDTH, SSM_WIDTH), lambda i: (0, 0)),
            vec(SSM_WIDTH), vec(SSM_WIDTH),
            pl.BlockSpec((CONV_WIDTH + SSM_WIDTH, D_MODEL), lambda i: (0, 0)),
            vec(D_MODEL), vec(D_MODEL),
        ],
        out_specs=[
            pl.BlockSpec((TM3, D_MODEL), lambda i: (i, 0)),
            pl.BlockSpec((TM3, D_MODEL), lambda i: (i, 0)),
        ],
        out_shape=[
            jax.ShapeDtypeStruct((N_TOK, D_MODEL), f32),
            jax.ShapeDtypeStruct((N_TOK, D_MODEL), bf16),
        ],
        compiler_params=_params("arbitrary"),
        name="glu_out_proj",
    )(x, cn, y, mod, w_glu_bf, b_glu, g_ssm_out, w_out_bf, g_post1, g_pre2)


def _k4_kernel(h_ref, x1_ref, mod_ref, wa_ref, wg_ref, cwa_ref, cwg_ref, wd_ref, gpost2_ref,
               *rest, latent):
    windows = (None, None)
    if not latent:
        windows, rest = rest[-2:], rest[:-2]
    n_cast = (len(rest) - 1) // 2
    o_ref = rest[n_cast]
    for src, dst in zip(rest[:n_cast], rest[n_cast + 1:]):
        dst[...] = src[...].astype(bf16)

    i = pl.program_id(0)
    j = pl.program_id(1)
    blk = o_ref.shape

    @pl.when(j == 0)
    def _():
        o_ref[...] = jnp.zeros(blk, f32)

    period, shift = (TM4, LAT_STRIP) if latent else (SEQ, 1)
    h = h_ref[...].reshape(TM4, D_MODEL)
    acts = []
    for sub in range(TF4 // SUB4):
        cols = slice(sub * SUB4, (sub + 1) * SUB4)
        a = _dwconv3_rows_static(_dot(h, wa_ref[:, cols]), cwa_ref[:, cols], period, TM4, shift,
                                 windows[0])
        g = _dwconv3_rows_static(_dot(h, wg_ref[:, cols]), cwg_ref[:, cols], period, TM4, shift,
                                 windows[1])
        acts.append((g * jax.nn.sigmoid(g) * a).astype(bf16))
    o_ref[...] += _dot(jnp.concatenate(acts, axis=1), wd_ref[...]).reshape(blk)

    @pl.when(j == pl.num_programs(1) - 1)
    def _():
        r = (1 + i // (GRID_W // LAT_STRIP)) if latent else 0
        gt2 = mod_ref[pl.ds(r, 1), 5 * D_MODEL:6 * D_MODEL]
        ff = o_ref[...].reshape(TM4, D_MODEL)
        x2 = x1_ref[...].reshape(TM4, D_MODEL) + _rms_rows(ff, gt2 * gpost2_ref[...])
        o_ref[...] = x2.reshape(blk)


def _conv_ffn(h2, x1, mod, w_up_bf, ffn_conv_w, w_down_bf, g_post2, layer, next_f32=(), last=False):
    nj = D_FF // TF4
    n_ctx_tiles = N_CTX // TM4

    weight_specs = [
        pl.BlockSpec((None, 8, 6 * D_MODEL), lambda i, j: (layer, 0, 0)),
        pl.BlockSpec((D_MODEL, TF4), lambda i, j: (0, j)),
        pl.BlockSpec((D_MODEL, TF4), lambda i, j: (0, nj + j)),
        pl.BlockSpec((None, 3, TF4), lambda i, j: (layer, 0, j)),
        pl.BlockSpec((None, 3, TF4), lambda i, j: (layer, 0, nj + j)),
        pl.BlockSpec((TF4, D_MODEL), lambda i, j: (j, 0)),
        pl.BlockSpec((None, 1, D_MODEL), lambda i, j: (layer, 0, 0)),
    ]
    weights = (mod, w_up_bf, w_up_bf, ffn_conv_w, ffn_conv_w, w_down_bf, g_post2)

    cast_in, cast_out, cast_shapes = [], [], []
    for w in next_f32:
        _, rows, cols = w.shape
        if cols % nj == 0:
            blk = (rows // n_ctx_tiles, cols // nj)
            idx = lambda i, j: (i, j)
        elif rows % nj == 0:
            blk = (rows // nj, cols // n_ctx_tiles)
            idx = lambda i, j: (j, i)
        else:
            blk = (rows // n_ctx_tiles, cols // n_ctx_tiles)
            idx = lambda i, j: (i, jnp.minimum(j, n_ctx_tiles - 1))
        cast_in.append(pl.BlockSpec((None,) + blk, lambda i, j, idx=idx: (layer + 1,) + idx(i, j)))
        cast_out.append(pl.BlockSpec(blk, idx))
        cast_shapes.append(jax.ShapeDtypeStruct((rows, cols), bf16))
    row_spec = pl.BlockSpec((TM4, D_MODEL), lambda i, j: (i, 0))
    res = pl.pallas_call(
        functools.partial(_k4_kernel, latent=False),
        grid=(n_ctx_tiles, nj),
        in_specs=[row_spec, row_spec] + weight_specs + cast_in,
        out_specs=[row_spec] + cast_out,
        out_shape=[jax.ShapeDtypeStruct((N_CTX if last else N_TOK, D_MODEL), f32)] + cast_shapes,
        scratch_shapes=[pltpu.VMEM((TM4 + 16, SUB4), f32)] * 2,
        input_output_aliases={} if last else {1: 0},
        compiler_params=_params("arbitrary", "arbitrary"),
        name="conv_ffn_ctx",
    )(h2, x1, *weights, *next_f32)

    view = lambda t: t.reshape(-1, GRID_W, D_MODEL)
    rows_per_img = DEC_SEQ // GRID_W
    strips = GRID_W // LAT_STRIP
    first_img = N_CTX // DEC_SEQ
    strip = (rows_per_img, LAT_STRIP, D_MODEL)
    strip_spec = lambda img0: pl.BlockSpec(strip, lambda i, j: (img0 + i // strips, i % strips, 0))
    x2_lat = pl.pallas_call(
        functools.partial(_k4_kernel, latent=True),
        grid=(DEC_BATCH * strips, nj),
        in_specs=[strip_spec(first_img), strip_spec(first_img)] + weight_specs,
        out_specs=strip_spec(0 if last else first_img),
        out_shape=jax.ShapeDtypeStruct(((N_LAT if last else N_TOK) // GRID_W, GRID_W, D_MODEL), f32),
        input_output_aliases={} if last else {1: 0},
        compiler_params=_params("arbitrary", "arbitrary"),
        name="conv_ffn_lat",
    )(view(h2), view(x1 if last else res[0]), *weights)
    if last:
        return (res[0], x2_lat.reshape(N_LAT, D_MODEL)), ()
    return x2_lat.reshape(N_TOK, D_MODEL), tuple(res[1:])


def kernel(x_prompt, x_sample, state_ssm_re, state_ssm_im, c, c_ctx, w_ada, b_ada, g_pre1, w_in, conv_w, ssm_a_re, ssm_a_im, ssm_log_dt, ssm_b_re, ssm_b_im, ssm_c_re, ssm_c_im, ssm_d, w_glu, b_glu, g_conv_out, g_ssm_out, w_out, g_post1, g_pre2, w_up, ffn_conv_w, w_down, g_post2):
    f32_weights = (w_in, w_glu, w_out, w_up, w_down)

    cvec8 = jnp.concatenate([c_ctx[None, :], c, jnp.zeros((8 - 1 - DEC_BATCH, D_MODEL), f32)], axis=0)
    mod = _modulation(cvec8, w_ada, b_ada)

    abr, abi, bfr, bfi = [t.reshape(DEPTH, 2, N_PAIR, PAIR_ST)
                          for t in _discretise(ssm_a_re, ssm_a_im, ssm_log_dt)]
    prow = jnp.stack([abr, abi, bfr, bfi], axis=2)
    prow = prow.transpose(0, 3, 1, 2, 4).reshape(DEPTH, N_PAIR, 8, PAIR_ST)

    def b_rows(b):
        b = b.transpose(0, 1, 2, 4, 3).reshape(DEPTH, 2, N_PAIR, 2, SSM_GROUP, SSM_STATE)
        return b.transpose(0, 1, 2, 4, 3, 5).reshape(DEPTH, 2, N_PAIR, SSM_GROUP, PAIR_ST)

    def c_cols(cm):
        return cm.transpose(0, 1, 2, 4, 3).reshape(DEPTH, 2, N_PAIR, PAIR_ST, SSM_GROUP)

    d_vec = jnp.broadcast_to(ssm_d.reshape(DEPTH, N_PAIR, 1, PAIR_CH), (DEPTH, N_PAIR, CHUNK, PAIR_CH))
    d_vec = d_vec.reshape(DEPTH, N_PAIR, 1, PAIR_IN)
    w1, w2, a_chunk, bf_weights = _s5_matrices(
        prow, b_rows(ssm_b_re), b_rows(ssm_b_im), c_cols(ssm_c_re), c_cols(ssm_c_im), d_vec,
        first_f32=f32_weights)

    g3 = lambda g: g.reshape(DEPTH, 1, -1)
    g_pre1_, g_conv_out_, g_ssm_out_, g_post1_, g_pre2_, g_post2_, b_glu_ = map(
        g3, (g_pre1, g_conv_out, g_ssm_out, g_post1, g_pre2, g_post2, b_glu))

    xs = (x_prompt.reshape(N_CTX, D_MODEL), x_sample.reshape(N_LAT, D_MODEL))
    fins = []
    for l in range(DEPTH):
        last = l + 1 == DEPTH
        w_in_bf, w_glu_bf, w_out_bf, w_up_bf, w_down_bf = bf_weights
        res = _in_proj(xs, mod, g_pre1_, w_in_bf, conv_w, g_conv_out_, l)
        cn, u = res[:2]
        x = res[2] if len(xs) == 2 else xs[0]
        sre = state_ssm_re[:, l].reshape(DEC_BATCH, 2, -1)
        sim = state_ssm_im[:, l].reshape(DEC_BATCH, 2, -1)
        h0 = jnp.stack([sre[:, 0], sim[:, 0], sre[:, 1], sim[:, 1]], axis=0)
        y, fin = _s5_mixer(u, w1, w2, a_chunk, h0, l)
        fins.append(fin)
        x1, h2 = _out_proj(x, cn, y.reshape(N_TOK, SSM_WIDTH), mod, w_glu_bf, b_glu_, g_ssm_out_,
                           w_out_bf, g_post1_, g_pre2_, l)
        x, bf_weights = _conv_ffn(h2, x1, mod, w_up_bf, ffn_conv_w, w_down_bf, g_post2_, l,
                                  next_f32=() if last else f32_weights, last=last)
        xs = (x,)

    fin = jnp.stack(fins, axis=0).reshape(DEPTH, 2, 2, BATCH, SSM_GROUPS, SSM_STATE)
    new_re = fin[:, :, 0].transpose(2, 0, 1, 3, 4)
    new_im = fin[:, :, 1].transpose(2, 0, 1, 3, 4)
    y_prompt, y_sample = x
    return (y_prompt.reshape(BATCH, SEQ, D_MODEL), y_sample.reshape(DEC_BATCH, DEC_SEQ, D_MODEL),
            new_re, new_im)
```

```python
import functools

import jax
import jax.numpy as jnp
from jax import lax
from jax.experimental import pallas as pl
from jax.experimental.pallas import tpu as pltpu

D_MODEL = 2048
BATCH = 16
SEQ = 256
DEPTH = 4
DEC_BATCH = 2
DEC_SEQ = 1024
GRID_W = 64
CONV_WIDTH = 1024
HEAD_DIM = 64
SSM_WIDTH = 1024
SSM_GROUP = 16
SSM_GROUPS = 64
SSM_STATE = 64
D_FF = 5632
EPS = 1e-6

LANES = 128
N_CTX = BATCH * SEQ
N_LAT = DEC_BATCH * DEC_SEQ
N_TOK = N_CTX + N_LAT
CHUNK = 16
NC_CTX = SEQ // CHUNK
NC_LAT = DEC_SEQ // CHUNK
ROWS_CTX = BATCH * NC_CTX
ROWS_LAT = DEC_BATCH * NC_LAT
ROWS_S5 = ROWS_CTX + ROWS_LAT
N_PAIR = SSM_GROUPS // 2
PAIR_CH = 2 * SSM_GROUP
PAIR_IN = CHUNK * PAIR_CH
PAIR_ST = 2 * SSM_STATE
PAIRS_PER_STEP = LANES // PAIR_CH
STEPS_PER_LANE_BLOCK = LANES // PAIR_CH
PREP_PAIRS = 4

TM1 = 1024
TM1_SPLIT = 512
TC1 = 256
TM3 = 512
SUB3 = 256
TM4 = 512
TF4 = 512
SUB4 = 256
LAT_STRIP = 32

VMEM_LIMIT = 56 * 1024 * 1024

f32 = jnp.float32
bf16 = jnp.bfloat16


def _dot(a, b):
    return jnp.dot(a, b, preferred_element_type=f32)


def _split_bf16(x):
    hi = x.astype(bf16)
    lo = (x - hi.astype(f32)).astype(bf16)
    return hi, lo


def _dot3(a, b):
    ah, al = _split_bf16(a)
    bh, bl = _split_bf16(b)
    return _dot(ah, bh) + (_dot(ah, bl) + _dot(al, bh))


def _dot_select(a, sel):
    a1 = a.astype(bf16)
    r1 = a - a1.astype(f32)
    a2 = r1.astype(bf16)
    a3 = (r1 - a2.astype(f32)).astype(bf16)
    return _dot(a1, sel) + (_dot(a2, sel) + _dot(a3, sel))


def _rms_rows(x, g):
    ms = jnp.mean(x * x, axis=-1, keepdims=True)
    return x * lax.rsqrt(ms + EPS) * g


def _cmul(ar, ai, br, bi):
    return ar * br - ai * bi, ar * bi + ai * br


def _params(*sem):
    return pltpu.CompilerParams(dimension_semantics=sem, vmem_limit_bytes=VMEM_LIMIT)


def _mod_kernel(cv_ref, w_ref, b_ref, o_ref):
    cv = cv_ref[...]
    s = cv * jax.nn.sigmoid(cv)
    o_ref[...] = _dot(s.astype(bf16), w_ref[...].astype(bf16)) + b_ref[...]


def _modulation(cvec8, w_ada, b_ada):
    tn = 2048
    n_out = 6 * D_MODEL
    return pl.pallas_call(
        _mod_kernel,
        grid=(DEPTH, n_out // tn),
        in_specs=[
            pl.BlockSpec((8, D_MODEL), lambda l, n: (0, 0)),
            pl.BlockSpec((None, D_MODEL, tn), lambda l, n: (l, 0, n)),
            pl.BlockSpec((None, 1, tn), lambda l, n: (l, 0, n)),
        ],
        out_specs=pl.BlockSpec((None, 8, tn), lambda l, n: (l, 0, n)),
        out_shape=jax.ShapeDtypeStruct((DEPTH, 8, n_out), f32),
        compiler_params=_params("arbitrary", "arbitrary"),
        name="adaln_modulation",
    )(cvec8, w_ada, b_ada.reshape(DEPTH, 1, n_out))


def _disc_kernel(ar_ref, ai_ref, ldt_ref, abr_ref, abi_ref, bfr_ref, bfi_ref):
    ar = ar_ref[...]
    ai = ai_ref[...]
    dt = jnp.exp(ldt_ref[...])
    mag = jnp.exp(ar * dt)
    abr = mag * jnp.cos(ai * dt)
    abi = mag * jnp.sin(ai * dt)
    nr = abr - 1.0
    den = ar * ar + ai * ai
    abr_ref[...] = abr
    abi_ref[...] = abi
    bfr_ref[...] = (nr * ar + abi * ai) / den
    bfi_ref[...] = (abi * ar - nr * ai) / den


def _discretise(a_re, a_im, log_dt):
    shape = (DEPTH * 2, SSM_GROUPS * SSM_STATE)
    spec = pl.BlockSpec(shape, lambda: (0, 0))
    out = jax.ShapeDtypeStruct(shape, f32)
    return pl.pallas_call(
        _disc_kernel,
        in_specs=[spec, spec, spec],
        out_specs=[spec, spec, spec, spec],
        out_shape=[out, out, out, out],
        name="s5_discretise",
    )(a_re.reshape(shape), a_im.reshape(shape),
      jnp.broadcast_to(log_dt[..., None], (DEPTH, 2, SSM_GROUPS, SSM_STATE)).reshape(shape))


def _pow_table(br, bi, kk, shape, bits=4):
    tr = ti = None
    pr, pi = br, bi
    for bit in range(bits):
        sel = ((kk >> bit) & 1) == 1
        fr = jnp.broadcast_to(jnp.where(sel, pr, 1.0), shape)
        fi = jnp.broadcast_to(jnp.where(sel, pi, 0.0), shape)
        if tr is None:
            tr, ti = fr, fi
        else:
            tr, ti = _cmul(tr, ti, fr, fi)
        pr, pi = _cmul(pr, pi, pr, pi)
    return tr, ti


def _prep_kernel(pr_ref, btr_ref, bti_ref, ctr_ref, cti_ref, dv_ref, *rest):
    n_cast = (len(rest) - 3) // 2
    w1_ref, w2_ref, at_ref = rest[n_cast:n_cast + 3]
    for src, dst in zip(rest[:n_cast], rest[n_cast + 3:]):
        dst[...] = src[...].astype(bf16)
    for pp in range(PREP_PAIRS):
        _prep_pair(pr_ref.at[pp], btr_ref.at[:, pp], bti_ref.at[:, pp], ctr_ref.at[:, pp],
                   cti_ref.at[:, pp], dv_ref.at[pp], w1_ref.at[pp], w2_ref.at[pp],
                   at_ref.at[:, pl.ds(pp * PAIR_ST, PAIR_ST)])


def _prep_pair(pr_ref, btr_ref, bti_ref, ctr_ref, cti_ref, dv_ref, w1_ref, w2_ref, at_ref):
    lane_in = lax.broadcasted_iota(jnp.int32, (1, PAIR_IN), 1)
    step_of_lane = lane_in >> 5
    grp_of_lane = (lane_in >> 4) & 1
    chan_of_lane = lane_in & (SSM_GROUP - 1)
    lane_st = lax.broadcasted_iota(jnp.int32, (1, PAIR_ST), 1)
    grp_of_st_lane = lane_st >> 6
    row_st = lax.broadcasted_iota(jnp.int32, (PAIR_ST, 1), 0)
    grp_of_st_row = row_st >> 6
    on_diag_st = row_st == lane_st
    chan_row =lax.broadcasted_iota(jnp.int32, (SSM_GROUP, 1), 0)

    lane_blk = lax.broadcasted_iota(jnp.int32, (1, LANES), 1)
    step_in_blk = lane_blk >> 5
    spread_ch = jnp.where((lane_blk & (SSM_GROUP - 1)) == chan_row, 1.0, 0.0).astype(bf16)
    same_grp = grp_of_st_row == ((lane_blk >> 4) & 1)
    n_blk = PAIR_IN // LANES

    taps = []
    for d in range(2):
        abr = pr_ref[4 * d + 0:4 * d + 1, :]
        abi = pr_ref[4 * d + 1:4 * d + 2, :]
        bfr = pr_ref[4 * d + 2:4 * d + 3, :]
        bfi = pr_ref[4 * d + 3:4 * d + 4, :]
        a2 = _cmul(abr, abi, abr, abi)
        a4 = _cmul(*a2, *a2)
        a8 = _cmul(*a4, *a4)
        a16 = _cmul(*a8, *a8)
        at_ref[2 * d:2 * d + 1, :] = a16[0]
        at_ref[2 * d + 1:2 * d + 2, :] = a16[1]
        bbr, bbi = _cmul(bfr, bfi, btr_ref[d], bti_ref[d])
        kk_rows = (CHUNK - 1 - chan_row) if d == 0 else chan_row
        pcr, pci = _pow_table(abr, abi, kk_rows, (CHUNK, PAIR_ST))
        pbr = jnp.concatenate([jnp.broadcast_to(pcr[j:j + 1, :], (PAIR_CH, PAIR_ST))
                               for j in range(CHUNK)], axis=0)
        pbi = jnp.concatenate([jnp.broadcast_to(pci[j:j + 1, :], (PAIR_CH, PAIR_ST))
                               for j in range(CHUNK)], axis=0)
        slab_r = jnp.concatenate([jnp.where(grp_of_st_lane == e, bbr, 0.0) for e in range(2)], axis=0)
        slab_i = jnp.concatenate([jnp.where(grp_of_st_lane == e, bbi, 0.0) for e in range(2)], axis=0)
        tile_r = jnp.concatenate([slab_r] * CHUNK, axis=0)
        tile_i = jnp.concatenate([slab_i] * CHUNK, axis=0)
        wbr, wbi = _cmul(pbr, pbi, tile_r, tile_i)
        for part, wb in ((0, wbr), (1, wbi)):
            c0 = PAIR_IN + (2 * d + part) * PAIR_ST
            w1_ref[:, c0:c0 + PAIR_ST] = wb.astype(bf16)

        acr = jnp.sum(jnp.where(on_diag_st, abr, 0.0), axis=1, keepdims=True)
        aci = jnp.sum(jnp.where(on_diag_st, abi, 0.0), axis=1, keepdims=True)
        ctr = jnp.where(same_grp, _dot_select(ctr_ref[d], spread_ch), 0.0)
        cti = jnp.where(same_grp, _dot_select(cti_ref[d], spread_ch), 0.0)
        kk_blk = (step_in_blk + 1) if d == 0 else (STEPS_PER_LANE_BLOCK - step_in_blk)
        m0r, m0i = _cmul(*_pow_table(acr, aci, kk_blk, (PAIR_ST, LANES), bits=3), ctr, cti)
        c2 = _cmul(acr, aci, acr, aci)
        c4 = _cmul(*c2, *c2)
        c8 = _cmul(*c4, *c4)
        c12 = _cmul(*c8, *c4)
        blocks = [(m0r, m0i)] + [_cmul(*cm, m0r, m0i) for cm in (c4, c8, c12)]
        if d == 1:
            blocks = blocks[::-1]
        ca1r = jnp.concatenate([b[0] for b in blocks], axis=1)
        ca1i = jnp.concatenate([b[1] for b in blocks], axis=1)
        ctr4 = jnp.concatenate([ctr] * n_blk, axis=1)
        cti4 = jnp.concatenate([cti] * n_blk, axis=1)
        if d == 0:
            ca0r = jnp.where(lane_in < PAIR_CH, ctr4, pltpu.roll(ca1r, PAIR_CH, axis=1))
            ca0i = jnp.where(lane_in < PAIR_CH, cti4, pltpu.roll(ca1i, PAIR_CH, axis=1))
        else:
            ca0r = jnp.where(lane_in >= PAIR_IN - PAIR_CH, ctr4, pltpu.roll(ca1r, PAIR_IN - PAIR_CH, axis=1))
            ca0i = jnp.where(lane_in >= PAIR_IN - PAIR_CH, cti4, pltpu.roll(ca1i, PAIR_IN - PAIR_CH, axis=1))
        w2_ref[(2 * d) * PAIR_ST:(2 * d + 1) * PAIR_ST, :] = ca1r.astype(bf16)
        w2_ref[(2 * d + 1) * PAIR_ST:(2 * d + 2) * PAIR_ST, :] = (-ca1i).astype(bf16)
        taps.append(_dot3(bbr, ca0r) - _dot3(bbi, ca0i))

    gf, gb = taps
    dvec = dv_ref[...]
    for jp in range(CHUNK):
        lo = PAIR_CH * jp
        hi = PAIR_CH * (jp + 1)
        rf = gf if jp == 0 else pltpu.roll(gf, lo, axis=1)
        rb = gb if jp == CHUNK - 1 else pltpu.roll(gb, hi, axis=1)
        blk = jnp.where(lane_in >= lo, rf, 0.0) + jnp.where(lane_in < hi, rb, 0.0)
        on_diag = (step_of_lane == jp) & (chan_of_lane == chan_row)
        blk = blk + jnp.where(on_diag, dvec, 0.0)
        for e in range(2):
            r0 = lo + e * SSM_GROUP
            w1_ref[r0:r0 + SSM_GROUP, 0:PAIR_IN] = jnp.where(grp_of_lane == e, blk, 0.0).astype(bf16)


def _s5_matrices(prow, bt_re, bt_im, ct_re, ct_im, d_vec, first_f32=()):
    n_q = N_PAIR // PREP_PAIRS
    n_steps = DEPTH * n_q
    cast_in, cast_out, cast_shapes = [], [], []
    for w in first_f32:
        _, rows, cols = w.shape
        n_blk = next(n for n in (n_steps, n_steps // 2, n_steps // 4)
                     if rows % n == 0 and (rows // n) % 16 == 0)
        idx = lambda l, q, n_blk=n_blk: (jnp.minimum(l * n_q + q, n_blk - 1), 0)
        cast_in.append(pl.BlockSpec((None, rows // n_blk, cols), lambda l, q, idx=idx: (0,) + idx(l, q)))
        cast_out.append(pl.BlockSpec((rows // n_blk, cols), idx))
        cast_shapes.append(jax.ShapeDtypeStruct((rows, cols), bf16))
    pp = PREP_PAIRS
    res = pl.pallas_call(
        _prep_kernel,
        grid=(DEPTH, n_q),
        in_specs=[
            pl.BlockSpec((None, pp, 8, PAIR_ST), lambda l, q: (l, q, 0, 0)),
            pl.BlockSpec((None, 2, pp, SSM_GROUP, PAIR_ST), lambda l, q: (l, 0, q, 0, 0)),
            pl.BlockSpec((None, 2, pp, SSM_GROUP, PAIR_ST), lambda l, q: (l, 0, q, 0, 0)),
            pl.BlockSpec((None, 2, pp, PAIR_ST, SSM_GROUP), lambda l, q: (l, 0, q, 0, 0)),
            pl.BlockSpec((None, 2, pp, PAIR_ST, SSM_GROUP), lambda l, q: (l, 0, q, 0, 0)),
            pl.BlockSpec((None, pp, 1, PAIR_IN), lambda l, q: (l, q, 0, 0)),
        ] + cast_in,
        out_specs=[
            pl.BlockSpec((None, pp, PAIR_IN, PAIR_IN + 4 * PAIR_ST), lambda l, q: (l, q, 0, 0)),
            pl.BlockSpec((None, pp, 4 * PAIR_ST, PAIR_IN), lambda l, q: (l, q, 0, 0)),
            pl.BlockSpec((None, 4, pp * PAIR_ST), lambda l, q: (l, 0, q)),
        ] + cast_out,
        out_shape=[
            jax.ShapeDtypeStruct((DEPTH, N_PAIR, PAIR_IN, PAIR_IN + 4 * PAIR_ST), bf16),
            jax.ShapeDtypeStruct((DEPTH, N_PAIR, 4 * PAIR_ST, PAIR_IN), bf16),
            jax.ShapeDtypeStruct((DEPTH, 4, SSM_GROUPS * SSM_STATE), f32),
        ] + cast_shapes,
        compiler_params=_params("arbitrary", "arbitrary"),
        name="s5_chunk_matrices",
    )(prow, bt_re, bt_im, ct_re, ct_im, d_vec, *first_f32)
    return res[0], res[1], res[2], tuple(res[3:])


def _chunk_scan(pr, pi, pows, n_rows, seq, reverse):
    pos = lax.broadcasted_iota(jnp.int32, (n_rows, 1), 0) & (seq - 1)
    shift, k = 1, 0
    while shift < seq:
        ar, ai = pows[k]
        if reverse:
            valid = pos < seq - shift
            sr = pltpu.roll(pr, n_rows - shift, axis=0)
            si = pltpu.roll(pi, n_rows - shift, axis=0)
        else:
            valid = pos >= shift
            sr = pltpu.roll(pr, shift, axis=0)
            si = pltpu.roll(pi, shift, axis=0)
        tr, ti = _cmul(ar, ai, sr, si)
        pr = pr + jnp.where(valid, tr, 0.0)
        pi = pi + jnp.where(valid, ti, 0.0)
        shift, k = 2 * shift, k + 1
    return pr, pi


def _s5_kernel(*refs):
    a_refs = refs[0:CHUNK]
    w1_ref, w2_ref, at_ref, h0_ref = refs[CHUNK:CHUNK + 4]
    y_ref, fin_ref = refs[CHUNK + 4:CHUNK + 6]
    r_scr, sp_scr, y_scr, sr_scr, si_scr = refs[CHUNK + 6:]

    lane_blk = lax.broadcasted_iota(jnp.int32, (1, LANES), 1) // PAIR_CH
    row_lat = lax.broadcasted_iota(jnp.int32, (ROWS_LAT, 1), 0)
    pos_ctx = lax.broadcasted_iota(jnp.int32, (ROWS_CTX, 1), 0) & (NC_CTX - 1)
    pos_lat = row_lat & (NC_LAT - 1)

    for p in range(PAIRS_PER_STEP):
        pieces = []
        for k in range(CHUNK // STEPS_PER_LANE_BLOCK):
            piece = None
            for jj in range(STEPS_PER_LANE_BLOCK):
                src = a_refs[STEPS_PER_LANE_BLOCK * k + jj][...]
                sh = (PAIR_CH * (jj - p)) % LANES
                rolled = src if sh == 0 else pltpu.roll(src, sh, axis=1)
                piece = rolled if piece is None else jnp.where(lane_blk == jj, rolled, piece)
            pieces.append(piece.astype(bf16))
        u_pair = jnp.concatenate(pieces, axis=1)
        r_scr[...] = _dot(u_pair, w1_ref[p])

        lanes_p = slice(p * PAIR_ST, (p + 1) * PAIR_ST)
        for d in range(2):
            pows = [(at_ref[2 * d:2 * d + 1, lanes_p], at_ref[2 * d + 1:2 * d + 2, lanes_p])]
            for _ in range(5):
                pows.append(_cmul(*pows[-1], *pows[-1]))
            c_re = PAIR_IN + 2 * d * PAIR_ST
            c_im = c_re + PAIR_ST
            reverse = d == 1

            sr, si = _chunk_scan(r_scr[0:ROWS_CTX, c_re:c_re + PAIR_ST],
                                 r_scr[0:ROWS_CTX, c_im:c_im + PAIR_ST],
                                 pows, ROWS_CTX, NC_CTX, reverse)
            sr_scr[...] = sr
            si_scr[...] = si
            last = 0 if reverse else NC_CTX - 1
            fin_ref[2 * d, :, lanes_p] = sr_scr[pl.ds(last, BATCH, stride=NC_CTX), :]
            fin_ref[2 * d + 1, :, lanes_p] = si_scr[pl.ds(last, BATCH, stride=NC_CTX), :]
            edge = pos_ctx == (NC_CTX - 1 if reverse else 0)
            back = ROWS_CTX - 1 if reverse else 1
            sp_scr[0:ROWS_CTX, 2 * d * PAIR_ST:(2 * d + 1) * PAIR_ST] = jnp.where(
                edge, 0.0, pltpu.roll(sr, back, axis=0))
            sp_scr[0:ROWS_CTX, (2 * d + 1) * PAIR_ST:(2 * d + 2) * PAIR_ST] = jnp.where(
                edge, 0.0, pltpu.roll(si, back, axis=0))

            h0r = jnp.where(row_lat < NC_LAT, h0_ref[2 * d, 0:1, lanes_p], h0_ref[2 * d, 1:2, lanes_p])
            h0i = jnp.where(row_lat < NC_LAT, h0_ref[2 * d + 1, 0:1, lanes_p],
                            h0_ref[2 * d + 1, 1:2, lanes_p])
            edge = pos_lat == (NC_LAT - 1 if reverse else 0)
            ahr, ahi = _cmul(*pows[0], h0r, h0i)
            pr = r_scr[ROWS_CTX:ROWS_S5, c_re:c_re + PAIR_ST] + jnp.where(edge, ahr, 0.0)
            pi = r_scr[ROWS_CTX:ROWS_S5, c_im:c_im + PAIR_ST] + jnp.where(edge, ahi, 0.0)
            sr, si = _chunk_scan(pr, pi, pows, ROWS_LAT, NC_LAT, reverse)
            back = ROWS_LAT - 1 if reverse else 1
            sp_scr[ROWS_CTX:ROWS_S5, 2 * d * PAIR_ST:(2 * d + 1) * PAIR_ST] = jnp.where(
                edge, h0r, pltpu.roll(sr, back, axis=0))
            sp_scr[ROWS_CTX:ROWS_S5, (2 * d + 1) * PAIR_ST:(2 * d + 2) * PAIR_ST] = jnp.where(
                edge, h0i, pltpu.roll(si, back, axis=0))

        y_scr[:, p * PAIR_IN:(p + 1) * PAIR_IN] = (
            r_scr[:, 0:PAIR_IN] + _dot(sp_scr[...].astype(bf16), w2_ref[p]))

    for j in range(CHUNK):
        k, jj = divmod(j, STEPS_PER_LANE_BLOCK)
        out = None
        for p in range(PAIRS_PER_STEP):
            src = y_scr[:, p * PAIR_IN + k * LANES:p * PAIR_IN + (k + 1) * LANES]
            sh = (PAIR_CH * (p - jj)) % LANES
            rolled = src if sh == 0 else pltpu.roll(src, sh, axis=1)
            out = rolled if out is None else jnp.where(lane_blk == p, rolled, out)
        y_ref[:, j, :] = out


def _s5_mixer(u_rows, w1, w2, a_chunk, h0, layer):
    n_state = SSM_GROUPS * SSM_STATE
    n_blk = SSM_WIDTH // LANES
    step_spec = lambda j: pl.BlockSpec((None, ROWS_S5, LANES), lambda g, j=j: (j, 0, g))
    w_pairs = lambda rows, cols: pl.BlockSpec((None, PAIRS_PER_STEP, rows, cols),
                                              lambda g: (layer, g, 0, 0))
    st_lanes = PAIRS_PER_STEP * PAIR_ST
    return pl.pallas_call(
        _s5_kernel,
        grid=(n_blk,),
        in_specs=[step_spec(j) for j in range(CHUNK)] + [
            w_pairs(PAIR_IN, PAIR_IN + 4 * PAIR_ST),
            w_pairs(4 * PAIR_ST, PAIR_IN),
            pl.BlockSpec((None, 4, st_lanes), lambda g: (layer, 0, g)),
            pl.BlockSpec((4, DEC_BATCH, st_lanes), lambda g: (0, 0, g)),
        ],
        out_specs=[
            pl.BlockSpec((ROWS_S5, CHUNK, LANES), lambda g: (0, 0, g)),
            pl.BlockSpec((4, BATCH, st_lanes), lambda g: (0, 0, g)),
        ],
        out_shape=[
            jax.ShapeDtypeStruct((ROWS_S5, CHUNK, SSM_WIDTH), f32),
            jax.ShapeDtypeStruct((4, BATCH, n_state), f32),
        ],
        scratch_shapes=[
            pltpu.VMEM((ROWS_S5, PAIR_IN + 4 * PAIR_ST), f32),
            pltpu.VMEM((ROWS_S5, 4 * PAIR_ST), f32),
            pltpu.VMEM((ROWS_S5, PAIRS_PER_STEP * PAIR_IN), f32),
            pltpu.VMEM((ROWS_CTX, PAIR_ST), f32),
            pltpu.VMEM((ROWS_CTX, PAIR_ST), f32),
        ],
        compiler_params=_params("arbitrary"),
        name="s5_chunked_mixer",
    )(*([u_rows] * CHUNK), w1, w2, a_chunk, h0)


def _shift_rows(v, period, n_rows, shift):
    row = lax.broadcasted_iota(jnp.int32, (n_rows, 1), 0)
    pos = row & (period - 1)
    prev = jnp.where(pos < shift, 0.0, pltpu.roll(v, shift, axis=0))
    nxt = jnp.where(pos >= period - shift, 0.0, pltpu.roll(v, n_rows - shift, axis=0))
    return prev, nxt


def _dwconv3_rows(v, w_ref, period, n_rows, shift=1):
    prev, nxt = _shift_rows(v, period, n_rows, shift)
    return prev * w_ref[0:1, :] + v * w_ref[1:2, :] + nxt * w_ref[2:3, :]


def _dwconv3_rows_static(v, w, period, n_rows, shift, window=None):
    w0, w1, w2 = w[0:1, :], w[1:2, :], w[2:3, :]
    if shift % 8 == 0 and period == n_rows:
        zeros = jnp.zeros((shift, v.shape[1]), v.dtype)
        prev = jnp.concatenate([zeros, v[:n_rows - shift]], axis=0)
        nxt = jnp.concatenate([v[shift:], zeros], axis=0)
        return prev * w0 + v * w1 + nxt * w2
    if window is not None:
        pad = jnp.zeros((8, v.shape[1]), v.dtype)
        window[0:8, :] = pad
        window[8 + n_rows:16 + n_rows, :] = pad
        window[8:8 + n_rows, :] = v
        prev = window[8 - shift:8 - shift + n_rows, :]
        nxt = window[8 + shift:8 + shift + n_rows, :]
    else:
        prev = pltpu.roll(v, shift, axis=0)
        nxt = pltpu.roll(v, n_rows - shift, axis=0)
    out = prev * w0 + v * w1 + nxt * w2
    sub = lax.broadcasted_iota(jnp.int32, (8, 1), 0)
    pieces = []
    for lo in range(0, n_rows, period):
        hi = lo + period
        first = (jnp.where(sub < shift, 0.0, prev[lo:lo + 8]) * w0 + v[lo:lo + 8] * w1
                 + nxt[lo:lo + 8] * w2)
        last = (prev[hi - 8:hi] * w0 + v[hi - 8:hi] * w1
                + jnp.where(sub >= 8 - shift, 0.0, nxt[hi - 8:hi]) * w2)
        pieces += [first, out[lo + 8:hi - 8], last]
    return jnp.concatenate(pieces, axis=0)


def _k1_kernel(*refs, split_input):
    if split_input:
        xp_ref, xs_ref = refs[:2]
        refs = refs[2:]
    else:
        x_ref = refs[0]
        refs = refs[1:]
    mod_ref, gpre_ref, wgb_ref, wgc_ref, whv_ref, wu_ref, cw_ref, gco_ref, cn_ref, u_ref = refs[:10]
    if split_input:
        x_ref, h_scr, u_scr = refs[10:]
    else:
        h_scr, u_scr = refs[10:]
    i = pl.program_id(0)
    n = pl.program_id(1)
    tm = cn_ref.shape[0]
    n_ctx_tiles = N_CTX // tm

    @pl.when(n == 0)
    def _():
        if split_input:
            @pl.when(i < n_ctx_tiles)
            def _():
                x_ref[...] = xp_ref[...]

            @pl.when(i >= n_ctx_tiles)
            def _():
                x_ref[...] = xs_ref[...]

        r = jnp.where(i < n_ctx_tiles, 0, 1 + (i - n_ctx_tiles) // (DEC_SEQ // tm))
        sh1 = mod_ref[pl.ds(r, 1), 0:D_MODEL]
        sc1 = mod_ref[pl.ds(r, 1), D_MODEL:2 * D_MODEL]
        h_scr[...] = (_rms_rows(x_ref[...], gpre_ref[...] * (1.0 + sc1)) + sh1).astype(bf16)

    h = h_scr[...]
    u = _dot(h, wu_ref[...])
    for t in range(TC1 // LANES):
        u_scr[t] = u[:, t * LANES:(t + 1) * LANES]
    for j in range(CHUNK):
        for t in range(TC1 // LANES):
            u_ref[j, :, t * LANES:(t + 1) * LANES] = u_scr[t, pl.ds(j, tm // CHUNK, stride=CHUNK), :]
    v = _dot(h, wgc_ref[...]) * _dot(h, whv_ref[...])
    period = jnp.where(i < n_ctx_tiles, SEQ, GRID_W)
    v = _dwconv3_rows(v, cw_ref, period, tm)
    co = _dot(h, wgb_ref[...]) * v
    rr = lax.broadcasted_iota(jnp.int32, (TC1, TC1), 0) // HEAD_DIM
    cc = lax.broadcasted_iota(jnp.int32, (TC1, TC1), 1) // HEAD_DIM
    avg = jnp.where(rr == cc, 1.0 / HEAD_DIM, 0.0).astype(bf16)
    hi, lo = _split_bf16(co * co)
    ms = _dot(hi, avg) + _dot(lo, avg)
    cn_ref[...] = (co * lax.rsqrt(ms + EPS) * gco_ref[...]).astype(bf16)


def _in_proj(xs, mod, g_pre1, w_in_bf, conv_w, g_conv_out, layer):
    nb = CONV_WIDTH // TC1
    split_input = len(xs) == 2
    tm = TM1_SPLIT if split_input else TM1
    n_ctx_tiles = N_CTX // tm
    wspec = lambda off: pl.BlockSpec((D_MODEL, TC1), lambda i, n: (0, off * nb + n))
    x_spec = pl.BlockSpec((tm, D_MODEL), lambda i, n: (i, 0))
    if split_input:
        x_specs = [pl.BlockSpec((tm, D_MODEL), lambda i, n: (jnp.minimum(i, n_ctx_tiles - 1), 0)),
                   pl.BlockSpec((tm, D_MODEL), lambda i, n: (jnp.maximum(i - n_ctx_tiles, 0), 0))]
    else:
        x_specs = [x_spec]
    return pl.pallas_call(
        functools.partial(_k1_kernel, split_input=split_input),
        grid=(N_TOK // tm, nb),
        in_specs=x_specs + [
            pl.BlockSpec((None, 8, 6 * D_MODEL), lambda i, n: (layer, 0, 0)),
            pl.BlockSpec((None, 1, D_MODEL), lambda i, n: (layer, 0, 0)),
            wspec(0), wspec(1), wspec(2), wspec(3),
            pl.BlockSpec((None, 3, TC1), lambda i, n: (layer, 0, n)),
            pl.BlockSpec((None, 1, TC1), lambda i, n: (layer, 0, n)),
        ],
        out_specs=[
            pl.BlockSpec((tm, TC1), lambda i, n: (i, n)),
            pl.BlockSpec((CHUNK, tm // CHUNK, TC1), lambda i, n: (0, i, n)),
        ] + ([x_spec] if split_input else []),
        out_shape=[
            jax.ShapeDtypeStruct((N_TOK, CONV_WIDTH), bf16),
            jax.ShapeDtypeStruct((CHUNK, ROWS_S5, SSM_WIDTH), f32),
        ] + ([jax.ShapeDtypeStruct((N_TOK, D_MODEL), f32)] if split_input else []),
        scratch_shapes=[pltpu.VMEM((tm, D_MODEL), bf16),
                        pltpu.VMEM((TC1 // LANES, tm, LANES), f32)],
        compiler_params=_params("arbitrary", "arbitrary"),
        name="in_proj_conv_mixer",
    )(*xs, mod, g_pre1, w_in_bf, w_in_bf, w_in_bf, w_in_bf, conv_w, g_conv_out)


def _k3_kernel(x_ref, cn_ref, y_ref, mod_ref, wglu_ref, bglu_ref, gsso_ref, wout_ref,
               gpost1_ref, gpre2_ref, x1_ref, h2_ref):
    i = pl.program_id(0)
    n_ctx_tiles = N_CTX // TM3
    r = jnp.where(i < n_ctx_tiles, 0, 1 + (i - n_ctx_tiles) // (DEC_SEQ // TM3))
    gt1 = mod_ref[pl.ds(r, 1), 2 * D_MODEL:3 * D_MODEL]
    sh2 = mod_ref[pl.ds(r, 1), 3 * D_MODEL:4 * D_MODEL]
    sc2 = mod_ref[pl.ds(r, 1), 4 * D_MODEL:5 * D_MODEL]

    gain1 = gt1 * gpost1_ref[...]
    gain2 = gpre2_ref[...] * (1.0 + sc2)
    for s in range(TM3 // SUB3):
        rows = slice(s * SUB3, (s + 1) * SUB3)
        z = jax.nn.gelu(y_ref[rows, :], approximate=True)
        gate = jax.nn.sigmoid(_dot(z.astype(bf16), wglu_ref[...]) + bglu_ref[...])
        sn = _rms_rows(z * gate, gsso_ref[...])
        mixed = (_dot(cn_ref[rows, :], wout_ref[0:CONV_WIDTH, :])
                 + _dot(sn.astype(bf16), wout_ref[CONV_WIDTH:CONV_WIDTH + SSM_WIDTH, :]))
        x1 = x_ref[rows, :] + _rms_rows(mixed, gain1)
        x1_ref[rows, :] = x1
        h2_ref[rows, :] = (_rms_rows(x1, gain2) + sh2).astype(bf16)


def _out_proj(x, cn, y, mod, w_glu_bf, b_glu, g_ssm_out, w_out_bf, g_post1, g_pre2, layer):
    vec = lambda width: pl.BlockSpec((None, 1, width), lambda i: (layer, 0, 0))
    return pl.pallas_call(
        _k3_kernel,
        grid=(N_TOK // TM3,),
        in_specs=[
            pl.BlockSpec((TM3, D_MODEL), lambda i: (i, 0)),
            pl.BlockSpec((TM3, CONV_WIDTH), lambda i: (i, 0)),
            pl.BlockSpec((TM3, SSM_WIDTH), lambda i: (i, 0)),
            pl.BlockSpec((None, 8, 6 * D_MODEL), lambda i: (layer, 0, 0)),
            pl.BlockSpec((SSM_WIDTH, SSM_WIDTH), lambda i: (0, 0)),
            vec(SSM_WIDTH), vec(SSM_WIDTH),
            pl.BlockSpec((CONV_WIDTH + SSM_WIDTH, D_MODEL), lambda i: (0, 0)),
            vec(D_MODEL), vec(D_MODEL),
        ],
        out_specs=[
            pl.BlockSpec((TM3, D_MODEL), lambda i: (i, 0)),
            pl.BlockSpec((TM3, D_MODEL), lambda i: (i, 0)),
        ],
        out_shape=[
            jax.ShapeDtypeStruct((N_TOK, D_MODEL), f32),
            jax.ShapeDtypeStruct((N_TOK, D_MODEL), bf16),
        ],
        compiler_params=_params("arbitrary"),
        name="glu_out_proj",
    )(x, cn, y, mod, w_glu_bf, b_glu, g_ssm_out, w_out_bf, g_post1, g_pre2)


def _k4_kernel(h_ref, x1_ref, mod_ref, wa_ref, wg_ref, cwa_ref, cwg_ref, wd_ref, gpost2_ref,
               *rest, latent):
    windows = (None, None)
    if not latent:
        windows, rest = rest[-2:], rest[:-2]
    n_cast = (len(rest) - 1) // 2
    o_ref = rest[n_cast]
    for src, dst in zip(rest[:n_cast], rest[n_cast + 1:]):
        dst[...] = src[...].astype(bf16)

    i = pl.program_id(0)
    j = pl.program_id(1)
    blk = o_ref.shape

    @pl.when(j == 0)
    def _():
        o_ref[...] = jnp.zeros(blk, f32)

    period, shift = (TM4, LAT_STRIP) if latent else (SEQ, 1)
    h = h_ref[...].reshape(TM4, D_MODEL)
    acts = []
    for sub in range(TF4 // SUB4):
        cols = slice(sub * SUB4, (sub + 1) * SUB4)
        a = _dwconv3_rows_static(_dot(h, wa_ref[:, cols]), cwa_ref[:, cols], period, TM4, shift,
                                 windows[0])
        g = _dwconv3_rows_static(_dot(h, wg_ref[:, cols]), cwg_ref[:, cols], period, TM4, shift,
                                 windows[1])
        acts.append((g * jax.nn.sigmoid(g) * a).astype(bf16))
    o_ref[...] += _dot(jnp.concatenate(acts, axis=1), wd_ref[...]).reshape(blk)

    @pl.when(j == pl.num_programs(1) - 1)
    def _():
        r = (1 + i // (GRID_W // LAT_STRIP)) if latent else 0
        gt2 = mod_ref[pl.ds(r, 1), 5 * D_MODEL:6 * D_MODEL]
        ff = o_ref[...].reshape(TM4, D_MODEL)
        x2 = x1_ref[...].reshape(TM4, D_MODEL) + _rms_rows(ff, gt2 * gpost2_ref[...])
        o_ref[...] = x2.reshape(blk)


def _conv_ffn(h2, x1, mod, w_up_bf, ffn_conv_w, w_down_bf, g_post2, layer, next_f32=(), last=False):
    nj = D_FF // TF4
    n_ctx_tiles = N_CTX // TM4

    weight_specs = [
        pl.BlockSpec((None, 8, 6 * D_MODEL), lambda i, j: (layer, 0, 0)),
        pl.BlockSpec((D_MODEL, TF4), lambda i, j: (0, j)),
        pl.BlockSpec((D_MODEL, TF4), lambda i, j: (0, nj + j)),
        pl.BlockSpec((None, 3, TF4), lambda i, j: (layer, 0, j)),
        pl.BlockSpec((None, 3, TF4), lambda i, j: (layer, 0, nj + j)),
        pl.BlockSpec((TF4, D_MODEL), lambda i, j: (j, 0)),
        pl.BlockSpec((None, 1, D_MODEL), lambda i, j: (layer, 0, 0)),
    ]
    weights = (mod, w_up_bf, w_up_bf, ffn_conv_w, ffn_conv_w, w_down_bf, g_post2)

    cast_in, cast_out, cast_shapes = [], [], []
    for w in next_f32:
        _, rows, cols = w.shape
        if cols % nj == 0:
            blk = (rows // n_ctx_tiles, cols // nj)
            idx = lambda i, j: (i, j)
        elif rows % nj == 0:
            blk = (rows // nj, cols // n_ctx_tiles)
            idx = lambda i, j: (j, i)
        else:
            blk = (rows // n_ctx_tiles, cols // n_ctx_tiles)
            idx = lambda i, j: (i, jnp.minimum(j, n_ctx_tiles - 1))
        cast_in.append(pl.BlockSpec((None,) + blk, lambda i, j, idx=idx: (layer + 1,) + idx(i, j)))
        cast_out.append(pl.BlockSpec(blk, idx))
        cast_shapes.append(jax.ShapeDtypeStruct((rows, cols), bf16))
    row_spec = pl.BlockSpec((TM4, D_MODEL), lambda i, j: (i, 0))
    res = pl.pallas_call(
        functools.partial(_k4_kernel, latent=False),
        grid=(n_ctx_tiles, nj),
        in_specs=[row_spec, row_spec] + weight_specs + cast_in,
        out_specs=[row_spec] + cast_out,
        out_shape=[jax.ShapeDtypeStruct((N_CTX if last else N_TOK, D_MODEL), f32)] + cast_shapes,
        scratch_shapes=[pltpu.VMEM((TM4 + 16, SUB4), f32)] * 2,
        input_output_aliases={} if last else {1: 0},
        compiler_params=_params("arbitrary", "arbitrary"),
        name="conv_ffn_ctx",
    )(h2, x1, *weights, *next_f32)

    view = lambda t: t.reshape(-1, GRID_W, D_MODEL)
    rows_per_img = DEC_SEQ // GRID_W
    strips = GRID_W // LAT_STRIP
    first_img = N_CTX // DEC_SEQ
    strip = (rows_per_img, LAT_STRIP, D_MODEL)
    strip_spec = lambda img0: pl.BlockSpec(strip, lambda i, j: (img0 + i // strips, i % strips, 0))
    x2_lat = pl.pallas_call(
        functools.partial(_k4_kernel, latent=True),
        grid=(DEC_BATCH * strips, nj),
        in_specs=[strip_spec(first_img), strip_spec(first_img)] + weight_specs,
        out_specs=strip_spec(0 if last else first_img),
        out_shape=jax.ShapeDtypeStruct(((N_LAT if last else N_TOK) // GRID_W, GRID_W, D_MODEL), f32),
        input_output_aliases={} if last else {1: 0},
        compiler_params=_params("arbitrary", "arbitrary"),
        name="conv_ffn_lat",
    )(view(h2), view(x1 if last else res[0]), *weights)
    if last:
        return (res[0], x2_lat.reshape(N_LAT, D_MODEL)), ()
    return x2_lat.reshape(N_TOK, D_MODEL), tuple(res[1:])


def kernel(x_prompt, x_sample, state_ssm_re, state_ssm_im, c, c_ctx, w_ada, b_ada, g_pre1, w_in, conv_w, ssm_a_re, ssm_a_im, ssm_log_dt, ssm_b_re, ssm_b_im, ssm_c_re, ssm_c_im, ssm_d, w_glu, b_glu, g_conv_out, g_ssm_out, w_out, g_post1, g_pre2, w_up, ffn_conv_w, w_down, g_post2):
    f32_weights = (w_in, w_glu, w_out, w_up, w_down)

    cvec8 = jnp.concatenate([c_ctx[None, :], c, jnp.zeros((8 - 1 - DEC_BATCH, D_MODEL), f32)], axis=0)
    mod = _modulation(cvec8, w_ada, b_ada)

    abr, abi, bfr, bfi = [t.reshape(DEPTH, 2, N_PAIR, PAIR_ST)
                          for t in _discretise(ssm_a_re, ssm_a_im, ssm_log_dt)]
    prow = jnp.stack([abr, abi, bfr, bfi], axis=2)
    prow = prow.transpose(0, 3, 1, 2, 4).reshape(DEPTH, N_PAIR, 8, PAIR_ST)

    def b_rows(b):
        b = b.transpose(0, 1, 2, 4, 3).reshape(DEPTH, 2, N_PAIR, 2, SSM_GROUP, SSM_STATE)
        return b.transpose(0, 1, 2, 4, 3, 5).reshape(DEPTH, 2, N_PAIR, SSM_GROUP, PAIR_ST)

    def c_cols(cm):
        return cm.transpose(0, 1, 2, 4, 3).reshape(DEPTH, 2, N_PAIR, PAIR_ST, SSM_GROUP)

    d_vec = jnp.broadcast_to(ssm_d.reshape(DEPTH, N_PAIR, 1, PAIR_CH), (DEPTH, N_PAIR, CHUNK, PAIR_CH))
    d_vec = d_vec.reshape(DEPTH, N_PAIR, 1, PAIR_IN)
    w1, w2, a_chunk, bf_weights = _s5_matrices(
        prow, b_rows(ssm_b_re), b_rows(ssm_b_im), c_cols(ssm_c_re), c_cols(ssm_c_im), d_vec,
        first_f32=f32_weights)

    g3 = lambda g: g.reshape(DEPTH, 1, -1)
    g_pre1_, g_conv_out_, g_ssm_out_, g_post1_, g_pre2_, g_post2_, b_glu_ = map(
        g3, (g_pre1, g_conv_out, g_ssm_out, g_post1, g_pre2, g_post2, b_glu))

    xs = (x_prompt.reshape(N_CTX, D_MODEL), x_sample.reshape(N_LAT, D_MODEL))
    fins = []
    for l in range(DEPTH):
        last = l + 1 == DEPTH
        w_in_bf, w_glu_bf, w_out_bf, w_up_bf, w_down_bf = bf_weights
        res = _in_proj(xs, mod, g_pre1_, w_in_bf, conv_w, g_conv_out_, l)
        cn, u = res[:2]
        x = res[2] if len(xs) == 2 else xs[0]
        sre = state_ssm_re[:, l].reshape(DEC_BATCH, 2, -1)
        sim = state_ssm_im[:, l].reshape(DEC_BATCH, 2, -1)
        h0 = jnp.stack([sre[:, 0], sim[:, 0], sre[:, 1], sim[:, 1]], axis=0)
        y, fin = _s5_mixer(u, w1, w2, a_chunk, h0, l)
        fins.append(fin)
        x1, h2 = _out_proj(x, cn, y.reshape(N_TOK, SSM_WIDTH), mod, w_glu_bf, b_glu_, g_ssm_out_,
                           w_out_bf, g_post1_, g_pre2_, l)
        x, bf_weights = _conv_ffn(h2, x1, mod, w_up_bf, ffn_conv_w, w_down_bf, g_post2_, l,
                                  next_f32=() if last else f32_weights, last=last)
        xs = (x,)

    fin = jnp.stack(fins, axis=0).reshape(DEPTH, 2, 2, BATCH, SSM_GROUPS, SSM_STATE)
    new_re = fin[:, :, 0].transpose(2, 0, 1, 3, 4)
    new_im = fin[:, :, 1].transpose(2, 0, 1, 3, 4)
    y_prompt, y_sample = x
    return (y_prompt.reshape(BATCH, SEQ, D_MODEL), y_sample.reshape(DEC_BATCH, DEC_SEQ, D_MODEL),
            new_re, new_im)
```
